```python
import math
import jax, jax.numpy as jnp
from jax import lax
import numpy as np


D_MODEL = 2048
BATCH = 4
SEQ = 2048
DEPTH = 2
DEC_BATCH = 128
DEC_SEQ = 8
PAST_LEN = 16384
PAGE_SIZE = 128

N_AB_LAYERS = (DEPTH + 1) // 2
N_C_LAYERS = DEPTH // 2
D_CONV = D_MODEL // 2
CONV_WIDTH = 3
D_CHUNK = D_MODEL // 2
CHUNK_HEADS = 8
CHUNK_HEAD_DIM = D_CHUNK // CHUNK_HEADS
CHUNK = 128
D_AB_IN = 3 * D_CONV + 2 * D_CHUNK
D_AB_OUT = D_CONV + D_CHUNK
GLA_HEADS = 4
GLA_DK = (D_MODEL // 2) // GLA_HEADS
GLA_DV = D_MODEL // GLA_HEADS
GLA_DK_TOT = GLA_HEADS * GLA_DK
GLA_DV_TOT = GLA_HEADS * GLA_DV
GLA_GATE_RANK = 16
GLA_GATE_NORMALIZER = 16.0
GLA_BLOCK = 64
D_GLA_IN = 2 * GLA_DK_TOT + 2 * GLA_DV_TOT + GLA_GATE_RANK
N_GROUPS = 4
EXPERTS_PER_GROUP = 4
TOP_K_INNER = 2
D_EXPERT = 512
ALPHA = (2 * DEPTH) ** 0.25
BETA = (8 * DEPTH) ** -0.25
LN_EPS = 1e-5
RMS_EPS = 1e-6

kernel_name = 'hybrid_conv_chunkmlp_gla_hmoe_step'


def layer_norm(x, g, b):
    xf = x.astype(jnp.float32)
    mu = jnp.mean(xf, axis=-1, keepdims=True)
    var = jnp.mean(jnp.square(xf - mu), axis=-1, keepdims=True)
    return ((xf - mu) * lax.rsqrt(var + LN_EPS) * g + b).astype(x.dtype)


def conv_chunk_mixer(h, conv_buf, w_in, conv_w, v_g, v_b, w_s, b_s, w_out):
    bn, seq_len, _ = h.shape
    proj = h @ w_in
    b_gate, c_gate, hx, u, v = jnp.split(
        proj, [D_CONV, 2 * D_CONV, 3 * D_CONV, 3 * D_CONV + D_CHUNK], axis=-1)
    z = c_gate * hx
    zc = jnp.concatenate([conv_buf.astype(z.dtype), z], axis=1)
    conv = sum(conv_w[k] * zc[:, k:k + seq_len] for k in range(CONV_WIDTH))
    y_conv = b_gate * conv
    new_buf = zc[:, seq_len:]
    v_n = layer_norm(v, v_g, v_b)
    cl = min(seq_len, CHUNK)
    n_chunks = seq_len // cl
    v4 = v_n.reshape(bn, n_chunks, cl, CHUNK_HEADS, CHUNK_HEAD_DIM)
    w_causal = jnp.tril(w_s[:, :cl, :cl])
    mixed = jnp.einsum('hts,bnshd->bnthd', w_causal, v4) + b_s[:, :cl].T[:, :, None]
    y_chunk = u * mixed.reshape(bn, seq_len, D_CHUNK)
    y = jnp.concatenate([y_conv, y_chunk], axis=-1) @ w_out
    return y, new_buf, v_n


def gla_recurrence(q, k, v, log_a, s0):
    bn, seq_len = q.shape[:2]
    blk = math.gcd(seq_len, GLA_BLOCK)
    n_blk = seq_len // blk

    def to_blocks(t):
        return t.reshape(bn, n_blk, blk, GLA_HEADS, t.shape[-1]).transpose(1, 0, 3, 2, 4).astype(jnp.float32)

    qb, kb, vb, ab = to_blocks(q), to_blocks(k), to_blocks(v), to_blocks(log_a)
    causal = jnp.tril(jnp.ones((blk, blk), dtype=bool))

    def step(s, inp):
        qc, kc, vc, ac = inp
        cum = jnp.cumsum(ac, axis=2)
        diff = cum[:, :, :, None, :] - cum[:, :, None, :, :]
        decay = jnp.exp(jnp.where(causal[:, :, None], diff, -jnp.inf))
        scores = jnp.einsum('bhtd,bhsd,bhtsd->bhts', qc, kc, decay)
        o = (jnp.einsum('bhts,bhsv->bhtv', scores, vc)
             + jnp.einsum('bhtd,bhdv->bhtv', qc * jnp.exp(cum), s))
        last = cum[:, :, -1:, :]
        s_new = (jnp.exp(last[:, :, 0, :])[..., None] * s
                 + jnp.einsum('bhsd,bhsv->bhdv', kc * jnp.exp(last - cum), vc))
        return s_new, o

    s_fin, ob = lax.scan(step, s0.astype(jnp.float32), (qb, kb, vb, ab))
    o = ob.transpose(1, 0, 3, 2, 4).reshape(bn, seq_len, GLA_HEADS, GLA_DV)
    return o, s_fin


def gla_mixer(h, s0, w_in, w_gk, b_gk, norm_g, w_out):
    bn, seq_len, _ = h.shape
    proj = h @ w_in
    q, k, v, g, gk_low = jnp.split(
        proj, [GLA_DK_TOT, 2 * GLA_DK_TOT, 2 * GLA_DK_TOT + GLA_DV_TOT, 2 * GLA_DK_TOT + 2 * GLA_DV_TOT], axis=-1)
    log_a = jax.nn.log_sigmoid((gk_low @ w_gk + b_gk).astype(jnp.float32)) / GLA_GATE_NORMALIZER
    q = q * (GLA_DK ** -0.5)
    hd = lambda t, d: t.reshape(bn, seq_len, GLA_HEADS, d)
    o, s_fin = gla_recurrence(hd(q, GLA_DK), hd(k, GLA_DK), hd(v, GLA_DV), hd(log_a, GLA_DK), s0)
    o_n = o * lax.rsqrt(jnp.mean(jnp.square(o), axis=-1, keepdims=True) + RMS_EPS) * norm_g
    y = (o_n.reshape(bn, seq_len, GLA_DV_TOT) * jax.nn.silu(g.astype(jnp.float32))).astype(h.dtype)
    return y @ w_out, s_fin


def hier_moe(h, w_grp, b_grp, w_rt, b_rt, w1, w3, w2):
    bn, seq_len, d = h.shape
    t = h.reshape(-1, d)
    g_prob = jax.nn.softmax((t @ w_grp + b_grp).astype(jnp.float32), axis=-1)
    g_top, g_idx = lax.top_k(g_prob, 1)
    e_logits = jnp.einsum('td,gde->tge', t, w_rt) + b_rt
    e_sel = jnp.take_along_axis(e_logits, g_idx[:, :, None], axis=1)[:, 0]
    e_prob = jax.nn.softmax(e_sel.astype(jnp.float32), axis=-1)
    e_top, e_idx = lax.top_k(e_prob, TOP_K_INNER)
    e_top = e_top / jnp.sum(e_top, axis=-1, keepdims=True)
    w_tok = g_top * e_top
    combine_e = jnp.einsum('tk,tke->te', w_tok, jax.nn.one_hot(e_idx, EXPERTS_PER_GROUP, dtype=jnp.float32))
    combine = (jax.nn.one_hot(g_idx[:, 0], N_GROUPS, dtype=jnp.float32)[:, :, None]
               * combine_e[:, None, :]).astype(t.dtype)
    out = sum(
        jnp.einsum('tef,efd->td',
                   jax.nn.silu(jnp.einsum('td,edf->tef', t, w1[gi]))
                   * jnp.einsum('td,edf->tef', t, w3[gi]) * combine[:, gi, :, None],
                   w2[gi])
        for gi in range(N_GROUPS))
    return out.reshape(bn, seq_len, d)


def run_trunk(x, c, conv_bufs, gla_states, p):
    new_conv, new_v, new_gla = [], [], []
    for layer in range(DEPTH):
        mod = (jax.nn.silu(c) @ p['w_mod'][layer] + p['b_mod'][layer])[:, None, :]
        sh1, sc1, gt1, sh2, sc2, gt2 = jnp.split(mod, 6, axis=-1)
        h = x * (1 + sc1) + sh1
        i = layer // 2
        if layer % 2 == 0:
            m, buf, v_rows = conv_chunk_mixer(
                h, conv_bufs[i], p['ab_w_in'][i], p['ab_conv_w'][i], p['ab_v_ln_g'][i], p['ab_v_ln_b'][i],
                p['ab_w_s'][i], p['ab_b_s'][i], p['ab_w_out'][i])
            new_conv.append(buf)
            new_v.append(v_rows)
        else:
            m, s = gla_mixer(h, gla_states[i], p['gla_w_in'][i], p['gla_w_gk'][i], p['gla_b_gk'][i],
                             p['gla_norm_g'][i], p['gla_w_out'][i])
            new_gla.append(s)
        x = layer_norm(ALPHA * x + gt1 * m.astype(x.dtype), p['ln_g'][layer, 0], p['ln_b'][layer, 0])
        h = x * (1 + sc2) + sh2
        f = hier_moe(h, p['moe_w_grp'][layer], p['moe_b_grp'][layer], p['moe_w_rt'][layer], p['moe_b_rt'][layer],
                     p['moe_w1'][layer], p['moe_w3'][layer], p['moe_w2'][layer])
        x = layer_norm(ALPHA * x + gt2 * f.astype(x.dtype), p['ln_g'][layer, 1], p['ln_b'][layer, 1])
    return x, jnp.stack(new_conv), jnp.stack(new_v), jnp.stack(new_gla)


def setup_inputs(seed: int = 0) -> dict:
    key = jax.random.key(seed)
    ks = iter(jax.random.split(key, 40))

    def nrm(shape, scale):
        return jax.random.normal(next(ks), shape, jnp.float32) * scale

    d = D_MODEL
    return {
        'x_prompt': nrm((BATCH, SEQ, d), 1.0),
        'x_sample': nrm((DEC_BATCH, DEC_SEQ, d), 1.0),
        'cache_conv': nrm((N_AB_LAYERS, DEC_BATCH, CONV_WIDTH - 1, D_CONV), 1.0),
        'state_gla': nrm((N_C_LAYERS, DEC_BATCH, GLA_HEADS, GLA_DK, GLA_DV), 2.0),
        'c_prompt': nrm((BATCH, d), 1.0),
        'c_sample': nrm((DEC_BATCH, d), 1.0),
        'w_mod': nrm((DEPTH, d, 6 * d), 0.5 * d ** -0.5),
        'b_mod': nrm((DEPTH, 6 * d), 0.01),
        'ln_g': 1.0 + nrm((DEPTH, 2, d), 0.01),
        'ln_b': nrm((DEPTH, 2, d), 0.01),
        'ab_w_in': nrm((N_AB_LAYERS, d, D_AB_IN), d ** -0.5),
        'ab_conv_w': nrm((N_AB_LAYERS, CONV_WIDTH, D_CONV), CONV_WIDTH ** -0.5),
        'ab_v_ln_g': 1.0 + nrm((N_AB_LAYERS, D_CHUNK), 0.01),
        'ab_v_ln_b': nrm((N_AB_LAYERS, D_CHUNK), 0.01),
        'ab_w_s': nrm((N_AB_LAYERS, CHUNK_HEADS, CHUNK, CHUNK), CHUNK ** -0.5),
        'ab_b_s': 1.0 + nrm((N_AB_LAYERS, CHUNK_HEADS, CHUNK), 0.01),
        'ab_w_out': nrm((N_AB_LAYERS, D_AB_OUT, d), BETA * D_AB_OUT ** -0.5),
        'gla_w_in': nrm((N_C_LAYERS, d, D_GLA_IN), d ** -0.5),
        'gla_w_gk': nrm((N_C_LAYERS, GLA_GATE_RANK, GLA_DK_TOT), GLA_GATE_RANK ** -0.5),
        'gla_b_gk': nrm((N_C_LAYERS, GLA_DK_TOT), 0.01),
        'gla_norm_g': 1.0 + nrm((N_C_LAYERS, GLA_DV), 0.01),
        'gla_w_out': nrm((N_C_LAYERS, GLA_DV_TOT, d), BETA * GLA_DV_TOT ** -0.5),
        'moe_w_grp': nrm((DEPTH, d, N_GROUPS), d ** -0.5),
        'moe_b_grp': nrm((DEPTH, N_GROUPS), 0.01),
        'moe_w_rt': nrm((DEPTH, N_GROUPS, d, EXPERTS_PER_GROUP), d ** -0.5),
        'moe_b_rt': nrm((DEPTH, N_GROUPS, EXPERTS_PER_GROUP), 0.01),
        'moe_w1': nrm((DEPTH, N_GROUPS, EXPERTS_PER_GROUP, d, D_EXPERT), d ** -0.5),
        'moe_w3': nrm((DEPTH, N_GROUPS, EXPERTS_PER_GROUP, d, D_EXPERT), d ** -0.5),
        'moe_w2': nrm((DEPTH, N_GROUPS, EXPERTS_PER_GROUP, D_EXPERT, d), BETA * D_EXPERT ** -0.5),
    }


def reference(x_prompt, x_sample, cache_conv, state_gla, c_prompt, c_sample,
              w_mod, b_mod, ln_g, ln_b,
              ab_w_in, ab_conv_w, ab_v_ln_g, ab_v_ln_b, ab_w_s, ab_b_s, ab_w_out,
              gla_w_in, gla_w_gk, gla_b_gk, gla_norm_g, gla_w_out,
              moe_w_grp, moe_b_grp, moe_w_rt, moe_b_rt, moe_w1, moe_w3, moe_w2):
    p = dict(w_mod=w_mod, b_mod=b_mod, ln_g=ln_g, ln_b=ln_b,
             ab_w_in=ab_w_in, ab_conv_w=ab_conv_w, ab_v_ln_g=ab_v_ln_g, ab_v_ln_b=ab_v_ln_b,
             ab_w_s=ab_w_s, ab_b_s=ab_b_s, ab_w_out=ab_w_out,
             gla_w_in=gla_w_in, gla_w_gk=gla_w_gk, gla_b_gk=gla_b_gk, gla_norm_g=gla_norm_g,
             gla_w_out=gla_w_out,
             moe_w_grp=moe_w_grp, moe_b_grp=moe_b_grp, moe_w_rt=moe_w_rt, moe_b_rt=moe_b_rt,
             moe_w1=moe_w1, moe_w3=moe_w3, moe_w2=moe_w2)
    bp = x_prompt.shape[0]
    conv0 = jnp.zeros((N_AB_LAYERS, bp, CONV_WIDTH - 1, D_CONV), x_prompt.dtype)
    gla0 = jnp.zeros((N_C_LAYERS, bp, GLA_HEADS, GLA_DK, GLA_DV), jnp.float32)
    y_prompt, conv_prompt, _, gla_prompt = run_trunk(x_prompt, c_prompt, conv0, gla0, p)
    y_sample, conv_sample, chunk_v_sample, gla_sample = run_trunk(x_sample, c_sample, cache_conv, state_gla, p)
    return (y_prompt, y_sample, conv_prompt, conv_sample, chunk_v_sample, gla_prompt, gla_sample)
```

```python
import functools

import jax
import jax.numpy as jnp
from jax import lax
from jax.experimental import pallas as pl
from jax.experimental.pallas import tpu as pltpu

F32 = jnp.float32
BF16 = jnp.bfloat16
I32 = jnp.int32

LN_EPS = 1e-5
RMS_EPS = 1e-6
GLA_GATE_NORMALIZER = 16.0
TOP_K_INNER = 2

SUBLANES_V7X = 8
LANES_V7X = 128
VMEM_LIMIT_V7X = 56 * 1024 * 1024

NT_DIMS = (((1,), (1,)), ((), ()))
TN_DIMS = (((0,), (0,)), ((), ()))


def _params(*sem):
    return pltpu.CompilerParams(dimension_semantics=sem, vmem_limit_bytes=VMEM_LIMIT_V7X)


def _silu(x):
    return x * (1.0 / (1.0 + jnp.exp(-x)))


def _log_sigmoid(z):
    return jnp.minimum(z, 0.0) - jnp.log(1.0 + jnp.exp(-jnp.abs(z)))


def _layer_norm(x, g, b):
    mu = jnp.mean(x, axis=-1, keepdims=True)
    xc = x - mu
    var = jnp.mean(xc * xc, axis=-1, keepdims=True)
    return xc * lax.rsqrt(var + LN_EPS) * g + b


def _dot(a, b):
    return jnp.dot(a, b, preferred_element_type=F32)


def _split_bf16(x):
    hi = x.astype(BF16)
    lo = (x - hi.astype(F32)).astype(BF16)
    return hi, lo


def _mod_kernel(cp_ref, cs_ref, w_ref, b_ref, o_ref, *, n_prompt, groups_per_seq):
    w = w_ref[...].astype(BF16)
    b = b_ref[...]
    rp = _dot(_silu(cp_ref[...]).astype(BF16), w) + b
    rs = _dot(_silu(cs_ref[...]).astype(BF16), w) + b
    tn = o_ref.shape[-1]
    for s in range(n_prompt):
        o_ref[s * groups_per_seq:(s + 1) * groups_per_seq, :] = jnp.broadcast_to(
            rp[s:s + 1, :], (groups_per_seq, tn))
    o_ref[n_prompt * groups_per_seq:, :] = rs


def _mod_table(c_prompt, c_sample, w_mod, b_mod, seq):
    depth, d, n = w_mod.shape
    bp, bs = c_prompt.shape[0], c_sample.shape[0]
    gps = seq // SUBLANES_V7X
    g_total = bp * gps + bs
    cp = jnp.pad(c_prompt, ((0, (-bp) % SUBLANES_V7X), (0, 0)))
    tn = 1024
    return pl.pallas_call(
        functools.partial(_mod_kernel, n_prompt=bp, groups_per_seq=gps),
        grid=(depth, n // tn),
        in_specs=[
            pl.BlockSpec(cp.shape, lambda l, j: (0, 0)),
            pl.BlockSpec(c_sample.shape, lambda l, j: (0, 0)),
            pl.BlockSpec((None, d, tn), lambda l, j: (l, 0, j)),
            pl.BlockSpec((None, 1, tn), lambda l, j: (l, 0, j)),
        ],
        out_specs=pl.BlockSpec((None, g_total, tn), lambda l, j: (l, 0, j)),
        out_shape=jax.ShapeDtypeStruct((depth, g_total, n), F32),
        compiler_params=_params("arbitrary", "arbitrary"),
        name="mod",
    )(cp, c_sample, w_mod, b_mod.reshape(depth, 1, n))


def _mm_kernel(*refs, has_mod):
    if has_mod:
        x_ref, sh_ref, sc_ref, w_ref, o_ref, wb_ref = refs
    else:
        a_ref, w_ref, o_ref, wb_ref = refs

    @pl.when(pl.program_id(1) == 0)
    def _():
        wb_ref[...] = w_ref[...].astype(BF16)

    if has_mod:
        x = x_ref[...]
        g, s, k = x.shape
        h = x * (1.0 + sc_ref[...][:, None, :]) + sh_ref[...][:, None, :]
        a = h.reshape(g * s, k).astype(BF16)
    else:
        a = a_ref[...]
    o_ref[...] = _dot(a, wb_ref[...]).astype(o_ref.dtype)


def _mm(a, w3, w_idx, n_out, *, mod=None, tm=512, tn=1024, out_dtype=BF16):
    k = w3.shape[1]
    tn = min(tn, n_out)
    if mod is None:
        t = a.shape[0]
        a_specs = [pl.BlockSpec((tm, k), lambda j, i: (i, 0))]
        ins = [a]
    else:
        table, layer, sh_col, sc_col = mod
        t = a.shape[0] * SUBLANES_V7X
        gt = tm // SUBLANES_V7X
        a_specs = [
            pl.BlockSpec((gt, SUBLANES_V7X, k), lambda j, i: (i, 0, 0)),
            pl.BlockSpec((None, gt, k), lambda j, i: (layer, i, sh_col)),
            pl.BlockSpec((None, gt, k), lambda j, i: (layer, i, sc_col)),
        ]
        ins = [a, table, table]
    return pl.pallas_call(
        functools.partial(_mm_kernel, has_mod=mod is not None),
        grid=(n_out // tn, t // tm),
        in_specs=a_specs + [pl.BlockSpec((None, k, tn), lambda j, i: (w_idx, 0, j))],
        out_specs=pl.BlockSpec((tm, tn), lambda j, i: (i, j)),
        out_shape=jax.ShapeDtypeStruct((t, n_out), out_dtype),
        scratch_shapes=[pltpu.VMEM((k, tn), BF16)],
        compiler_params=_params("arbitrary", "arbitrary"),
        name="mm",
    )(*ins, w3)


def _mix0_kernel(bg_ref, cg_ref, hx_ref, u_ref, v_ref, cache_ref, cw_ref, vg_ref, vb_ref, wm_ref, bias_ref,
                 y_ref, convp_ref, convs_ref, vns_ref, zprev_ref, *, n_prompt_tiles, tiles_per_seq, n_heads):
    i = pl.program_id(0)
    tm, dc = bg_ref.shape
    ns = tm // SUBLANES_V7X
    z = cg_ref[...].astype(F32) * hx_ref[...].astype(F32)
    row = lax.broadcasted_iota(I32, (tm, dc), 0)
    r1 = pltpu.roll(z, 1, 0)
    r2 = pltpu.roll(z, 2, 0)
    cw = cw_ref[...]
    bg = bg_ref[...].astype(F32)

    vn = _layer_norm(v_ref[...].astype(F32), vg_ref[...], vb_ref[...])
    vnb = vn.astype(BF16)
    hd = dc // n_heads
    mixed = jnp.concatenate(
        [_dot(wm_ref[h], vnb[:, h * hd:(h + 1) * hd]) for h in range(n_heads)], axis=-1) + bias_ref[...]
    y_ref[:, dc:] = (u_ref[...].astype(F32) * mixed).astype(BF16)

    def conv_out(zm1, zm2):
        conv = cw[0:1, :] * zm2 + cw[1:2, :] * zm1 + cw[2:3, :] * z
        y_ref[:, :dc] = (bg * conv).astype(BF16)

    @pl.when(i < n_prompt_tiles)
    def _prompt():
        @pl.when(i % tiles_per_seq == 0)
        def _():
            zprev_ref[...] = jnp.zeros_like(zprev_ref)

        zp = zprev_ref[...]
        p1 = zp[SUBLANES_V7X - 1:SUBLANES_V7X, :]
        p2 = zp[SUBLANES_V7X - 2:SUBLANES_V7X - 1, :]
        conv_out(jnp.where(row == 0, p1, r1),
                 jnp.where(row == 0, p2, jnp.where(row == 1, p1, r2)))
        zprev_ref[...] = z[tm - SUBLANES_V7X:, :]
        convp_ref[...] = z[tm - 2:, :].reshape(1, 2, dc)

    @pl.when(i >= n_prompt_tiles)
    def _sample():
        c = cache_ref[...]
        c0 = jnp.broadcast_to(c[:, 0:1, :], (ns, SUBLANES_V7X, dc)).reshape(tm, dc)
        c1 = jnp.broadcast_to(c[:, 1:2, :], (ns, SUBLANES_V7X, dc)).reshape(tm, dc)
        rr = row % SUBLANES_V7X
        conv_out(jnp.where(rr == 0, c1, r1),
                 jnp.where(rr == 0, c0, jnp.where(rr == 1, c1, r2)))
        z3 = z.reshape(ns, SUBLANES_V7X, dc)
        convs_ref[...] = z3[:, SUBLANES_V7X - 2:, :]
        vns_ref[...] = vn.reshape(ns, SUBLANES_V7X, dc)


def _mix0(p, cache, conv_w, v_g, v_b, wm, bias, *, t_prompt, seq, n_heads):
    t, n = p.shape
    bs, cwm1, dc = cache.shape
    tm = wm.shape[-1]
    assert cwm1 == 2 and conv_w.shape[0] == 3 and n == 5 * dc and seq % tm == 0
    n_p = t_prompt // tm
    n_s = (t - t_prompt) // tm
    tps = seq // tm
    bp = t_prompt // seq
    ns = tm // SUBLANES_V7X

    def col(c):
        return pl.BlockSpec((tm, dc), lambda i: (i, c))

    def s_idx(i):
        return jnp.maximum(i - n_p, 0)

    const2 = lambda i: (0, 0)
    mode = lambda i: ((i >= n_p).astype(I32), 0, 0, 0)
    return pl.pallas_call(
        functools.partial(_mix0_kernel, n_prompt_tiles=n_p, tiles_per_seq=tps, n_heads=n_heads),
        grid=(n_p + n_s,),
        in_specs=[col(0), col(1), col(2), col(3), col(4),
                  pl.BlockSpec((ns, 2, dc), lambda i: (s_idx(i), 0, 0)),
                  pl.BlockSpec((3, dc), const2),
                  pl.BlockSpec((1, dc), const2),
                  pl.BlockSpec((1, dc), const2),
                  pl.BlockSpec((None, n_heads, tm, tm), mode),
                  pl.BlockSpec((None, tm, dc), lambda i: ((i >= n_p).astype(I32), 0, 0))],
        out_specs=[pl.BlockSpec((tm, 2 * dc), lambda i: (i, 0)),
                   pl.BlockSpec((1, 2, dc), lambda i: (jnp.minimum(i // tps, bp - 1), 0, 0)),
                   pl.BlockSpec((ns, 2, dc), lambda i: (s_idx(i), 0, 0)),
                   pl.BlockSpec((ns, SUBLANES_V7X, dc), lambda i: (s_idx(i), 0, 0))],
        out_shape=[jax.ShapeDtypeStruct((t, 2 * dc), BF16),
                   jax.ShapeDtypeStruct((bp, 2, dc), F32),
                   jax.ShapeDtypeStruct((bs, 2, dc), F32),
                   jax.ShapeDtypeStruct((bs, SUBLANES_V7X, dc), F32)],
        scratch_shapes=[pltpu.VMEM((SUBLANES_V7X, dc), F32)],
        compiler_params=_params("arbitrary"),
        name="mix0",
    )(p, p, p, p, p, cache, conv_w, v_g.reshape(1, dc), v_b.reshape(1, dc), wm, bias)


def _outln_kernel(y_ref, w_ref, x_ref, gt_ref, sh_ref, sc_ref, lng_ref, lnb_ref, x1_ref, h_ref, *, alpha):
    g, s, d = x_ref.shape
    m = _dot(y_ref[...], w_ref[...]).reshape(g, s, d)
    x1 = _layer_norm(alpha * x_ref[...] + gt_ref[...][:, None, :] * m, lng_ref[...], lnb_ref[...])
    x1_ref[...] = x1
    h_ref[...] = (x1 * (1.0 + sc_ref[...][:, None, :]) + sh_ref[...][:, None, :]).reshape(g * s, d)


def _outln(y, w_bf, x, table, layer, ln_g, ln_b, *, alpha, tm=256):
    t, k = y.shape
    d = w_bf.shape[1]
    gt = tm // SUBLANES_V7X

    def mod(c):
        return pl.BlockSpec((None, gt, d), lambda i: (layer, i, c))

    return pl.pallas_call(
        functools.partial(_outln_kernel, alpha=alpha),
        grid=(t // tm,),
        in_specs=[pl.BlockSpec((tm, k), lambda i: (i, 0)),
                  pl.BlockSpec((k, d), lambda i: (0, 0), pipeline_mode=pl.Buffered(1)),
                  pl.BlockSpec((gt, SUBLANES_V7X, d), lambda i: (i, 0, 0)),
                  mod(2), mod(3), mod(4),
                  pl.BlockSpec((1, d), lambda i: (0, 0)),
                  pl.BlockSpec((1, d), lambda i: (0, 0))],
        out_specs=[pl.BlockSpec((gt, SUBLANES_V7X, d), lambda i: (i, 0, 0)),
                   pl.BlockSpec((tm, d), lambda i: (i, 0))],
        out_shape=[jax.ShapeDtypeStruct(x.shape, F32), jax.ShapeDtypeStruct((t, d), F32)],
        compiler_params=_params("arbitrary"),
        name="outln",
    )(y, w_bf, x, table, table, table, ln_g.reshape(1, d), ln_b.reshape(1, d))


def _first_index_of(vals, target):
    idx = jnp.full(target.shape, len(vals) - 1, I32)
    for j in reversed(range(len(vals))):
        idx = jnp.where(vals[j] == target, j, idx)
    return idx


def _softmax_rows(rows):
    m = functools.reduce(jnp.maximum, rows)
    e = [jnp.exp(r - m) for r in rows]
    s = functools.reduce(lambda a, b: a + b, e)
    return [x / s for x in e]


def _route_kernel(h_ref, wr_ref, br_ref, eid_ref, wt_ref, rank_ref, cnt_ref, carry_ref, *, n_groups, n_exp):
    i = pl.program_id(0)

    @pl.when(i == 0)
    def _():
        carry_ref[...] = jnp.zeros_like(carry_ref)

    tm = h_ref.shape[0]
    hh, hl = _split_bf16(h_ref[...])
    wh, wl = _split_bf16(wr_ref[...])
    dg = lambda a, b: lax.dot_general(a, b, NT_DIMS, preferred_element_type=F32)
    logits = dg(wh, hh) + dg(wh, hl) + dg(wl, hh) + br_ref[...]

    g_prob = _softmax_rows([logits[g:g + 1, :] for g in range(n_groups)])
    g_top = functools.reduce(jnp.maximum, g_prob)
    g_idx = _first_index_of(g_prob, g_top)

    e_sel = []
    for e in range(n_exp):
        sel = logits[n_groups + e:n_groups + e + 1, :]
        for g in range(1, n_groups):
            r = n_groups + g * n_exp + e
            sel = jnp.where(g_idx == g, logits[r:r + 1, :], sel)
        e_sel.append(sel)
    e_prob = _softmax_rows(e_sel)
    p1 = functools.reduce(jnp.maximum, e_prob)
    i1 = _first_index_of(e_prob, p1)
    rest = [jnp.where(i1 == e, -1.0, e_prob[e]) for e in range(n_exp)]
    p2 = functools.reduce(jnp.maximum, rest)
    i2 = _first_index_of(rest, p2)
    den = p1 + p2
    wt_ref[0:1, :] = g_top * (p1 / den)
    wt_ref[1:2, :] = g_top * (p2 / den)
    eid0 = g_idx * n_exp + i1
    eid1 = g_idx * n_exp + i2
    eid_ref[0:1, :] = eid0
    eid_ref[1:2, :] = eid1

    n_e = n_groups * n_exp
    eio = lax.broadcasted_iota(I32, (n_e, tm), 0)
    oh0 = (eio == eid0).astype(F32)
    oh1 = (eio == eid1).astype(F32)
    oh = oh0 + oh1
    before = (lax.broadcasted_iota(I32, (tm, tm), 0) < lax.broadcasted_iota(I32, (tm, tm), 1)).astype(BF16)
    base = _dot(oh.astype(BF16), before) + carry_ref[...]
    rank_ref[0:1, :] = jnp.sum(oh0 * base, axis=0, keepdims=True).astype(I32)
    rank_ref[1:2, :] = jnp.sum(oh1 * base, axis=0, keepdims=True).astype(I32)
    total = carry_ref[...] + jnp.sum(oh, axis=1, keepdims=True)
    carry_ref[...] = total
    cnt_ref[...] = jnp.broadcast_to(total, cnt_ref.shape).astype(I32)


def _route(h, wr, br, *, n_groups, n_exp, tm=512):
    t, d = h.shape
    n_e = n_groups * n_exp
    pair = lambda dt: jax.ShapeDtypeStruct((TOP_K_INNER, t), dt)
    pair_spec = pl.BlockSpec((TOP_K_INNER, tm), lambda i: (0, i))
    return pl.pallas_call(
        functools.partial(_route_kernel, n_groups=n_groups, n_exp=n_exp),
        grid=(t // tm,),
        in_specs=[pl.BlockSpec((tm, d), lambda i: (i, 0)),
                  pl.BlockSpec(wr.shape, lambda i: (0, 0)),
                  pl.BlockSpec(br.shape, lambda i: (0, 0))],
        out_specs=[pair_spec, pair_spec, pair_spec,
                   pl.BlockSpec((n_e, LANES_V7X), lambda i: (0, 0))],
        out_shape=[pair(I32), pair(F32), pair(I32), jax.ShapeDtypeStruct((n_e, LANES_V7X), I32)],
        scratch_shapes=[pltpu.VMEM((n_e, 1), F32)],
        compiler_params=_params("arbitrary"),
        name="route",
    )(h, wr, br)


def _row_gather_start(src_hbm, dst, sem, idx_ref, base, n):
    def body(r, c):
        tok = idx_ref[base + r]
        pltpu.make_async_copy(src_hbm.at[pl.ds(tok, 1)], dst.at[pl.ds(r, 1)], sem).start()
        return c
    lax.fori_loop(0, n, body, 0, unroll=8)


def _row_gather_wait(src_hbm, dst, sem, n):
    pltpu.make_async_copy(src_hbm.at[pl.ds(0, n)], dst, sem).wait()


def _moe_kernel(te_ref, src_ref, nu_ref, h_hbm, w1_ref, w3_ref, w2_ref, o_ref,
                xbuf, w1b, w3b, w2b, sem, *, tg):
    i = pl.program_id(0)
    n_used = nu_ref[0]

    @pl.when(i == 0)
    def _():
        _row_gather_start(h_hbm, xbuf.at[0], sem.at[0], src_ref, 0, tg)

    @pl.when(i < n_used)
    def _():
        slot = i % 2

        @pl.when(i + 1 < n_used)
        def _():
            _row_gather_start(h_hbm, xbuf.at[1 - slot], sem.at[1 - slot], src_ref, (i + 1) * tg, tg)

        @pl.when((i == 0) | (te_ref[i] != te_ref[jnp.maximum(i - 1, 0)]))
        def _():
            w1b[...] = w1_ref[...].astype(BF16)
            w3b[...] = w3_ref[...].astype(BF16)
            w2b[...] = w2_ref[...].astype(BF16)

        _row_gather_wait(h_hbm, xbuf.at[slot], sem.at[slot], tg)
        x = xbuf[slot].astype(BF16)
        a = _dot(x, w1b[...])
        b = _dot(x, w3b[...])
        o_ref[...] = _dot((_silu(a) * b).astype(BF16), w2b[...])

    @pl.when(i >= n_used)
    def _():
        o_ref[...] = jnp.zeros_like(o_ref)


def _moe_call(h, w1, w3, w2, w_base, te, src, n_used, *, tg):
    t, d = h.shape
    f = w1.shape[-1]
    n_tiles = src.shape[0] // tg
    w_in = pl.BlockSpec((None, d, f), lambda i, te, src, nu: (w_base + te[i], 0, 0))
    w_out = pl.BlockSpec((None, f, d), lambda i, te, src, nu: (w_base + te[i], 0, 0))
    grid_spec = pltpu.PrefetchScalarGridSpec(
        num_scalar_prefetch=3,
        grid=(n_tiles,),
        in_specs=[pl.BlockSpec(memory_space=pl.ANY), w_in, w_in, w_out],
        out_specs=pl.BlockSpec((tg, d), lambda i, te, src, nu: (i, 0)),
        scratch_shapes=[pltpu.VMEM((2, tg, d), F32),
                        pltpu.VMEM((d, f), BF16), pltpu.VMEM((d, f), BF16), pltpu.VMEM((f, d), BF16),
                        pltpu.SemaphoreType.DMA((2,))],
    )
    return pl.pallas_call(
        functools.partial(_moe_kernel, tg=tg),
        grid_spec=grid_spec,
        out_shape=jax.ShapeDtypeStruct((n_tiles * tg, d), F32),
        compiler_params=_params("arbitrary"),
        name="moe",
    )(te, src, n_used, h, w1, w3, w2)


def _comb_kernel(pos_ref, y_hbm, wt_ref, x_ref, gt_ref, lng_ref, lnb_ref, *rest, alpha, tm, t_total, has_next):
    if has_next:
        shn_ref, scn_ref, x2_ref, hn_ref, ybuf, sem = rest
    else:
        x2_ref, ybuf, sem = rest
    i = pl.program_id(0)
    n = pl.num_programs(0)

    def start(tile, slot):
        for k in range(TOP_K_INNER):
            _row_gather_start(y_hbm, ybuf.at[slot, k], sem.at[slot], pos_ref, k * t_total + tile * tm, tm)

    @pl.when(i == 0)
    def _():
        start(0, 0)

    slot = i % 2

    @pl.when(i + 1 < n)
    def _():
        start(i + 1, 1 - slot)

    for k in range(TOP_K_INNER):
        _row_gather_wait(y_hbm, ybuf.at[slot, k], sem.at[slot], tm)
    w = wt_ref[...]
    f = w[:, 0:1] * ybuf[slot, 0] + w[:, 1:2] * ybuf[slot, 1]
    g, s, d = x_ref.shape
    x2 = _layer_norm(alpha * x_ref[...] + gt_ref[...][:, None, :] * f.reshape(g, s, d), lng_ref[...], lnb_ref[...])
    x2_ref[...] = x2
    if has_next:
        hn = x2 * (1.0 + scn_ref[...][:, None, :]) + shn_ref[...][:, None, :]
        hn_ref[...] = hn.reshape(g * s, d).astype(BF16)


def _comb(pos_flat, y_sorted, wt_t, x1, table, layer, ln_g, ln_b, *, alpha, has_next, tm=256):
    g_total, s, d = x1.shape
    t = g_total * s
    gt = tm // SUBLANES_V7X

    def mod(l, c):
        return pl.BlockSpec((None, gt, d), lambda i, pos: (l, i, c))

    xspec = pl.BlockSpec((gt, s, d), lambda i, pos: (i, 0, 0))
    vec = pl.BlockSpec((1, d), lambda i, pos: (0, 0))
    in_specs = [pl.BlockSpec(memory_space=pl.ANY),
                pl.BlockSpec((tm, TOP_K_INNER), lambda i, pos: (i, 0)),
                xspec, mod(layer, 5), vec, vec]
    ins = [y_sorted, wt_t, x1, table, ln_g.reshape(1, d), ln_b.reshape(1, d)]
    out_specs = [xspec]
    out_shape = [jax.ShapeDtypeStruct(x1.shape, F32)]
    if has_next:
        in_specs += [mod(layer + 1, 0), mod(layer + 1, 1)]
        ins += [table, table]
        out_specs.append(pl.BlockSpec((tm, d), lambda i, pos: (i, 0)))
        out_shape.append(jax.ShapeDtypeStruct((t, d), BF16))
    grid_spec = pltpu.PrefetchScalarGridSpec(
        num_scalar_prefetch=1,
        grid=(t // tm,),
        in_specs=in_specs,
        out_specs=out_specs,
        scratch_shapes=[pltpu.VMEM((2, TOP_K_INNER, tm, d), F32), pltpu.SemaphoreType.DMA((2,))],
    )
    return pl.pallas_call(
        functools.partial(_comb_kernel, alpha=alpha, tm=tm, t_total=t, has_next=has_next),
        grid_spec=grid_spec,
        out_shape=out_shape,
        compiler_params=_params("arbitrary"),
        name="comb",
    )(pos_flat, *ins)


def _gla_decay(gl, wgk, bg_row, wgk_t, bg_col):
    la = _log_sigmoid(_dot(gl, wgk) + bg_row) * (1.0 / GLA_GATE_NORMALIZER)
    la_t = _log_sigmoid(lax.dot_general(wgk_t, gl, NT_DIMS, preferred_element_type=F32) + bg_col) * (
        1.0 / GLA_GATE_NORMALIZER)
    return la, la_t


def _rms_gate(o, ng, gate):
    on = o * lax.rsqrt(jnp.mean(o * o, axis=-1, keepdims=True) + RMS_EPS) * ng
    return (on * _silu(gate)).astype(BF16)


def _gla_prompt_kernel(q_ref, k_ref, v_ref, g_ref, gl_ref, wgk_ref, wgkt_ref, bgr_ref, bgc_ref, ng_ref,
                       y_ref, sout_ref, st_ref, *, n_heads, scale):
    j = pl.program_id(1)

    @pl.when(j == 0)
    def _():
        st_ref[...] = jnp.zeros_like(st_ref)

    bk = q_ref.shape[0]
    dk = q_ref.shape[1] // n_heads
    dv = v_ref.shape[1] // n_heads
    gl = gl_ref[...]
    causal = lax.broadcasted_iota(I32, (bk, bk), 0) >= lax.broadcasted_iota(I32, (bk, bk), 1)
    tri = causal.astype(BF16)
    for h in range(n_heads):
        ks, vs = slice(h * dk, (h + 1) * dk), slice(h * dv, (h + 1) * dv)
        q = q_ref[:, ks].astype(F32) * scale
        k = k_ref[:, ks].astype(F32)
        v = v_ref[:, vs]
        la, la_t = _gla_decay(gl, wgk_ref[:, ks], bgr_ref[:, ks], wgkt_ref[ks, :], bgc_ref[ks, :])
        la_hi, la_lo = _split_bf16(la)
        cum = _dot(tri, la_hi) + _dot(tri, la_lo)
        last = cum[bk - 1:bk, :]
        last_t = jnp.sum(la_t, axis=1, keepdims=True)
        qs = (q * jnp.exp(cum)).astype(BF16)
        kn = (k * jnp.exp(-cum)).astype(BF16)
        sc = lax.dot_general(qs, kn, NT_DIMS, preferred_element_type=F32)
        sc = jnp.where(causal, sc, 0.0).astype(BF16)
        s_old = st_ref[h]
        o = _dot(sc, v) + _dot(qs, s_old.astype(BF16))
        kd = (k * jnp.exp(last - cum)).astype(BF16)
        st_ref[h] = s_old * jnp.exp(last_t) + lax.dot_general(kd, v, TN_DIMS, preferred_element_type=F32)
        y_ref[:, vs] = _rms_gate(o, ng_ref[...], g_ref[:, vs].astype(F32))

    @pl.when(j == pl.num_programs(1) - 1)
    def _():
        sout_ref[...] = st_ref[...]


def _gla_sample_kernel(q_ref, k_ref, v_ref, g_ref, gl_ref, wgk_ref, wgkt_ref, bgr_ref, bgc_ref, ng_ref, sin_ref,
                       y_ref, sout_ref, *, n_heads, scale, seq):
    rows = q_ref.shape[0]
    nb = rows // seq
    dk = q_ref.shape[1] // n_heads
    dv = v_ref.shape[1] // n_heads
    gl = gl_ref[...]
    ri = lax.broadcasted_iota(I32, (rows, rows), 0)
    ci = lax.broadcasted_iota(I32, (rows, rows), 1)
    same = (ri // seq) == (ci // seq)
    causal = same & (ri >= ci)
    tri = causal.astype(BF16)
    ones_bd = same.astype(BF16)
    lane_seq = lax.broadcasted_iota(I32, (dk, rows), 1) // seq
    for h in range(n_heads):
        ks, vs = slice(h * dk, (h + 1) * dk), slice(h * dv, (h + 1) * dv)
        q = q_ref[:, ks].astype(F32) * scale
        k = k_ref[:, ks].astype(F32)
        v = v_ref[:, vs]
        la, la_t = _gla_decay(gl, wgk_ref[:, ks], bgr_ref[:, ks], wgkt_ref[ks, :], bgc_ref[ks, :])
        la_hi, la_lo = _split_bf16(la)
        cum = _dot(tri, la_hi) + _dot(tri, la_lo)
        last = _dot(ones_bd, la_hi) + _dot(ones_bd, la_lo)
        q_dec = q * jnp.exp(cum)
        qs = q_dec.astype(BF16)
        kn = (k * jnp.exp(-cum)).astype(BF16)
        sc = lax.dot_general(qs, kn, NT_DIMS, preferred_element_type=F32)
        sc = jnp.where(causal, sc, 0.0).astype(BF16)
        kd = k * jnp.exp(last - cum)
        v32 = v.astype(F32)
        o_state = []
        for s in range(nb):
            rs = slice(s * seq, (s + 1) * seq)
            s_old = sin_ref[s, h]
            o_state.append(_dot(q_dec[rs, :].astype(BF16), s_old.astype(BF16)))
            last_t = jnp.sum(jnp.where(lane_seq == s, la_t, 0.0), axis=1, keepdims=True)
            sout_ref[s, h] = s_old * jnp.exp(last_t) + lax.dot_general(
                kd[rs, :].astype(BF16), v32[rs, :].astype(BF16), TN_DIMS, preferred_element_type=F32)
        o = _dot(sc, v) + jnp.concatenate(o_state, axis=0)
        y_ref[:, vs] = _rms_gate(o, ng_ref[...], g_ref[:, vs].astype(F32))


def _gla(p, gl, wgk, wgk_t, bg_row, bg_col, ng, state_s, *, t_prompt, seq, n_heads, dk, dv, bk=128, nb=4):
    t = p.shape[0]
    bs, _, _, _ = state_s.shape
    s_len = (t - t_prompt) // bs
    bp = t_prompt // seq
    dkt, dvt = n_heads * dk, n_heads * dv
    assert dvt == 2 * dkt
    scale = dk ** -0.5
    r = gl.shape[1]
    const = lambda *_: (0, 0)
    w_specs = [pl.BlockSpec((r, dkt), const), pl.BlockSpec((dkt, r), const),
               pl.BlockSpec((1, dkt), const), pl.BlockSpec((dkt, 1), const), pl.BlockSpec((1, dv), const)]
    w_ins = [wgk, wgk_t, bg_row, bg_col, ng]

    nblk = seq // bk
    rowp = lambda b, j: b * nblk + j
    y_p, s_p = pl.pallas_call(
        functools.partial(_gla_prompt_kernel, n_heads=n_heads, scale=scale),
        grid=(bp, nblk),
        in_specs=[pl.BlockSpec((bk, dkt), lambda b, j: (rowp(b, j), 0)),
                  pl.BlockSpec((bk, dkt), lambda b, j: (rowp(b, j), 1)),
                  pl.BlockSpec((bk, dvt), lambda b, j: (rowp(b, j), 1)),
                  pl.BlockSpec((bk, dvt), lambda b, j: (rowp(b, j), 2)),
                  pl.BlockSpec((bk, r), lambda b, j: (rowp(b, j), 0))] + w_specs,
        out_specs=[pl.BlockSpec((bk, dvt), lambda b, j: (rowp(b, j), 0)),
                   pl.BlockSpec((None, n_heads, dk, dv), lambda b, j: (b, 0, 0, 0))],
        out_shape=[jax.ShapeDtypeStruct((t_prompt, dvt), BF16),
                   jax.ShapeDtypeStruct((bp, n_heads, dk, dv), F32)],
        scratch_shapes=[pltpu.VMEM((n_heads, dk, dv), F32)],
        compiler_params=_params("arbitrary", "arbitrary"),
        name="gla_prompt",
    )(p, p, p, p, gl, *w_ins)

    rows = nb * s_len
    off = t_prompt // rows
    y_s, s_s = pl.pallas_call(
        functools.partial(_gla_sample_kernel, n_heads=n_heads, scale=scale, seq=s_len),
        grid=(bs // nb,),
        in_specs=[pl.BlockSpec((rows, dkt), lambda i: (off + i, 0)),
                  pl.BlockSpec((rows, dkt), lambda i: (off + i, 1)),
                  pl.BlockSpec((rows, dvt), lambda i: (off + i, 1)),
                  pl.BlockSpec((rows, dvt), lambda i: (off + i, 2)),
                  pl.BlockSpec((rows, r), lambda i: (off + i, 0))] + w_specs + [
                  pl.BlockSpec((nb, n_heads, dk, dv), lambda i: (i, 0, 0, 0))],
        out_specs=[pl.BlockSpec((rows, dvt), lambda i: (i, 0)),
                   pl.BlockSpec((nb, n_heads, dk, dv), lambda i: (i, 0, 0, 0))],
        out_shape=[jax.ShapeDtypeStruct((t - t_prompt, dvt), BF16),
                   jax.ShapeDtypeStruct(state_s.shape, F32)],
        compiler_params=_params("arbitrary"),
        name="gla_sample",
    )(p, p, p, p, gl, *w_ins, state_s)
    return jnp.concatenate([y_p, y_s], axis=0), s_p, s_s


def _moe_schedule(eid, rank, counts, *, tg, n_tiles):
    t = eid.shape[1]
    padded = ((counts + tg - 1) // tg) * tg
    ends = jnp.cumsum(padded)
    starts = ends - padded
    pos = starts[eid] + rank
    n_used = (ends[-1] // tg).astype(I32)
    tile_start = jnp.arange(n_tiles, dtype=I32) * tg
    te = jnp.searchsorted(ends, tile_start, side="right").astype(I32)
    te = jnp.where(jnp.arange(n_tiles) < n_used, te, te[n_used - 1])
    tok = jnp.tile(jnp.arange(t, dtype=I32), TOP_K_INNER)
    src = jnp.zeros((n_tiles * tg,), I32).at[pos.reshape(-1)].set(tok)
    return pos.reshape(-1).astype(I32), te, src, n_used.reshape(1)


def kernel(x_prompt, x_sample, cache_conv, state_gla, c_prompt, c_sample, w_mod, b_mod, ln_g, ln_b, ab_w_in, ab_conv_w, ab_v_ln_g, ab_v_ln_b, ab_w_s, ab_b_s, ab_w_out, gla_w_in, gla_w_gk, gla_b_gk, gla_norm_g, gla_w_out, moe_w_grp, moe_b_grp, moe_w_rt, moe_b_rt, moe_w1, moe_w3, moe_w2):
    bp, seq, d = x_prompt.shape
    bs, s_len, _ = x_sample.shape
    assert s_len == SUBLANES_V7X and seq % SUBLANES_V7X == 0
    depth = w_mod.shape[0]
    alpha = float((2 * depth) ** 0.25)
    t_p, t_s = bp * seq, bs * s_len
    t = t_p + t_s
    n_groups, n_exp = moe_w_rt.shape[1], moe_w_rt.shape[3]
    n_e = n_groups * n_exp
    d_ff = moe_w1.shape[-1]
    tg = 256
    n_tiles = (TOP_K_INNER * t) // tg + n_e

    x = jnp.concatenate([x_prompt.reshape(t_p, d), x_sample.reshape(t_s, d)], axis=0)
    x = x.reshape(t // SUBLANES_V7X, SUBLANES_V7X, d)
    table = _mod_table(c_prompt, c_sample, w_mod, b_mod, seq)

    w1 = moe_w1.reshape(depth * n_e, d, d_ff)
    w3 = moe_w3.reshape(depth * n_e, d, d_ff)
    w2 = moe_w2.reshape(depth * n_e, d_ff, d)

    conv_p, conv_s, chunk_v, gla_p, gla_s = [], [], [], [], []
    h_bf = None
    for layer in range(depth):
        li = layer // 2
        if layer % 2 == 0:
            n_heads, chunk = ab_w_s.shape[1], ab_w_s.shape[2]
            dc = ab_conv_w.shape[-1]
            if h_bf is None:
                p = _mm(x, ab_w_in, li, ab_w_in.shape[-1], mod=(table, layer, 0, 1))
            else:
                p = _mm(h_bf, ab_w_in, li, ab_w_in.shape[-1])
            w_s = ab_w_s[li]
            wm_p = jnp.tril(w_s)
            reps = chunk // s_len
            blk = jnp.tril(w_s[:, :s_len, :s_len])
            wm_s = jnp.einsum("ab,hts->hatbs", jnp.eye(reps, dtype=F32), blk).reshape(n_heads, chunk, chunk)
            wm = jnp.stack([wm_p, wm_s]).astype(BF16)
            b_s = ab_b_s[li]
            hd = dc // n_heads
            bias_p = jnp.repeat(b_s.T, hd, axis=1)
            bias_s = jnp.repeat(jnp.tile(b_s[:, :s_len].T, (reps, 1)), hd, axis=1)
            bias = jnp.stack([bias_p, bias_s])
            y, cp_new, cs_new, vn_s = _mix0(p, cache_conv[li], ab_conv_w[li], ab_v_ln_g[li], ab_v_ln_b[li],
                                            wm, bias, t_prompt=t_p, seq=seq, n_heads=n_heads)
            conv_p.append(cp_new)
            conv_s.append(cs_new)
            chunk_v.append(vn_s)
            w_out = ab_w_out[li].astype(BF16)
        else:
            n_heads, dk, dv = state_gla.shape[2], state_gla.shape[3], state_gla.shape[4]
            dkt, dvt = n_heads * dk, n_heads * dv
            rank = gla_w_gk.shape[1]
            n_main = 2 * dkt + 2 * dvt
            p = _mm(h_bf, gla_w_in, li, n_main)
            w_lo = jnp.pad(gla_w_in[li][:, n_main:], ((0, 0), (0, LANES_V7X - rank)))[None]
            gl = _mm(h_bf, w_lo, 0, LANES_V7X)
            wgk = jnp.pad(gla_w_gk[li], ((0, LANES_V7X - rank), (0, 0))).astype(BF16)
            y, sp_new, ss_new = _gla(p, gl, wgk, wgk.T, gla_b_gk[li].reshape(1, dkt), gla_b_gk[li].reshape(dkt, 1),
                                     gla_norm_g[li].reshape(1, dv), state_gla[li],
                                     t_prompt=t_p, seq=seq, n_heads=n_heads, dk=dk, dv=dv)
            gla_p.append(sp_new)
            gla_s.append(ss_new)
            w_out = gla_w_out[li].astype(BF16)

        x1, h2 = _outln(y, w_out, x, table, layer, ln_g[layer, 0], ln_b[layer, 0], alpha=alpha)

        wr = jnp.concatenate([moe_w_grp[layer].T,
                              jnp.transpose(moe_w_rt[layer], (0, 2, 1)).reshape(n_e, d)], axis=0)
        wr = jnp.pad(wr, ((0, LANES_V7X - wr.shape[0]), (0, 0)))
        br = jnp.concatenate([moe_b_grp[layer], moe_b_rt[layer].reshape(n_e)])
        br = jnp.pad(br, (0, LANES_V7X - br.shape[0])).reshape(LANES_V7X, 1)
        eid, wt, rank_, cnt = _route(h2, wr, br, n_groups=n_groups, n_exp=n_exp)
        pos, te, src, n_used = _moe_schedule(eid, rank_, cnt[:, 0], tg=tg, n_tiles=n_tiles)
        y_sorted = _moe_call(h2, w1, w3, w2, layer * n_e, te, src, n_used, tg=tg)
        has_next = layer + 1 < depth
        outs = _comb(pos, y_sorted, wt.T, x1, table, layer, ln_g[layer, 1], ln_b[layer, 1],
                     alpha=alpha, has_next=has_next)
        x = outs[0]
        h_bf = outs[1] if has_next else None

    xf = x.reshape(t, d)
    y_prompt = xf[:t_p].reshape(bp, seq, d)
    y_sample = xf[t_p:].reshape(bs, s_len, d)
    return (y_prompt, y_sample, jnp.stack(conv_p), jnp.stack(conv_s), jnp.stack(chunk_v),
            jnp.stack(gla_p), jnp.stack(gla_s))
```

```python
import functools

import jax
import jax.numpy as jnp
from jax import lax
from jax.experimental import pallas as pl
from jax.experimental.pallas import tpu as pltpu

F32 = jnp.float32
BF16 = jnp.bfloat16
I32 = jnp.int32

LN_EPS = 1e-5
RMS_EPS = 1e-6
GLA_GATE_NORMALIZER = 16.0
TOP_K_INNER = 2

SUBLANES_V7X = 8
LANES_V7X = 128
VMEM_LIMIT_V7X = 56 * 1024 * 1024

NT_DIMS = (((1,), (1,)), ((), ()))
TN_DIMS = (((0,), (0,)), ((), ()))


def _params(*sem):
    return pltpu.CompilerParams(dimension_semantics=sem, vmem_limit_bytes=VMEM_LIMIT_V7X)


def _silu(x):
    return x * (1.0 / (1.0 + jnp.exp(-x)))


def _log_sigmoid(z):
    return jnp.minimum(z, 0.0) - jnp.log(1.0 + jnp.exp(-jnp.abs(z)))


def _layer_norm(x, g, b):
    mu = jnp.mean(x, axis=-1, keepdims=True)
    xc = x - mu
    var = jnp.mean(xc * xc, axis=-1, keepdims=True)
    return xc * lax.rsqrt(var + LN_EPS) * g + b


def _dot(a, b):
    return jnp.dot(a, b, preferred_element_type=F32)


def _split_bf16(x):
    hi = x.astype(BF16)
    lo = (x - hi.astype(F32)).astype(BF16)
    return hi, lo


def _to_token_tiles(h, out_ref, scr_ref):
    tm, d = h.shape
    nc = d // LANES_V7X
    for j in range(nc):
        scr_ref[j * tm:(j + 1) * tm, :] = h[:, j * LANES_V7X:(j + 1) * LANES_V7X]

    def body(t, c):
        out_ref[t] = scr_ref[pl.ds(t, nc, stride=tm), :].astype(BF16)
        return c

    lax.fori_loop(0, tm, body, 0, unroll=8)


def _from_token_tiles(in_ref, scr_ref, dtype):
    tm, nc, _ = in_ref.shape

    def body(t, c):
        scr_ref[pl.ds(t, nc, stride=tm), :] = in_ref[t].astype(F32)
        return c

    lax.fori_loop(0, tm, body, 0, unroll=8)
    return jnp.concatenate([scr_ref[j * tm:(j + 1) * tm, :].astype(dtype) for j in range(nc)], axis=-1)


def _mod_kernel(cp_ref, cs_ref, w_ref, b_ref, o_ref, *, n_prompt, groups_per_seq):
    w = w_ref[...].astype(BF16)
    b = b_ref[...]
    rp = _dot(_silu(cp_ref[...]).astype(BF16), w) + b
    rs = _dot(_silu(cs_ref[...]).astype(BF16), w) + b
    tn = o_ref.shape[-1]
    for s in range(n_prompt):
        o_ref[s * groups_per_seq:(s + 1) * groups_per_seq, :] = jnp.broadcast_to(
            rp[s:s + 1, :], (groups_per_seq, tn))
    o_ref[n_prompt * groups_per_seq:, :] = rs


def _mod_table(c_prompt, c_sample, w_mod, b_mod, seq):
    depth, d, n = w_mod.shape
    bp, bs = c_prompt.shape[0], c_sample.shape[0]
    gps = seq // SUBLANES_V7X
    g_total = bp * gps + bs
    cp = jnp.pad(c_prompt, ((0, (-bp) % SUBLANES_V7X), (0, 0)))
    tn = 1024
    return pl.pallas_call(
        functools.partial(_mod_kernel, n_prompt=bp, groups_per_seq=gps),
        grid=(depth, n // tn),
        in_specs=[
            pl.BlockSpec(cp.shape, lambda l, j: (0, 0)),
            pl.BlockSpec(c_sample.shape, lambda l, j: (0, 0)),
            pl.BlockSpec((None, d, tn), lambda l, j: (l, 0, j)),
            pl.BlockSpec((None, 1, tn), lambda l, j: (l, 0, j)),
        ],
        out_specs=pl.BlockSpec((None, g_total, tn), lambda l, j: (l, 0, j)),
        out_shape=jax.ShapeDtypeStruct((depth, g_total, n), F32),
        compiler_params=_params("arbitrary", "arbitrary"),
        name="mod",
    )(cp, c_sample, w_mod, b_mod.reshape(depth, 1, n))


def _mm_kernel(*refs, has_mod):
    if has_mod:
        x_ref, sh_ref, sc_ref, w_ref, o_ref, wb_ref = refs
    else:
        a_ref, w_ref, o_ref, wb_ref = refs

    @pl.when(pl.program_id(1) == 0)
    def _():
        wb_ref[...] = w_ref[...].astype(BF16)

    if has_mod:
        x = x_ref[...]
        g, s, k = x.shape
        h = x * (1.0 + sc_ref[...][:, None, :]) + sh_ref[...][:, None, :]
        a = h.reshape(g * s, k).astype(BF16)
    else:
        a = a_ref[...]
    o_ref[...] = _dot(a, wb_ref[...]).astype(o_ref.dtype)


def _mm(a, w3, w_idx, n_out, *, mod=None, tm=512, tn=1024, out_dtype=BF16):
    k = w3.shape[1]
    tn = min(tn, n_out)
    if mod is None:
        t = a.shape[0]
        a_specs = [pl.BlockSpec((tm, k), lambda j, i: (i, 0))]
        ins = [a]
    else:
        table, layer, sh_col, sc_col = mod
        t = a.shape[0] * SUBLANES_V7X
        gt = tm // SUBLANES_V7X
        a_specs = [
            pl.BlockSpec((gt, SUBLANES_V7X, k), lambda j, i: (i, 0, 0)),
            pl.BlockSpec((None, gt, k), lambda j, i: (layer, i, sh_col)),
            pl.BlockSpec((None, gt, k), lambda j, i: (layer, i, sc_col)),
        ]
        ins = [a, table, table]
    return pl.pallas_call(
        functools.partial(_mm_kernel, has_mod=mod is not None),
        grid=(n_out // tn, t // tm),
        in_specs=a_specs + [pl.BlockSpec((None, k, tn), lambda j, i: (w_idx, 0, j))],
        out_specs=pl.BlockSpec((tm, tn), lambda j, i: (i, j)),
        out_shape=jax.ShapeDtypeStruct((t, n_out), out_dtype),
        scratch_shapes=[pltpu.VMEM((k, tn), BF16)],
        compiler_params=_params("arbitrary", "arbitrary"),
        name="mm",
    )(*ins, w3)


def _mix0_kernel(bg_ref, cg_ref, hx_ref, u_ref, v_ref, cache_ref, cw_ref, vg_ref, vb_ref, wm_ref, bias_ref,
                 y_ref, convp_ref, convs_ref, vns_ref, zprev_ref, *, n_prompt_tiles, tiles_per_seq, n_heads):
    i = pl.program_id(0)
    tm, dc = bg_ref.shape
    ns = tm // SUBLANES_V7X
    z = cg_ref[...].astype(F32) * hx_ref[...].astype(F32)
    row = lax.broadcasted_iota(I32, (tm, dc), 0)
    r1 = pltpu.roll(z, 1, 0)
    r2 = pltpu.roll(z, 2, 0)
    cw = cw_ref[...]
    bg = bg_ref[...].astype(F32)

    vn = _layer_norm(v_ref[...].astype(F32), vg_ref[...], vb_ref[...])
    vnb = vn.astype(BF16)
    hd = dc // n_heads
    mixed = jnp.concatenate(
        [_dot(wm_ref[h], vnb[:, h * hd:(h + 1) * hd]) for h in range(n_heads)], axis=-1) + bias_ref[...]
    y_ref[:, dc:] = (u_ref[...].astype(F32) * mixed).astype(BF16)

    def conv_out(zm1, zm2):
        conv = cw[0:1, :] * zm2 + cw[1:2, :] * zm1 + cw[2:3, :] * z
        y_ref[:, :dc] = (bg * conv).astype(BF16)

    @pl.when(i < n_prompt_tiles)
    def _prompt():
        @pl.when(i % tiles_per_seq == 0)
        def _():
            zprev_ref[...] = jnp.zeros_like(zprev_ref)

        zp = zprev_ref[...]
        p1 = zp[SUBLANES_V7X - 1:SUBLANES_V7X, :]
        p2 = zp[SUBLANES_V7X - 2:SUBLANES_V7X - 1, :]
        conv_out(jnp.where(row == 0, p1, r1),
                 jnp.where(row == 0, p2, jnp.where(row == 1, p1, r2)))
        zprev_ref[...] = z[tm - SUBLANES_V7X:, :]
        convp_ref[...] = z[tm - 2:, :].reshape(1, 2, dc)

    @pl.when(i >= n_prompt_tiles)
    def _sample():
        c = cache_ref[...]
        c0 = jnp.broadcast_to(c[:, 0:1, :], (ns, SUBLANES_V7X, dc)).reshape(tm, dc)
        c1 = jnp.broadcast_to(c[:, 1:2, :], (ns, SUBLANES_V7X, dc)).reshape(tm, dc)
        rr = row % SUBLANES_V7X
        conv_out(jnp.where(rr == 0, c1, r1),
                 jnp.where(rr == 0, c0, jnp.where(rr == 1, c1, r2)))
        z3 = z.reshape(ns, SUBLANES_V7X, dc)
        convs_ref[...] = z3[:, SUBLANES_V7X - 2:, :]
        vns_ref[...] = vn.reshape(ns, SUBLANES_V7X, dc)


def _mix0(p, cache, conv_w, v_g, v_b, wm, bias, *, t_prompt, seq, n_heads):
    t, n = p.shape
    bs, cwm1, dc = cache.shape
    tm = wm.shape[-1]
    assert cwm1 == 2 and conv_w.shape[0] == 3 and n == 5 * dc and seq % tm == 0
    n_p = t_prompt // tm
    n_s = (t - t_prompt) // tm
    tps = seq // tm
    bp = t_prompt // seq
    ns = tm // SUBLANES_V7X

    def col(c):
        return pl.BlockSpec((tm, dc), lambda i: (i, c))

    def s_idx(i):
        return jnp.maximum(i - n_p, 0)

    const2 = lambda i: (0, 0)
    mode = lambda i: ((i >= n_p).astype(I32), 0, 0, 0)
    return pl.pallas_call(
        functools.partial(_mix0_kernel, n_prompt_tiles=n_p, tiles_per_seq=tps, n_heads=n_heads),
        grid=(n_p + n_s,),
        in_specs=[col(0), col(1), col(2), col(3), col(4),
                  pl.BlockSpec((ns, 2, dc), lambda i: (s_idx(i), 0, 0)),
                  pl.BlockSpec((3, dc), const2),
                  pl.BlockSpec((1, dc), const2),
                  pl.BlockSpec((1, dc), const2),
                  pl.BlockSpec((None, n_heads, tm, tm), mode),
                  pl.BlockSpec((None, tm, dc), lambda i: ((i >= n_p).astype(I32), 0, 0))],
        out_specs=[pl.BlockSpec((tm, 2 * dc), lambda i: (i, 0)),
                   pl.BlockSpec((1, 2, dc), lambda i: (jnp.minimum(i // tps, bp - 1), 0, 0)),
                   pl.BlockSpec((ns, 2, dc), lambda i: (s_idx(i), 0, 0)),
                   pl.BlockSpec((ns, SUBLANES_V7X, dc), lambda i: (s_idx(i), 0, 0))],
        out_shape=[jax.ShapeDtypeStruct((t, 2 * dc), BF16),
                   jax.ShapeDtypeStruct((bp, 2, dc), F32),
                   jax.ShapeDtypeStruct((bs, 2, dc), F32),
                   jax.ShapeDtypeStruct((bs, SUBLANES_V7X, dc), F32)],
        scratch_shapes=[pltpu.VMEM((SUBLANES_V7X, dc), F32)],
        compiler_params=_params("arbitrary"),
        name="mix0",
    )(p, p, p, p, p, cache, conv_w, v_g.reshape(1, dc), v_b.reshape(1, dc), wm, bias)


def _first_index_of(vals, target):
    idx = jnp.full(target.shape, len(vals) - 1, I32)
    for j in reversed(range(len(vals))):
        idx = jnp.where(vals[j] == target, j, idx)
    return idx


def _softmax_rows(rows):
    m = functools.reduce(jnp.maximum, rows)
    e = [jnp.exp(r - m) for r in rows]
    s = functools.reduce(lambda a, b: a + b, e)
    return [x / s for x in e]


def _route(h, wr_ref, br_ref, eid_ref, wt_ref, rank_ref, cnt_ref, carry_ref, n_groups, n_exp):
    tm = h.shape[0]
    hh, hl = _split_bf16(h)
    wh, wl = _split_bf16(wr_ref[...])
    dg = lambda a, b: lax.dot_general(a, b, NT_DIMS, preferred_element_type=F32)
    logits = dg(wh, hh) + dg(wh, hl) + dg(wl, hh) + br_ref[...]

    g_prob = _softmax_rows([logits[g:g + 1, :] for g in range(n_groups)])
    g_top = functools.reduce(jnp.maximum, g_prob)
    g_idx = _first_index_of(g_prob, g_top)

    e_sel = []
    for e in range(n_exp):
        sel = logits[n_groups + e:n_groups + e + 1, :]
        for g in range(1, n_groups):
            r = n_groups + g * n_exp + e
            sel = jnp.where(g_idx == g, logits[r:r + 1, :], sel)
        e_sel.append(sel)
    e_prob = _softmax_rows(e_sel)
    p1 = functools.reduce(jnp.maximum, e_prob)
    i1 = _first_index_of(e_prob, p1)
    rest = [jnp.where(i1 == e, -1.0, e_prob[e]) for e in range(n_exp)]
    p2 = functools.reduce(jnp.maximum, rest)
    i2 = _first_index_of(rest, p2)
    den = p1 + p2
    wt_ref[0:1, :] = g_top * (p1 / den)
    wt_ref[1:2, :] = g_top * (p2 / den)
    eid0 = g_idx * n_exp + i1
    eid1 = g_idx * n_exp + i2
    eid_ref[0:1, :] = eid0
    eid_ref[1:2, :] = eid1

    n_e = n_groups * n_exp
    eio = lax.broadcasted_iota(I32, (n_e, tm), 0)
    oh0 = (eio == eid0).astype(F32)
    oh1 = (eio == eid1).astype(F32)
    oh = oh0 + oh1
    before = (lax.broadcasted_iota(I32, (tm, tm), 0) < lax.broadcasted_iota(I32, (tm, tm), 1)).astype(BF16)
    base = _dot(oh.astype(BF16), before) + carry_ref[...]
    rank_ref[0:1, :] = jnp.sum(oh0 * base, axis=0, keepdims=True).astype(I32)
    rank_ref[1:2, :] = jnp.sum(oh1 * base, axis=0, keepdims=True).astype(I32)
    total = carry_ref[...] + jnp.sum(oh, axis=1, keepdims=True)
    carry_ref[...] = total
    cnt_ref[...] = jnp.broadcast_to(total, cnt_ref.shape).astype(I32)


def _outln_kernel(y_ref, w_ref, x_ref, gt_ref, sh_ref, sc_ref, lng_ref, lnb_ref, wr_ref, br_ref,
                  x1_ref, hq_ref, eid_ref, wt_ref, rank_ref, cnt_ref, carry_ref, scr_ref,
                  *, alpha, n_groups, n_exp):
    @pl.when(pl.program_id(0) == 0)
    def _():
        carry_ref[...] = jnp.zeros_like(carry_ref)

    g, s, d = x_ref.shape
    m = _dot(y_ref[...], w_ref[...]).reshape(g, s, d)
    x1 = _layer_norm(alpha * x_ref[...] + gt_ref[...][:, None, :] * m, lng_ref[...], lnb_ref[...])
    x1_ref[...] = x1
    h = (x1 * (1.0 + sc_ref[...][:, None, :]) + sh_ref[...][:, None, :]).reshape(g * s, d)
    _to_token_tiles(h, hq_ref, scr_ref)
    _route(h, wr_ref, br_ref, eid_ref, wt_ref, rank_ref, cnt_ref, carry_ref, n_groups, n_exp)


def _outln(y, w_bf, x, table, layer, ln_g, ln_b, wr, br, *, alpha, n_groups, n_exp, tm=256):
    t, k = y.shape
    d = w_bf.shape[1]
    gt = tm // SUBLANES_V7X
    nc = d // LANES_V7X
    n_e = n_groups * n_exp

    def mod(c):
        return pl.BlockSpec((None, gt, d), lambda i: (layer, i, c))

    const = lambda i: (0, 0)
    pair = lambda dt: jax.ShapeDtypeStruct((TOP_K_INNER, t), dt)
    pair_spec = pl.BlockSpec((TOP_K_INNER, tm), lambda i: (0, i))
    return pl.pallas_call(
        functools.partial(_outln_kernel, alpha=alpha, n_groups=n_groups, n_exp=n_exp),
        grid=(t // tm,),
        in_specs=[pl.BlockSpec((tm, k), lambda i: (i, 0)),
                  pl.BlockSpec((k, d), const, pipeline_mode=pl.Buffered(1)),
                  pl.BlockSpec((gt, SUBLANES_V7X, d), lambda i: (i, 0, 0)),
                  mod(2), mod(3), mod(4),
                  pl.BlockSpec((1, d), const),
                  pl.BlockSpec((1, d), const),
                  pl.BlockSpec(wr.shape, const),
                  pl.BlockSpec(br.shape, const)],
        out_specs=[pl.BlockSpec((gt, SUBLANES_V7X, d), lambda i: (i, 0, 0)),
                   pl.BlockSpec((tm, nc, LANES_V7X), lambda i: (i, 0, 0)),
                   pair_spec, pair_spec, pair_spec,
                   pl.BlockSpec((n_e, LANES_V7X), const)],
        out_shape=[jax.ShapeDtypeStruct(x.shape, F32),
                   jax.ShapeDtypeStruct((t, nc, LANES_V7X), BF16),
                   pair(I32), pair(F32), pair(I32),
                   jax.ShapeDtypeStruct((n_e, LANES_V7X), I32)],
        scratch_shapes=[pltpu.VMEM((n_e, 1), F32), pltpu.VMEM((nc * tm, LANES_V7X), F32)],
        compiler_params=_params("arbitrary"),
        name="outln",
    )(y, w_bf, x, table, table, table, ln_g.reshape(1, d), ln_b.reshape(1, d), wr, br)


def _dispatch_kernel(pos_ref, hq_hbm, xs_hbm, zero_ref, zsem, sem, *, chunk, t_total, n_zero_tiles, tg):
    i = pl.program_id(0)
    n = pl.num_programs(0)

    def chunk_wait():
        for _ in range(TOP_K_INNER):
            pltpu.make_async_copy(hq_hbm.at[pl.ds(0, chunk)], xs_hbm.at[pl.ds(0, chunk)], sem).wait()

    @pl.when(i == 0)
    def _():
        zero_ref[...] = jnp.zeros_like(zero_ref)
        zcopy = lambda j: pltpu.make_async_copy(zero_ref, xs_hbm.at[pl.ds(j * tg, tg)], zsem)

        def zstart(j, c):
            zcopy(j).start()
            return c

        def zwait(j, c):
            zcopy(j).wait()
            return c

        lax.fori_loop(0, n_zero_tiles, zstart, 0)
        lax.fori_loop(0, n_zero_tiles, zwait, 0)

    def body(r, c):
        tok = i * chunk + r
        for k in range(TOP_K_INNER):
            p = pos_ref[k * t_total + tok]
            pltpu.make_async_copy(hq_hbm.at[pl.ds(tok, 1)], xs_hbm.at[pl.ds(p, 1)], sem).start()
        return c

    lax.fori_loop(0, chunk, body, 0, unroll=8)

    @pl.when(i > 0)
    def _():
        chunk_wait()

    @pl.when(i == n - 1)
    def _():
        chunk_wait()


def _dispatch(pos_flat, hq, *, n_tiles, tg, chunk=256):
    t, nc, lanes = hq.shape
    grid_spec = pltpu.PrefetchScalarGridSpec(
        num_scalar_prefetch=1,
        grid=(t // chunk,),
        in_specs=[pl.BlockSpec(memory_space=pl.ANY)],
        out_specs=pl.BlockSpec(memory_space=pl.ANY),
        scratch_shapes=[pltpu.VMEM((tg, nc, lanes), BF16),
                        pltpu.SemaphoreType.DMA(()), pltpu.SemaphoreType.DMA(())],
    )
    return pl.pallas_call(
        functools.partial(_dispatch_kernel, chunk=chunk, t_total=t, n_zero_tiles=n_tiles, tg=tg),
        grid_spec=grid_spec,
        out_shape=jax.ShapeDtypeStruct((n_tiles * tg, nc, lanes), BF16),
        compiler_params=_params("arbitrary"),
        name="dispatch",
    )(pos_flat, hq)


def _moe_kernel(te_ref, nu_ref, xs_ref, w1_ref, w3_ref, w2_ref, o_ref, w1b, w3b, w2b, scr_ref):
    i = pl.program_id(0)
    n_used = nu_ref[0]

    @pl.when(i < n_used)
    def _():
        @pl.when((i == 0) | (te_ref[i] != te_ref[jnp.maximum(i - 1, 0)]))
        def _():
            w1b[...] = w1_ref[...].astype(BF16)
            w3b[...] = w3_ref[...].astype(BF16)
            w2b[...] = w2_ref[...].astype(BF16)

        x = _from_token_tiles(xs_ref, scr_ref, BF16)
        a = _dot(x, w1b[...])
        b = _dot(x, w3b[...])
        y = _dot((_silu(a) * b).astype(BF16), w2b[...])
        _to_token_tiles(y, o_ref, scr_ref)

    @pl.when(i >= n_used)
    def _():
        o_ref[...] = jnp.zeros_like(o_ref)


def _moe(xs, w1, w3, w2, w_base, te, n_used, *, tg):
    p_tot, nc, lanes = xs.shape
    d = nc * lanes
    f = w1.shape[-1]
    n_tiles = p_tot // tg
    w_in = pl.BlockSpec((None, d, f), lambda i, te, nu: (w_base + te[i], 0, 0))
    w_out = pl.BlockSpec((None, f, d), lambda i, te, nu: (w_base + te[i], 0, 0))
    grid_spec = pltpu.PrefetchScalarGridSpec(
        num_scalar_prefetch=2,
        grid=(n_tiles,),
        in_specs=[pl.BlockSpec((tg, nc, lanes), lambda i, te, nu: (jnp.minimum(i, nu[0] - 1), 0, 0)),
                  w_in, w_in, w_out],
        out_specs=pl.BlockSpec((tg, nc, lanes), lambda i, te, nu: (i, 0, 0)),
        scratch_shapes=[pltpu.VMEM((d, f), BF16), pltpu.VMEM((d, f), BF16), pltpu.VMEM((f, d), BF16),
                        pltpu.VMEM((nc * tg, lanes), F32)],
    )
    return pl.pallas_call(
        _moe_kernel,
        grid_spec=grid_spec,
        out_shape=jax.ShapeDtypeStruct(xs.shape, BF16),
        compiler_params=_params("arbitrary"),
        name="moe",
    )(te, n_used, xs, w1, w3, w2)


def _row_gather_start(src_hbm, dst, sem, idx_ref, base, n):
    def body(r, c):
        row = idx_ref[base + r]
        pltpu.make_async_copy(src_hbm.at[pl.ds(row, 1)], dst.at[pl.ds(r, 1)], sem).start()
        return c
    lax.fori_loop(0, n, body, 0, unroll=8)


def _row_gather_wait(src_hbm, dst, sem, n):
    pltpu.make_async_copy(src_hbm.at[pl.ds(0, n)], dst, sem).wait()


def _comb_kernel(pos_ref, y_hbm, wt_ref, x_ref, gt_ref, lng_ref, lnb_ref, *rest, alpha, tm, t_total, has_next):
    if has_next:
        shn_ref, scn_ref, x2_ref, hn_ref, ybuf, scr_ref, sem = rest
    else:
        x2_ref, ybuf, scr_ref, sem = rest
    i = pl.program_id(0)
    n = pl.num_programs(0)

    def start(tile, slot):
        for k in range(TOP_K_INNER):
            _row_gather_start(y_hbm, ybuf.at[slot, k], sem.at[slot], pos_ref, k * t_total + tile * tm, tm)

    @pl.when(i == 0)
    def _():
        start(0, 0)

    slot = i % 2

    @pl.when(i + 1 < n)
    def _():
        start(i + 1, 1 - slot)

    for k in range(TOP_K_INNER):
        _row_gather_wait(y_hbm, ybuf.at[slot, k], sem.at[slot], tm)
    w = wt_ref[...]
    f = w[:, 0:1] * _from_token_tiles(ybuf.at[slot, 0], scr_ref, F32)
    f = f + w[:, 1:2] * _from_token_tiles(ybuf.at[slot, 1], scr_ref, F32)
    g, s, d = x_ref.shape
    x2 = _layer_norm(alpha * x_ref[...] + gt_ref[...][:, None, :] * f.reshape(g, s, d), lng_ref[...], lnb_ref[...])
    x2_ref[...] = x2
    if has_next:
        hn = x2 * (1.0 + scn_ref[...][:, None, :]) + shn_ref[...][:, None, :]
        hn_ref[...] = hn.reshape(g * s, d).astype(BF16)


def _comb(pos_flat, ys, wt_t, x1, table, layer, ln_g, ln_b, *, alpha, has_next, tm=256):
    g_total, s, d = x1.shape
    t = g_total * s
    gt = tm // SUBLANES_V7X
    nc = d // LANES_V7X

    def mod(l, c):
        return pl.BlockSpec((None, gt, d), lambda i, pos: (l, i, c))

    xspec = pl.BlockSpec((gt, s, d), lambda i, pos: (i, 0, 0))
    vec = pl.BlockSpec((1, d), lambda i, pos: (0, 0))
    in_specs = [pl.BlockSpec(memory_space=pl.ANY),
                pl.BlockSpec((tm, TOP_K_INNER), lambda i, pos: (i, 0)),
                xspec, mod(layer, 5), vec, vec]
    ins = [ys, wt_t, x1, table, ln_g.reshape(1, d), ln_b.reshape(1, d)]
    out_specs = [xspec]
    out_shape = [jax.ShapeDtypeStruct(x1.shape, F32)]
    if has_next:
        in_specs += [mod(layer + 1, 0), mod(layer + 1, 1)]
        ins += [table, table]
        out_specs.append(pl.BlockSpec((tm, d), lambda i, pos: (i, 0)))
        out_shape.append(jax.ShapeDtypeStruct((t, d), BF16))
    grid_spec = pltpu.PrefetchScalarGridSpec(
        num_scalar_prefetch=1,
        grid=(t // tm,),
        in_specs=in_specs,
        out_specs=out_specs,
        scratch_shapes=[pltpu.VMEM((2, TOP_K_INNER, tm, nc, LANES_V7X), BF16),
                        pltpu.VMEM((nc * tm, LANES_V7X), F32),
                        pltpu.SemaphoreType.DMA((2,))],
    )
    return pl.pallas_call(
        functools.partial(_comb_kernel, alpha=alpha, tm=tm, t_total=t, has_next=has_next),
        grid_spec=grid_spec,
        out_shape=out_shape,
        compiler_params=_params("arbitrary"),
        name="comb",
    )(pos_flat, *ins)


def _gla_decay(gl, wgk, bg_row, wgk_t, bg_col):
    la = _log_sigmoid(_dot(gl, wgk) + bg_row) * (1.0 / GLA_GATE_NORMALIZER)
    la_t = _log_sigmoid(lax.dot_general(wgk_t, gl, NT_DIMS, preferred_element_type=F32) + bg_col) * (
        1.0 / GLA_GATE_NORMALIZER)
    return la, la_t


def _rms_gate(o, ng, gate):
    on = o * lax.rsqrt(jnp.mean(o * o, axis=-1, keepdims=True) + RMS_EPS) * ng
    return (on * _silu(gate)).astype(BF16)


def _gla_prompt_kernel(q_ref, k_ref, v_ref, g_ref, gl_ref, wgk_ref, wgkt_ref, bgr_ref, bgc_ref, ng_ref,
                       y_ref, sout_ref, st_ref, *, n_heads, scale):
    j = pl.program_id(1)

    @pl.when(j == 0)
    def _():
        st_ref[...] = jnp.zeros_like(st_ref)

    bk = q_ref.shape[0]
    dk = q_ref.shape[1] // n_heads
    dv = v_ref.shape[1] // n_heads
    gl = gl_ref[...]
    causal = lax.broadcasted_iota(I32, (bk, bk), 0) >= lax.broadcasted_iota(I32, (bk, bk), 1)
    tri = causal.astype(BF16)
    for h in range(n_heads):
        ks, vs = slice(h * dk, (h + 1) * dk), slice(h * dv, (h + 1) * dv)
        q = q_ref[:, ks].astype(F32) * scale
        k = k_ref[:, ks].astype(F32)
        v = v_ref[:, vs]
        la, la_t = _gla_decay(gl, wgk_ref[:, ks], bgr_ref[:, ks], wgkt_ref[ks, :], bgc_ref[ks, :])
        la_hi, la_lo = _split_bf16(la)
        cum = _dot(tri, la_hi) + _dot(tri, la_lo)
        last = cum[bk - 1:bk, :]
        last_t = jnp.sum(la_t, axis=1, keepdims=True)
        qs = (q * jnp.exp(cum)).astype(BF16)
        kn = (k * jnp.exp(-cum)).astype(BF16)
        sc = lax.dot_general(qs, kn, NT_DIMS, preferred_element_type=F32)
        sc = jnp.where(causal, sc, 0.0).astype(BF16)
        s_old = st_ref[h]
        o = _dot(sc, v) + _dot(qs, s_old.astype(BF16))
        kd = (k * jnp.exp(last - cum)).astype(BF16)
        st_ref[h] = s_old * jnp.exp(last_t) + lax.dot_general(kd, v, TN_DIMS, preferred_element_type=F32)
        y_ref[:, vs] = _rms_gate(o, ng_ref[...], g_ref[:, vs].astype(F32))

    @pl.when(j == pl.num_programs(1) - 1)
    def _():
        sout_ref[...] = st_ref[...]


def _gla_sample_kernel(q_ref, k_ref, v_ref, g_ref, gl_ref, wgk_ref, wgkt_ref, bgr_ref, bgc_ref, ng_ref, sin_ref,
                       y_ref, sout_ref, *, n_heads, scale, seq):
    rows = q_ref.shape[0]
    nb = rows // seq
    dk = q_ref.shape[1] // n_heads
    dv = v_ref.shape[1] // n_heads
    gl = gl_ref[...]
    ri = lax.broadcasted_iota(I32, (rows, rows), 0)
    ci = lax.broadcasted_iota(I32, (rows, rows), 1)
    same = (ri // seq) == (ci // seq)
    causal = same & (ri >= ci)
    tri = causal.astype(BF16)
    ones_bd = same.astype(BF16)
    lane_seq = lax.broadcasted_iota(I32, (dk, rows), 1) // seq
    for h in range(n_heads):
        ks, vs = slice(h * dk, (h + 1) * dk), slice(h * dv, (h + 1) * dv)
        q = q_ref[:, ks].astype(F32) * scale
        k = k_ref[:, ks].astype(F32)
        v = v_ref[:, vs]
        la, la_t = _gla_decay(gl, wgk_ref[:, ks], bgr_ref[:, ks], wgkt_ref[ks, :], bgc_ref[ks, :])
        la_hi, la_lo = _split_bf16(la)
        cum = _dot(tri, la_hi) + _dot(tri, la_lo)
        last = _dot(ones_bd, la_hi) + _dot(ones_bd, la_lo)
        q_dec = q * jnp.exp(cum)
        qs = q_dec.astype(BF16)
        kn = (k * jnp.exp(-cum)).astype(BF16)
        sc = lax.dot_general(qs, kn, NT_DIMS, preferred_element_type=F32)
        sc = jnp.where(causal, sc, 0.0).astype(BF16)
        kd = k * jnp.exp(last - cum)
        v32 = v.astype(F32)
        o_state = []
        for s in range(nb):
            rs = slice(s * seq, (s + 1) * seq)
            s_old = sin_ref[s, h]
            o_state.append(_dot(q_dec[rs, :].astype(BF16), s_old.astype(BF16)))
            last_t = jnp.sum(jnp.where(lane_seq == s, la_t, 0.0), axis=1, keepdims=True)
            sout_ref[s, h] = s_old * jnp.exp(last_t) + lax.dot_general(
                kd[rs, :].astype(BF16), v32[rs, :].astype(BF16), TN_DIMS, preferred_element_type=F32)
        o = _dot(sc, v) + jnp.concatenate(o_state, axis=0)
        y_ref[:, vs] = _rms_gate(o, ng_ref[...], g_ref[:, vs].astype(F32))


def _gla(p, gl, wgk, wgk_t, bg_row, bg_col, ng, state_s, *, t_prompt, seq, n_heads, dk, dv, bk=128, nb=4):
    t = p.shape[0]
    bs, _, _, _ = state_s.shape
    s_len = (t - t_prompt) // bs
    bp = t_prompt // seq
    dkt, dvt = n_heads * dk, n_heads * dv
    assert dvt == 2 * dkt
    scale = dk ** -0.5
    r = gl.shape[1]
    const = lambda *_: (0, 0)
    w_specs = [pl.BlockSpec((r, dkt), const), pl.BlockSpec((dkt, r), const),
               pl.BlockSpec((1, dkt), const), pl.BlockSpec((dkt, 1), const), pl.BlockSpec((1, dv), const)]
    w_ins = [wgk, wgk_t, bg_row, bg_col, ng]

    nblk = seq // bk
    rowp = lambda b, j: b * nblk + j
    y_p, s_p = pl.pallas_call(
        functools.partial(_gla_prompt_kernel, n_heads=n_heads, scale=scale),
        grid=(bp, nblk),
        in_specs=[pl.BlockSpec((bk, dkt), lambda b, j: (rowp(b, j), 0)),
                  pl.BlockSpec((bk, dkt), lambda b, j: (rowp(b, j), 1)),
                  pl.BlockSpec((bk, dvt), lambda b, j: (rowp(b, j), 1)),
                  pl.BlockSpec((bk, dvt), lambda b, j: (rowp(b, j), 2)),
                  pl.BlockSpec((bk, r), lambda b, j: (rowp(b, j), 0))] + w_specs,
        out_specs=[pl.BlockSpec((bk, dvt), lambda b, j: (rowp(b, j), 0)),
                   pl.BlockSpec((None, n_heads, dk, dv), lambda b, j: (b, 0, 0, 0))],
        out_shape=[jax.ShapeDtypeStruct((t_prompt, dvt), BF16),
                   jax.ShapeDtypeStruct((bp, n_heads, dk, dv), F32)],
        scratch_shapes=[pltpu.VMEM((n_heads, dk, dv), F32)],
        compiler_params=_params("arbitrary", "arbitrary"),
        name="gla_prompt",
    )(p, p, p, p, gl, *w_ins)

    rows = nb * s_len
    off = t_prompt // rows
    y_s, s_s = pl.pallas_call(
        functools.partial(_gla_sample_kernel, n_heads=n_heads, scale=scale, seq=s_len),
        grid=(bs // nb,),
        in_specs=[pl.BlockSpec((rows, dkt), lambda i: (off + i, 0)),
                  pl.BlockSpec((rows, dkt), lambda i: (off + i, 1)),
                  pl.BlockSpec((rows, dvt), lambda i: (off + i, 1)),
                  pl.BlockSpec((rows, dvt), lambda i: (off + i, 2)),
                  pl.BlockSpec((rows, r), lambda i: (off + i, 0))] + w_specs + [
                  pl.BlockSpec((nb, n_heads, dk, dv), lambda i: (i, 0, 0, 0))],
        out_specs=[pl.BlockSpec((rows, dvt), lambda i: (i, 0)),
                   pl.BlockSpec((nb, n_heads, dk, dv), lambda i: (i, 0, 0, 0))],
        out_shape=[jax.ShapeDtypeStruct((t - t_prompt, dvt), BF16),
                   jax.ShapeDtypeStruct(state_s.shape, F32)],
        compiler_params=_params("arbitrary"),
        name="gla_sample",
    )(p, p, p, p, gl, *w_ins, state_s)
    return jnp.concatenate([y_p, y_s], axis=0), s_p, s_s


def _moe_schedule(eid, rank, counts, *, tg, n_tiles):
    n_e = counts.shape[0]
    e_ids = jnp.arange(n_e, dtype=I32)
    padded = ((counts + tg - 1) // tg) * tg
    ends = jnp.sum(jnp.where(e_ids[None, :] <= e_ids[:, None], padded[None, :], 0), axis=1)
    starts = ends - padded
    pos = jnp.sum(jnp.where(eid[None] == e_ids[:, None, None], starts[:, None, None], 0), axis=0) + rank
    n_used = ends[n_e - 1] // tg
    tile_start = jnp.arange(n_tiles, dtype=I32) * tg
    te = jnp.sum((ends[None, :] <= tile_start[:, None]).astype(I32), axis=1)
    te_last = jnp.sum((ends <= (n_used - 1) * tg).astype(I32))
    te = jnp.where(jnp.arange(n_tiles) < n_used, te, te_last)
    return pos.reshape(-1).astype(I32), te.astype(I32), n_used.reshape(1).astype(I32)


def kernel(x_prompt, x_sample, cache_conv, state_gla, c_prompt, c_sample, w_mod, b_mod, ln_g, ln_b, ab_w_in, ab_conv_w, ab_v_ln_g, ab_v_ln_b, ab_w_s, ab_b_s, ab_w_out, gla_w_in, gla_w_gk, gla_b_gk, gla_norm_g, gla_w_out, moe_w_grp, moe_b_grp, moe_w_rt, moe_b_rt, moe_w1, moe_w3, moe_w2):
    bp, seq, d = x_prompt.shape
    bs, s_len, _ = x_sample.shape
    assert s_len == SUBLANES_V7X and seq % SUBLANES_V7X == 0
    depth = w_mod.shape[0]
    alpha = float((2 * depth) ** 0.25)
    t_p, t_s = bp * seq, bs * s_len
    t = t_p + t_s
    n_groups, n_exp = moe_w_rt.shape[1], moe_w_rt.shape[3]
    n_e = n_groups * n_exp
    d_ff = moe_w1.shape[-1]
    tg = 256
    n_tiles = (TOP_K_INNER * t) // tg + n_e

    x = jnp.concatenate([x_prompt.reshape(t_p, d), x_sample.reshape(t_s, d)], axis=0)
    x = x.reshape(t // SUBLANES_V7X, SUBLANES_V7X, d)
    table = _mod_table(c_prompt, c_sample, w_mod, b_mod, seq)

    w1 = moe_w1.reshape(depth * n_e, d, d_ff)
    w3 = moe_w3.reshape(depth * n_e, d, d_ff)
    w2 = moe_w2.reshape(depth * n_e, d_ff, d)

    conv_p, conv_s, chunk_v, gla_p, gla_s = [], [], [], [], []
    h_bf = None
    for layer in range(depth):
        li = layer // 2
        if layer % 2 == 0:
            n_heads, chunk = ab_w_s.shape[1], ab_w_s.shape[2]
            dc = ab_conv_w.shape[-1]
            if h_bf is None:
                p = _mm(x, ab_w_in, li, ab_w_in.shape[-1], mod=(table, layer, 0, 1))
            else:
                p = _mm(h_bf, ab_w_in, li, ab_w_in.shape[-1])
            w_s = ab_w_s[li]
            wm_p = jnp.tril(w_s)
            reps = chunk // s_len
            blk = jnp.tril(w_s[:, :s_len, :s_len])
            wm_s = jnp.einsum("ab,hts->hatbs", jnp.eye(reps, dtype=F32), blk).reshape(n_heads, chunk, chunk)
            wm = jnp.stack([wm_p, wm_s]).astype(BF16)
            b_s = ab_b_s[li]
            hd = dc // n_heads
            bias_p = jnp.repeat(b_s.T, hd, axis=1)
            bias_s = jnp.repeat(jnp.tile(b_s[:, :s_len].T, (reps, 1)), hd, axis=1)
            bias = jnp.stack([bias_p, bias_s])
            y, cp_new, cs_new, vn_s = _mix0(p, cache_conv[li], ab_conv_w[li], ab_v_ln_g[li], ab_v_ln_b[li],
                                            wm, bias, t_prompt=t_p, seq=seq, n_heads=n_heads)
            conv_p.append(cp_new)
            conv_s.append(cs_new)
            chunk_v.append(vn_s)
            w_out = ab_w_out[li].astype(BF16)
        else:
            n_heads, dk, dv = state_gla.shape[2], state_gla.shape[3], state_gla.shape[4]
            dkt, dvt = n_heads * dk, n_heads * dv
            rank = gla_w_gk.shape[1]
            n_main = 2 * dkt + 2 * dvt
            p = _mm(h_bf, gla_w_in, li, n_main)
            w_lo = jnp.pad(gla_w_in[li][:, n_main:], ((0, 0), (0, LANES_V7X - rank)))[None]
            gl = _mm(h_bf, w_lo, 0, LANES_V7X)
            wgk = jnp.pad(gla_w_gk[li], ((0, LANES_V7X - rank), (0, 0))).astype(BF16)
            y, sp_new, ss_new = _gla(p, gl, wgk, wgk.T, gla_b_gk[li].reshape(1, dkt), gla_b_gk[li].reshape(dkt, 1),
                                     gla_norm_g[li].reshape(1, dv), state_gla[li],
                                     t_prompt=t_p, seq=seq, n_heads=n_heads, dk=dk, dv=dv)
            gla_p.append(sp_new)
            gla_s.append(ss_new)
            w_out = gla_w_out[li].astype(BF16)

        wr = jnp.concatenate([moe_w_grp[layer].T,
                              jnp.transpose(moe_w_rt[layer], (0, 2, 1)).reshape(n_e, d)], axis=0)
        wr = jnp.pad(wr, ((0, LANES_V7X - wr.shape[0]), (0, 0)))
        br = jnp.concatenate([moe_b_grp[layer], moe_b_rt[layer].reshape(n_e)])
        br = jnp.pad(br, (0, LANES_V7X - br.shape[0])).reshape(LANES_V7X, 1)
        x1, hq, eid, wt, rank_, cnt = _outln(y, w_out, x, table, layer, ln_g[layer, 0], ln_b[layer, 0], wr, br,
                                             alpha=alpha, n_groups=n_groups, n_exp=n_exp)
        pos, te, n_used = _moe_schedule(eid, rank_, cnt[:, 0], tg=tg, n_tiles=n_tiles)
        xs = _dispatch(pos, hq, n_tiles=n_tiles, tg=tg)
        ys = _moe(xs, w1, w3, w2, layer * n_e, te, n_used, tg=tg)
        has_next = layer + 1 < depth
        outs = _comb(pos, ys, wt.T, x1, table, layer, ln_g[layer, 1], ln_b[layer, 1],
                     alpha=alpha, has_next=has_next)
        x = outs[0]
        h_bf = outs[1] if has_next else None

    xf = x.reshape(t, d)
    y_prompt = xf[:t_p].reshape(bp, seq, d)
    y_sample = xf[t_p:].reshape(bs, s_len, d)
    return (y_prompt, y_sample, jnp.stack(conv_p), jnp.stack(conv_s), jnp.stack(chunk_v),
            jnp.stack(gla_p), jnp.stack(gla_s))
```

```python
import functools

import jax
import jax.numpy as jnp
from jax import lax
from jax.experimental import pallas as pl
from jax.experimental.pallas import tpu as pltpu

F32 = jnp.float32
BF16 = jnp.bfloat16
I32 = jnp.int32

LN_EPS = 1e-5
RMS_EPS = 1e-6
GLA_GATE_NORMALIZER = 16.0
TOP_K_INNER = 2

SUBLANES_V7X = 8
LANES_V7X = 128
VMEM_LIMIT_V7X = 56 * 1024 * 1024

NT_DIMS = (((1,), (1,)), ((), ()))
TN_DIMS = (((0,), (0,)), ((), ()))


def _params(*sem):
    return pltpu.CompilerParams(dimension_semantics=sem, vmem_limit_bytes=VMEM_LIMIT_V7X)


def _silu(x):
    return x * (1.0 / (1.0 + jnp.exp(-x)))


def _log_sigmoid(z):
    return jnp.minimum(z, 0.0) - jnp.log(1.0 + jnp.exp(-jnp.abs(z)))


def _layer_norm(x, g, b):
    mu = jnp.mean(x, axis=-1, keepdims=True)
    xc = x - mu
    var = jnp.mean(xc * xc, axis=-1, keepdims=True)
    return xc * lax.rsqrt(var + LN_EPS) * g + b


def _dot(a, b):
    return jnp.dot(a, b, preferred_element_type=F32)


def _split_bf16(x):
    hi = x.astype(BF16)
    lo = (x - hi.astype(F32)).astype(BF16)
    return hi, lo


def _mod_kernel(cp_ref, cs_ref, w_ref, b_ref, o_ref, *, n_prompt, groups_per_seq):
    w = w_ref[...].astype(BF16)
    b = b_ref[...]
    rp = _dot(_silu(cp_ref[...]).astype(BF16), w) + b
    rs = _dot(_silu(cs_ref[...]).astype(BF16), w) + b
    tn = o_ref.shape[-1]
    for s in range(n_prompt):
        o_ref[s * groups_per_seq:(s + 1) * groups_per_seq, :] = jnp.broadcast_to(
            rp[s:s + 1, :], (groups_per_seq, tn))
    o_ref[n_prompt * groups_per_seq:, :] = rs


def _mod_table(c_prompt, c_sample, w_mod, b_mod, seq):
    depth, d, n = w_mod.shape
    bp, bs = c_prompt.shape[0], c_sample.shape[0]
    gps = seq // SUBLANES_V7X
    g_total = bp * gps + bs
    cp = jnp.pad(c_prompt, ((0, (-bp) % SUBLANES_V7X), (0, 0)))
    tn = 1024
    return pl.pallas_call(
        functools.partial(_mod_kernel, n_prompt=bp, groups_per_seq=gps),
        grid=(depth, n // tn),
        in_specs=[
            pl.BlockSpec(cp.shape, lambda l, j: (0, 0)),
            pl.BlockSpec(c_sample.shape, lambda l, j: (0, 0)),
            pl.BlockSpec((None, d, tn), lambda l, j: (l, 0, j)),
            pl.BlockSpec((None, 1, tn), lambda l, j: (l, 0, j)),
        ],
        out_specs=pl.BlockSpec((None, g_total, tn), lambda l, j: (l, 0, j)),
        out_shape=jax.ShapeDtypeStruct((depth, g_total, n), F32),
        compiler_params=_params("arbitrary", "arbitrary"),
        name="mod",
    )(cp, c_sample, w_mod, b_mod.reshape(depth, 1, n))


def _mm_kernel(*refs, has_mod):
    if has_mod:
        x_ref, sh_ref, sc_ref, w_ref, o_ref, wb_ref = refs
    else:
        a_ref, w_ref, o_ref, wb_ref = refs

    @pl.when(pl.program_id(1) == 0)
    def _():
        wb_ref[...] = w_ref[...].astype(BF16)

    if has_mod:
        x = x_ref[...]
        g, s, k = x.shape
        h = x * (1.0 + sc_ref[...][:, None, :]) + sh_ref[...][:, None, :]
        a = h.reshape(g * s, k).astype(BF16)
    else:
        a = a_ref[...]
    o_ref[...] = _dot(a, wb_ref[...]).astype(o_ref.dtype)


def _mm(a, w3, w_idx, n_out, *, mod=None, tm=512, tn=1024, out_dtype=BF16):
    k = w3.shape[1]
    tn = min(tn, n_out)
    if mod is None:
        t = a.shape[0]
        a_specs = [pl.BlockSpec((tm, k), lambda j, i: (i, 0))]
        ins = [a]
    else:
        table, layer, sh_col, sc_col = mod
        t = a.shape[0] * SUBLANES_V7X
        gt = tm // SUBLANES_V7X
        a_specs = [
            pl.BlockSpec((gt, SUBLANES_V7X, k), lambda j, i: (i, 0, 0)),
            pl.BlockSpec((None, gt, k), lambda j, i: (layer, i, sh_col)),
            pl.BlockSpec((None, gt, k), lambda j, i: (layer, i, sc_col)),
        ]
        ins = [a, table, table]
    return pl.pallas_call(
        functools.partial(_mm_kernel, has_mod=mod is not None),
        grid=(n_out // tn, t // tm),
        in_specs=a_specs + [pl.BlockSpec((None, k, tn), lambda j, i: (w_idx, 0, j))],
        out_specs=pl.BlockSpec((tm, tn), lambda j, i: (i, j)),
        out_shape=jax.ShapeDtypeStruct((t, n_out), out_dtype),
        scratch_shapes=[pltpu.VMEM((k, tn), BF16)],
        compiler_params=_params("arbitrary", "arbitrary"),
        name="mm",
    )(*ins, w3)


def _mix0_kernel(bg_ref, cg_ref, hx_ref, u_ref, v_ref, cache_ref, cw_ref, vg_ref, vb_ref, wm_ref, bias_ref,
                 y_ref, convp_ref, convs_ref, vns_ref, zprev_ref, *, n_prompt_tiles, tiles_per_seq, n_heads):
    i = pl.program_id(0)
    tm, dc = bg_ref.shape
    ns = tm // SUBLANES_V7X
    z = cg_ref[...].astype(F32) * hx_ref[...].astype(F32)
    row = lax.broadcasted_iota(I32, (tm, dc), 0)
    r1 = pltpu.roll(z, 1, 0)
    r2 = pltpu.roll(z, 2, 0)
    cw = cw_ref[...]
    bg = bg_ref[...].astype(F32)

    vn = _layer_norm(v_ref[...].astype(F32), vg_ref[...], vb_ref[...])
    vnb = vn.astype(BF16)
    hd = dc // n_heads
    mixed = jnp.concatenate(
        [_dot(wm_ref[h], vnb[:, h * hd:(h + 1) * hd]) for h in range(n_heads)], axis=-1) + bias_ref[...]
    y_ref[:, dc:] = (u_ref[...].astype(F32) * mixed).astype(BF16)

    def conv_out(zm1, zm2):
        conv = cw[0:1, :] * zm2 + cw[1:2, :] * zm1 + cw[2:3, :] * z
        y_ref[:, :dc] = (bg * conv).astype(BF16)

    @pl.when(i < n_prompt_tiles)
    def _prompt():
        @pl.when(i % tiles_per_seq == 0)
        def _():
            zprev_ref[...] = jnp.zeros_like(zprev_ref)

        zp = zprev_ref[...]
        p1 = zp[SUBLANES_V7X - 1:SUBLANES_V7X, :]
        p2 = zp[SUBLANES_V7X - 2:SUBLANES_V7X - 1, :]
        conv_out(jnp.where(row == 0, p1, r1),
                 jnp.where(row == 0, p2, jnp.where(row == 1, p1, r2)))
        zprev_ref[...] = z[tm - SUBLANES_V7X:, :]
        convp_ref[...] = z[tm - 2:, :].reshape(1, 2, dc)

    @pl.when(i >= n_prompt_tiles)
    def _sample():
        c = cache_ref[...]
        c0 = jnp.broadcast_to(c[:, 0:1, :], (ns, SUBLANES_V7X, dc)).reshape(tm, dc)
        c1 = jnp.broadcast_to(c[:, 1:2, :], (ns, SUBLANES_V7X, dc)).reshape(tm, dc)
        rr = row % SUBLANES_V7X
        conv_out(jnp.where(rr == 0, c1, r1),
                 jnp.where(rr == 0, c0, jnp.where(rr == 1, c1, r2)))
        z3 = z.reshape(ns, SUBLANES_V7X, dc)
        convs_ref[...] = z3[:, SUBLANES_V7X - 2:, :]
        vns_ref[...] = vn.reshape(ns, SUBLANES_V7X, dc)


def _mix0(p, cache, conv_w, v_g, v_b, wm, bias, *, t_prompt, seq, n_heads):
    t, n = p.shape
    bs, cwm1, dc = cache.shape
    tm = wm.shape[-1]
    assert cwm1 == 2 and conv_w.shape[0] == 3 and n == 5 * dc and seq % tm == 0
    n_p = t_prompt // tm
    n_s = (t - t_prompt) // tm
    tps = seq // tm
    bp = t_prompt // seq
    ns = tm // SUBLANES_V7X

    def col(c):
        return pl.BlockSpec((tm, dc), lambda i: (i, c))

    def s_idx(i):
        return jnp.maximum(i - n_p, 0)

    const2 = lambda i: (0, 0)
    mode = lambda i: ((i >= n_p).astype(I32), 0, 0, 0)
    return pl.pallas_call(
        functools.partial(_mix0_kernel, n_prompt_tiles=n_p, tiles_per_seq=tps, n_heads=n_heads),
        grid=(n_p + n_s,),
        in_specs=[col(0), col(1), col(2), col(3), col(4),
                  pl.BlockSpec((ns, 2, dc), lambda i: (s_idx(i), 0, 0)),
                  pl.BlockSpec((3, dc), const2),
                  pl.BlockSpec((1, dc), const2),
                  pl.BlockSpec((1, dc), const2),
                  pl.BlockSpec((None, n_heads, tm, tm), mode),
                  pl.BlockSpec((None, tm, dc), lambda i: ((i >= n_p).astype(I32), 0, 0))],
        out_specs=[pl.BlockSpec((tm, 2 * dc), lambda i: (i, 0)),
                   pl.BlockSpec((1, 2, dc), lambda i: (jnp.minimum(i // tps, bp - 1), 0, 0)),
                   pl.BlockSpec((ns, 2, dc), lambda i: (s_idx(i), 0, 0)),
                   pl.BlockSpec((ns, SUBLANES_V7X, dc), lambda i: (s_idx(i), 0, 0))],
        out_shape=[jax.ShapeDtypeStruct((t, 2 * dc), BF16),
                   jax.ShapeDtypeStruct((bp, 2, dc), F32),
                   jax.ShapeDtypeStruct((bs, 2, dc), F32),
                   jax.ShapeDtypeStruct((bs, SUBLANES_V7X, dc), F32)],
        scratch_shapes=[pltpu.VMEM((SUBLANES_V7X, dc), F32)],
        compiler_params=_params("arbitrary"),
        name="mix0",
    )(p, p, p, p, p, cache, conv_w, v_g.reshape(1, dc), v_b.reshape(1, dc), wm, bias)


def _first_index_of(vals, target):
    idx = jnp.full(target.shape, len(vals) - 1, I32)
    for j in reversed(range(len(vals))):
        idx = jnp.where(vals[j] == target, j, idx)
    return idx


def _softmax_rows(rows):
    m = functools.reduce(jnp.maximum, rows)
    e = [jnp.exp(r - m) for r in rows]
    s = functools.reduce(lambda a, b: a + b, e)
    return [x / s for x in e]


def _route(h, wr_ref, br_ref, eid_ref, wt_ref, rank_ref, cnt_ref, carry_ref, n_groups, n_exp):
    tm = h.shape[0]
    hh, hl = _split_bf16(h)
    wh, wl = _split_bf16(wr_ref[...])
    dg = lambda a, b: lax.dot_general(a, b, NT_DIMS, preferred_element_type=F32)
    logits = dg(wh, hh) + dg(wh, hl) + dg(wl, hh) + br_ref[...]

    g_prob = _softmax_rows([logits[g:g + 1, :] for g in range(n_groups)])
    g_top = functools.reduce(jnp.maximum, g_prob)
    g_idx = _first_index_of(g_prob, g_top)

    e_sel = []
    for e in range(n_exp):
        sel = logits[n_groups + e:n_groups + e + 1, :]
        for g in range(1, n_groups):
            r = n_groups + g * n_exp + e
            sel = jnp.where(g_idx == g, logits[r:r + 1, :], sel)
        e_sel.append(sel)
    e_prob = _softmax_rows(e_sel)
    p1 = functools.reduce(jnp.maximum, e_prob)
    i1 = _first_index_of(e_prob, p1)
    rest = [jnp.where(i1 == e, -1.0, e_prob[e]) for e in range(n_exp)]
    p2 = functools.reduce(jnp.maximum, rest)
    i2 = _first_index_of(rest, p2)
    den = p1 + p2
    wt_ref[0:1, :] = g_top * (p1 / den)
    wt_ref[1:2, :] = g_top * (p2 / den)
    eid0 = g_idx * n_exp + i1
    eid1 = g_idx * n_exp + i2
    eid_ref[0:1, :] = eid0
    eid_ref[1:2, :] = eid1

    n_e = n_groups * n_exp
    eio = lax.broadcasted_iota(I32, (n_e, tm), 0)
    oh0 = (eio == eid0).astype(F32)
    oh1 = (eio == eid1).astype(F32)
    oh = oh0 + oh1
    before = (lax.broadcasted_iota(I32, (tm, tm), 0) < lax.broadcasted_iota(I32, (tm, tm), 1)).astype(BF16)
    base = _dot(oh.astype(BF16), before) + carry_ref[...]
    rank_ref[0:1, :] = jnp.sum(oh0 * base, axis=0, keepdims=True).astype(I32)
    rank_ref[1:2, :] = jnp.sum(oh1 * base, axis=0, keepdims=True).astype(I32)
    total = carry_ref[...] + jnp.sum(oh, axis=1, keepdims=True)
    carry_ref[...] = total
    cnt_ref[...] = jnp.broadcast_to(total, cnt_ref.shape).astype(I32)


def _outln_kernel(y_ref, w_ref, x_ref, gt_ref, sh_ref, sc_ref, lng_ref, lnb_ref, wr_ref, br_ref,
                  x1_ref, h_ref, eid_ref, wt_ref, rank_ref, cnt_ref, carry_ref,
                  *, alpha, n_groups, n_exp):
    @pl.when(pl.program_id(0) == 0)
    def _():
        carry_ref[...] = jnp.zeros_like(carry_ref)

    g, s, d = x_ref.shape
    m = _dot(y_ref[...], w_ref[...]).reshape(g, s, d)
    x1 = _layer_norm(alpha * x_ref[...] + gt_ref[...][:, None, :] * m, lng_ref[...], lnb_ref[...])
    x1_ref[...] = x1
    h = (x1 * (1.0 + sc_ref[...][:, None, :]) + sh_ref[...][:, None, :]).reshape(g * s, d)
    h_ref[...] = h
    _route(h, wr_ref, br_ref, eid_ref, wt_ref, rank_ref, cnt_ref, carry_ref, n_groups, n_exp)


def _outln(y, w_bf, x, table, layer, ln_g, ln_b, wr, br, *, alpha, n_groups, n_exp, tm=256):
    t, k = y.shape
    d = w_bf.shape[1]
    gt = tm // SUBLANES_V7X
    n_e = n_groups * n_exp

    def mod(c):
        return pl.BlockSpec((None, gt, d), lambda i: (layer, i, c))

    const = lambda i: (0, 0)
    pair = lambda dt: jax.ShapeDtypeStruct((TOP_K_INNER, t), dt)
    pair_spec = pl.BlockSpec((TOP_K_INNER, tm), lambda i: (0, i))
    return pl.pallas_call(
        functools.partial(_outln_kernel, alpha=alpha, n_groups=n_groups, n_exp=n_exp),
        grid=(t // tm,),
        in_specs=[pl.BlockSpec((tm, k), lambda i: (i, 0)),
                  pl.BlockSpec((k, d), const, pipeline_mode=pl.Buffered(1)),
                  pl.BlockSpec((gt, SUBLANES_V7X, d), lambda i: (i, 0, 0)),
                  mod(2), mod(3), mod(4),
                  pl.BlockSpec((1, d), const),
                  pl.BlockSpec((1, d), const),
                  pl.BlockSpec(wr.shape, const),
                  pl.BlockSpec(br.shape, const)],
        out_specs=[pl.BlockSpec((gt, SUBLANES_V7X, d), lambda i: (i, 0, 0)),
                   pl.BlockSpec((tm, d), lambda i: (i, 0)),
                   pair_spec, pair_spec, pair_spec,
                   pl.BlockSpec((n_e, LANES_V7X), const)],
        out_shape=[jax.ShapeDtypeStruct(x.shape, F32),
                   jax.ShapeDtypeStruct((t, d), F32),
                   pair(I32), pair(F32), pair(I32),
                   jax.ShapeDtypeStruct((n_e, LANES_V7X), I32)],
        scratch_shapes=[pltpu.VMEM((n_e, 1), F32)],
        compiler_params=_params("arbitrary"),
        name="outln",
    )(y, w_bf, x, table, table, table, ln_g.reshape(1, d), ln_b.reshape(1, d), wr, br)


def _row_gather_start(src_hbm, dst, sem, idx_ref, base, n):
    def body(r, c):
        row = idx_ref[base + r]
        pltpu.make_async_copy(src_hbm.at[pl.ds(row, 1)], dst.at[pl.ds(r, 1)], sem).start()
        return c
    lax.fori_loop(0, n, body, 0, unroll=8)


def _row_gather_wait(src_hbm, dst, sem, n):
    pltpu.make_async_copy(src_hbm.at[pl.ds(0, n)], dst, sem).wait()


def _moe_kernel(pos_ref, te_ref, nxt_ref, nu_ref, h_hbm, w1_hbm, w3_hbm, w2_hbm, o_ref,
                xbuf, w1s, w3s, w2s, w1b, w3b, w2b, src_ref, wslot_ref, xsem, wsem, *, tg, t_total, w_base):
    i = pl.program_id(0)
    n_used = nu_ref[0]

    def weight_copies(e, slot):
        return [pltpu.make_async_copy(hbm.at[w_base + e], stage.at[slot], wsem.at[slot])
                for hbm, stage in ((w1_hbm, w1s), (w3_hbm, w3s), (w2_hbm, w2s))]

    @pl.when(i == 0)
    def _():
        def clear(p, c):
            src_ref[p] = 0
            return c
        lax.fori_loop(0, src_ref.shape[0], clear, 0, unroll=8)

        def fill(t, c):
            for k in range(TOP_K_INNER):
                src_ref[pos_ref[k * t_total + t]] = t
            return c
        lax.fori_loop(0, t_total, fill, 0, unroll=8)

        wslot_ref[0] = 1
        for cp in weight_copies(te_ref[0], 0):
            cp.start()
        _row_gather_start(h_hbm, xbuf.at[0], xsem.at[0], src_ref, 0, tg)

    @pl.when(i < n_used)
    def _():
        slot = i % 2

        @pl.when(i + 1 < n_used)
        def _():
            _row_gather_start(h_hbm, xbuf.at[1 - slot], xsem.at[1 - slot], src_ref, (i + 1) * tg, tg)

        e = te_ref[i]

        @pl.when((i == 0) | (e != te_ref[jnp.maximum(i - 1, 0)]))
        def _():
            ws = 1 - wslot_ref[0]
            wslot_ref[0] = ws
            for cp in weight_copies(e, ws):
                cp.wait()
            w1b[...] = w1s[ws].astype(BF16)
            w3b[...] = w3s[ws].astype(BF16)
            w2b[...] = w2s[ws].astype(BF16)
            ne = nxt_ref[e]

            @pl.when(ne >= 0)
            def _():
                for cp in weight_copies(ne, 1 - ws):
                    cp.start()

        _row_gather_wait(h_hbm, xbuf.at[slot], xsem.at[slot], tg)
        x = xbuf[slot].astype(BF16)
        a = _dot(x, w1b[...])
        b = _dot(x, w3b[...])
        o_ref[...] = _dot((_silu(a) * b).astype(BF16), w2b[...])

    @pl.when(i >= n_used)
    def _():
        o_ref[...] = jnp.zeros_like(o_ref)


def _moe(h, w1, w3, w2, w_base, pos, te, nxt, n_used, *, tg, n_tiles):
    t, d = h.shape
    f = w1.shape[-1]
    any_spec = pl.BlockSpec(memory_space=pl.ANY)
    grid_spec = pltpu.PrefetchScalarGridSpec(
        num_scalar_prefetch=4,
        grid=(n_tiles,),
        in_specs=[any_spec, any_spec, any_spec, any_spec],
        out_specs=pl.BlockSpec((tg, d), lambda i, *_: (i, 0)),
        scratch_shapes=[pltpu.VMEM((2, tg, d), F32),
                        pltpu.VMEM((2, d, f), F32), pltpu.VMEM((2, d, f), F32), pltpu.VMEM((2, f, d), F32),
                        pltpu.VMEM((d, f), BF16), pltpu.VMEM((d, f), BF16), pltpu.VMEM((f, d), BF16),
                        pltpu.SMEM((n_tiles * tg,), I32), pltpu.SMEM((1,), I32),
                        pltpu.SemaphoreType.DMA((2,)), pltpu.SemaphoreType.DMA((2,))],
    )
    return pl.pallas_call(
        functools.partial(_moe_kernel, tg=tg, t_total=t, w_base=w_base),
        grid_spec=grid_spec,
        out_shape=jax.ShapeDtypeStruct((n_tiles * tg, d), F32),
        compiler_params=_params("arbitrary"),
        name="moe",
    )(pos, te, nxt, n_used, h, w1, w3, w2)


def _comb_kernel(pos_ref, y_hbm, wt_ref, x_ref, gt_ref, lng_ref, lnb_ref, *rest, alpha, tm, t_total, has_next):
    if has_next:
        shn_ref, scn_ref, x2_ref, hn_ref, ybuf, sem = rest
    else:
        x2_ref, ybuf, sem = rest
    i = pl.program_id(0)
    n = pl.num_programs(0)

    def start(tile, slot):
        for k in range(TOP_K_INNER):
            _row_gather_start(y_hbm, ybuf.at[slot, k], sem.at[slot], pos_ref, k * t_total + tile * tm, tm)

    @pl.when(i == 0)
    def _():
        start(0, 0)

    slot = i % 2

    @pl.when(i + 1 < n)
    def _():
        start(i + 1, 1 - slot)

    for k in range(TOP_K_INNER):
        _row_gather_wait(y_hbm, ybuf.at[slot, k], sem.at[slot], tm)
    w = wt_ref[...]
    f = w[:, 0:1] * ybuf[slot, 0] + w[:, 1:2] * ybuf[slot, 1]
    g, s, d = x_ref.shape
    x2 = _layer_norm(alpha * x_ref[...] + gt_ref[...][:, None, :] * f.reshape(g, s, d), lng_ref[...], lnb_ref[...])
    x2_ref[...] = x2
    if has_next:
        hn = x2 * (1.0 + scn_ref[...][:, None, :]) + shn_ref[...][:, None, :]
        hn_ref[...] = hn.reshape(g * s, d).astype(BF16)


def _comb(pos_flat, y_sorted, wt_t, x1, table, layer, ln_g, ln_b, *, alpha, has_next, tm=256):
    g_total, s, d = x1.shape
    t = g_total * s
    gt = tm // SUBLANES_V7X

    def mod(l, c):
        return pl.BlockSpec((None, gt, d), lambda i, pos: (l, i, c))

    xspec = pl.BlockSpec((gt, s, d), lambda i, pos: (i, 0, 0))
    vec = pl.BlockSpec((1, d), lambda i, pos: (0, 0))
    in_specs = [pl.BlockSpec(memory_space=pl.ANY),
                pl.BlockSpec((tm, TOP_K_INNER), lambda i, pos: (i, 0)),
                xspec, mod(layer, 5), vec, vec]
    ins = [y_sorted, wt_t, x1, table, ln_g.reshape(1, d), ln_b.reshape(1, d)]
    out_specs = [xspec]
    out_shape = [jax.ShapeDtypeStruct(x1.shape, F32)]
    if has_next:
        in_specs += [mod(layer + 1, 0), mod(layer + 1, 1)]
        ins += [table, table]
        out_specs.append(pl.BlockSpec((tm, d), lambda i, pos: (i, 0)))
        out_shape.append(jax.ShapeDtypeStruct((t, d), BF16))
    grid_spec = pltpu.PrefetchScalarGridSpec(
        num_scalar_prefetch=1,
        grid=(t // tm,),
        in_specs=in_specs,
        out_specs=out_specs,
        scratch_shapes=[pltpu.VMEM((2, TOP_K_INNER, tm, d), F32), pltpu.SemaphoreType.DMA((2,))],
    )
    return pl.pallas_call(
        functools.partial(_comb_kernel, alpha=alpha, tm=tm, t_total=t, has_next=has_next),
        grid_spec=grid_spec,
        out_shape=out_shape,
        compiler_params=_params("arbitrary"),
        name="comb",
    )(pos_flat, *ins)


def _gla_decay(gl, wgk, bg_row, wgk_t, bg_col):
    la = _log_sigmoid(_dot(gl, wgk) + bg_row) * (1.0 / GLA_GATE_NORMALIZER)
    la_t = _log_sigmoid(lax.dot_general(wgk_t, gl, NT_DIMS, preferred_element_type=F32) + bg_col) * (
        1.0 / GLA_GATE_NORMALIZER)
    return la, la_t


def _rms_gate(o, ng, gate):
    on = o * lax.rsqrt(jnp.mean(o * o, axis=-1, keepdims=True) + RMS_EPS) * ng
    return (on * _silu(gate)).astype(BF16)


def _gla_prompt_kernel(q_ref, k_ref, v_ref, g_ref, gl_ref, wgk_ref, wgkt_ref, bgr_ref, bgc_ref, ng_ref,
                       y_ref, sout_ref, st_ref, *, n_heads, scale):
    j = pl.program_id(1)

    @pl.when(j == 0)
    def _():
        st_ref[...] = jnp.zeros_like(st_ref)

    bk = q_ref.shape[0]
    dk = q_ref.shape[1] // n_heads
    dv = v_ref.shape[1] // n_heads
    gl = gl_ref[...]
    causal = lax.broadcasted_iota(I32, (bk, bk), 0) >= lax.broadcasted_iota(I32, (bk, bk), 1)
    tri = causal.astype(BF16)
    for h in range(n_heads):
        ks, vs = slice(h * dk, (h + 1) * dk), slice(h * dv, (h + 1) * dv)
        q = q_ref[:, ks].astype(F32) * scale
        k = k_ref[:, ks].astype(F32)
        v = v_ref[:, vs]
        la, la_t = _gla_decay(gl, wgk_ref[:, ks], bgr_ref[:, ks], wgkt_ref[ks, :], bgc_ref[ks, :])
        la_hi, la_lo = _split_bf16(la)
        cum = _dot(tri, la_hi) + _dot(tri, la_lo)
        last = cum[bk - 1:bk, :]
        last_t = jnp.sum(la_t, axis=1, keepdims=True)
        qs = (q * jnp.exp(cum)).astype(BF16)
        kn = (k * jnp.exp(-cum)).astype(BF16)
        sc = lax.dot_general(qs, kn, NT_DIMS, preferred_element_type=F32)
        sc = jnp.where(causal, sc, 0.0).astype(BF16)
        s_old = st_ref[h]
        o = _dot(sc, v) + _dot(qs, s_old.astype(BF16))
        kd = (k * jnp.exp(last - cum)).astype(BF16)
        st_ref[h] = s_old * jnp.exp(last_t) + lax.dot_general(kd, v, TN_DIMS, preferred_element_type=F32)
        y_ref[:, vs] = _rms_gate(o, ng_ref[...], g_ref[:, vs].astype(F32))

    @pl.when(j == pl.num_programs(1) - 1)
    def _():
        sout_ref[...] = st_ref[...]


def _gla_sample_kernel(q_ref, k_ref, v_ref, g_ref, gl_ref, wgk_ref, wgkt_ref, bgr_ref, bgc_ref, ng_ref, sin_ref,
                       y_ref, sout_ref, *, n_heads, scale, seq):
    rows = q_ref.shape[0]
    nb = rows // seq
    dk = q_ref.shape[1] // n_heads
    dv = v_ref.shape[1] // n_heads
    gl = gl_ref[...]
    ri = lax.broadcasted_iota(I32, (rows, rows), 0)
    ci = lax.broadcasted_iota(I32, (rows, rows), 1)
    same = (ri // seq) == (ci // seq)
    causal = same & (ri >= ci)
    tri = causal.astype(BF16)
    ones_bd = same.astype(BF16)
    lane_seq = lax.broadcasted_iota(I32, (dk, rows), 1) // seq
    for h in range(n_heads):
        ks, vs = slice(h * dk, (h + 1) * dk), slice(h * dv, (h + 1) * dv)
        q = q_ref[:, ks].astype(F32) * scale
        k = k_ref[:, ks].astype(F32)
        v = v_ref[:, vs]
        la, la_t = _gla_decay(gl, wgk_ref[:, ks], bgr_ref[:, ks], wgkt_ref[ks, :], bgc_ref[ks, :])
        la_hi, la_lo = _split_bf16(la)
        cum = _dot(tri, la_hi) + _dot(tri, la_lo)
        last = _dot(ones_bd, la_hi) + _dot(ones_bd, la_lo)
        q_dec = q * jnp.exp(cum)
        qs = q_dec.astype(BF16)
        kn = (k * jnp.exp(-cum)).astype(BF16)
        sc = lax.dot_general(qs, kn, NT_DIMS, preferred_element_type=F32)
        sc = jnp.where(causal, sc, 0.0).astype(BF16)
        kd = k * jnp.exp(last - cum)
        v32 = v.astype(F32)
        o_state = []
        for s in range(nb):
            rs = slice(s * seq, (s + 1) * seq)
            s_old = sin_ref[s, h]
            o_state.append(_dot(q_dec[rs, :].astype(BF16), s_old.astype(BF16)))
            last_t = jnp.sum(jnp.where(lane_seq == s, la_t, 0.0), axis=1, keepdims=True)
            sout_ref[s, h] = s_old * jnp.exp(last_t) + lax.dot_general(
                kd[rs, :].astype(BF16), v32[rs, :].astype(BF16), TN_DIMS, preferred_element_type=F32)
        o = _dot(sc, v) + jnp.concatenate(o_state, axis=0)
        y_ref[:, vs] = _rms_gate(o, ng_ref[...], g_ref[:, vs].astype(F32))


def _gla(p, gl, wgk, wgk_t, bg_row, bg_col, ng, state_s, *, t_prompt, seq, n_heads, dk, dv, bk=128, nb=4):
    t = p.shape[0]
    bs, _, _, _ = state_s.shape
    s_len = (t - t_prompt) // bs
    bp = t_prompt // seq
    dkt, dvt = n_heads * dk, n_heads * dv
    assert dvt == 2 * dkt
    scale = dk ** -0.5
    r = gl.shape[1]
    const = lambda *_: (0, 0)
    w_specs = [pl.BlockSpec((r, dkt), const), pl.BlockSpec((dkt, r), const),
               pl.BlockSpec((1, dkt), const), pl.BlockSpec((dkt, 1), const), pl.BlockSpec((1, dv), const)]
    w_ins = [wgk, wgk_t, bg_row, bg_col, ng]

    nblk = seq // bk
    rowp = lambda b, j: b * nblk + j
    y_p, s_p = pl.pallas_call(
        functools.partial(_gla_prompt_kernel, n_heads=n_heads, scale=scale),
        grid=(bp, nblk),
        in_specs=[pl.BlockSpec((bk, dkt), lambda b, j: (rowp(b, j), 0)),
                  pl.BlockSpec((bk, dkt), lambda b, j: (rowp(b, j), 1)),
                  pl.BlockSpec((bk, dvt), lambda b, j: (rowp(b, j), 1)),
                  pl.BlockSpec((bk, dvt), lambda b, j: (rowp(b, j), 2)),
                  pl.BlockSpec((bk, r), lambda b, j: (rowp(b, j), 0))] + w_specs,
        out_specs=[pl.BlockSpec((bk, dvt), lambda b, j: (rowp(b, j), 0)),
                   pl.BlockSpec((None, n_heads, dk, dv), lambda b, j: (b, 0, 0, 0))],
        out_shape=[jax.ShapeDtypeStruct((t_prompt, dvt), BF16),
                   jax.ShapeDtypeStruct((bp, n_heads, dk, dv), F32)],
        scratch_shapes=[pltpu.VMEM((n_heads, dk, dv), F32)],
        compiler_params=_params("arbitrary", "arbitrary"),
        name="gla_prompt",
    )(p, p, p, p, gl, *w_ins)

    rows = nb * s_len
    off = t_prompt // rows
    y_s, s_s = pl.pallas_call(
        functools.partial(_gla_sample_kernel, n_heads=n_heads, scale=scale, seq=s_len),
        grid=(bs // nb,),
        in_specs=[pl.BlockSpec((rows, dkt), lambda i: (off + i, 0)),
                  pl.BlockSpec((rows, dkt), lambda i: (off + i, 1)),
                  pl.BlockSpec((rows, dvt), lambda i: (off + i, 1)),
                  pl.BlockSpec((rows, dvt), lambda i: (off + i, 2)),
                  pl.BlockSpec((rows, r), lambda i: (off + i, 0))] + w_specs + [
                  pl.BlockSpec((nb, n_heads, dk, dv), lambda i: (i, 0, 0, 0))],
        out_specs=[pl.BlockSpec((rows, dvt), lambda i: (i, 0)),
                   pl.BlockSpec((nb, n_heads, dk, dv), lambda i: (i, 0, 0, 0))],
        out_shape=[jax.ShapeDtypeStruct((t - t_prompt, dvt), BF16),
                   jax.ShapeDtypeStruct(state_s.shape, F32)],
        compiler_params=_params("arbitrary"),
        name="gla_sample",
    )(p, p, p, p, gl, *w_ins, state_s)
    return jnp.concatenate([y_p, y_s], axis=0), s_p, s_s


def _moe_schedule(eid, rank, counts, *, tg, n_tiles):
    n_e = counts.shape[0]
    e_ids = jnp.arange(n_e, dtype=I32)
    padded = ((counts + tg - 1) // tg) * tg
    ends = jnp.sum(jnp.where(e_ids[None, :] <= e_ids[:, None], padded[None, :], 0), axis=1)
    starts = ends - padded
    pos = jnp.sum(jnp.where(eid[None] == e_ids[:, None, None], starts[:, None, None], 0), axis=0) + rank
    n_used = ends[n_e - 1] // tg
    tile_start = jnp.arange(n_tiles, dtype=I32) * tg
    te = jnp.sum((ends[None, :] <= tile_start[:, None]).astype(I32), axis=1)
    te_last = jnp.sum((ends <= (n_used - 1) * tg).astype(I32))
    te = jnp.where(jnp.arange(n_tiles) < n_used, te, te_last)
    later = (e_ids[None, :] > e_ids[:, None]) & (counts[None, :] > 0)
    nxt = jnp.min(jnp.where(later, e_ids[None, :], n_e), axis=1)
    nxt = jnp.where(nxt == n_e, -1, nxt)
    return pos.reshape(-1).astype(I32), te.astype(I32), nxt.astype(I32), n_used.reshape(1).astype(I32)


def kernel(x_prompt, x_sample, cache_conv, state_gla, c_prompt, c_sample, w_mod, b_mod, ln_g, ln_b, ab_w_in, ab_conv_w, ab_v_ln_g, ab_v_ln_b, ab_w_s, ab_b_s, ab_w_out, gla_w_in, gla_w_gk, gla_b_gk, gla_norm_g, gla_w_out, moe_w_grp, moe_b_grp, moe_w_rt, moe_b_rt, moe_w1, moe_w3, moe_w2):
    bp, seq, d = x_prompt.shape
    bs, s_len, _ = x_sample.shape
    assert s_len == SUBLANES_V7X and seq % SUBLANES_V7X == 0
    depth = w_mod.shape[0]
    alpha = float((2 * depth) ** 0.25)
    t_p, t_s = bp * seq, bs * s_len
    t = t_p + t_s
    n_groups, n_exp = moe_w_rt.shape[1], moe_w_rt.shape[3]
    n_e = n_groups * n_exp
    d_ff = moe_w1.shape[-1]
    tg = 256
    n_tiles = (TOP_K_INNER * t) // tg + n_e

    x = jnp.concatenate([x_prompt.reshape(t_p, d), x_sample.reshape(t_s, d)], axis=0)
    x = x.reshape(t // SUBLANES_V7X, SUBLANES_V7X, d)
    table = _mod_table(c_prompt, c_sample, w_mod, b_mod, seq)

    w1 = moe_w1.reshape(depth * n_e, d, d_ff)
    w3 = moe_w3.reshape(depth * n_e, d, d_ff)
    w2 = moe_w2.reshape(depth * n_e, d_ff, d)

    conv_p, conv_s, chunk_v, gla_p, gla_s = [], [], [], [], []
    h_bf = None
    for layer in range(depth):
        li = layer // 2
        if layer % 2 == 0:
            n_heads, chunk = ab_w_s.shape[1], ab_w_s.shape[2]
            dc = ab_conv_w.shape[-1]
            if h_bf is None:
                p = _mm(x, ab_w_in, li, ab_w_in.shape[-1], mod=(table, layer, 0, 1))
            else:
                p = _mm(h_bf, ab_w_in, li, ab_w_in.shape[-1])
            w_s = ab_w_s[li]
            wm_p = jnp.tril(w_s)
            reps = chunk // s_len
            blk = jnp.tril(w_s[:, :s_len, :s_len])
            wm_s = jnp.einsum("ab,hts->hatbs", jnp.eye(reps, dtype=F32), blk).reshape(n_heads, chunk, chunk)
            wm = jnp.stack([wm_p, wm_s]).astype(BF16)
            b_s = ab_b_s[li]
            hd = dc // n_heads
            bias_p = jnp.repeat(b_s.T, hd, axis=1)
            bias_s = jnp.repeat(jnp.tile(b_s[:, :s_len].T, (reps, 1)), hd, axis=1)
            bias = jnp.stack([bias_p, bias_s])
            y, cp_new, cs_new, vn_s = _mix0(p, cache_conv[li], ab_conv_w[li], ab_v_ln_g[li], ab_v_ln_b[li],
                                            wm, bias, t_prompt=t_p, seq=seq, n_heads=n_heads)
            conv_p.append(cp_new)
            conv_s.append(cs_new)
            chunk_v.append(vn_s)
            w_out = ab_w_out[li].astype(BF16)
        else:
            n_heads, dk, dv = state_gla.shape[2], state_gla.shape[3], state_gla.shape[4]
            dkt, dvt = n_heads * dk, n_heads * dv
            rank = gla_w_gk.shape[1]
            n_main = 2 * dkt + 2 * dvt
            p = _mm(h_bf, gla_w_in, li, n_main)
            w_lo = jnp.pad(gla_w_in[li][:, n_main:], ((0, 0), (0, LANES_V7X - rank)))[None]
            gl = _mm(h_bf, w_lo, 0, LANES_V7X)
            wgk = jnp.pad(gla_w_gk[li], ((0, LANES_V7X - rank), (0, 0))).astype(BF16)
            y, sp_new, ss_new = _gla(p, gl, wgk, wgk.T, gla_b_gk[li].reshape(1, dkt), gla_b_gk[li].reshape(dkt, 1),
                                     gla_norm_g[li].reshape(1, dv), state_gla[li],
                                     t_prompt=t_p, seq=seq, n_heads=n_heads, dk=dk, dv=dv)
            gla_p.append(sp_new)
            gla_s.append(ss_new)
            w_out = gla_w_out[li].astype(BF16)

        wr = jnp.concatenate([moe_w_grp[layer].T,
                              jnp.transpose(moe_w_rt[layer], (0, 2, 1)).reshape(n_e, d)], axis=0)
        wr = jnp.pad(wr, ((0, LANES_V7X - wr.shape[0]), (0, 0)))
        br = jnp.concatenate([moe_b_grp[layer], moe_b_rt[layer].reshape(n_e)])
        br = jnp.pad(br, (0, LANES_V7X - br.shape[0])).reshape(LANES_V7X, 1)
        x1, h2, eid, wt, rank_, cnt = _outln(y, w_out, x, table, layer, ln_g[layer, 0], ln_b[layer, 0], wr, br,
                                             alpha=alpha, n_groups=n_groups, n_exp=n_exp)
        pos, te, nxt, n_used = _moe_schedule(eid, rank_, cnt[:, 0], tg=tg, n_tiles=n_tiles)
        ys = _moe(h2, w1, w3, w2, layer * n_e, pos, te, nxt, n_used, tg=tg, n_tiles=n_tiles)
        has_next = layer + 1 < depth
        outs = _comb(pos, ys, wt.T, x1, table, layer, ln_g[layer, 1], ln_b[layer, 1],
                     alpha=alpha, has_next=has_next)
        x = outs[0]
        h_bf = outs[1] if has_next else None

    xf = x.reshape(t, d)
    y_prompt = xf[:t_p].reshape(bp, seq, d)
    y_sample = xf[t_p:].reshape(bs, s_len, d)
    return (y_prompt, y_sample, jnp.stack(conv_p), jnp.stack(conv_s), jnp.stack(chunk_v),
            jnp.stack(gla_p), jnp.stack(gla_s))
```

```python
import functools

import jax
import jax.numpy as jnp
from jax import lax
from jax.experimental import pallas as pl
from jax.experimental.pallas import tpu as pltpu

F32 = jnp.float32
BF16 = jnp.bfloat16
I32 = jnp.int32

LN_EPS = 1e-5
RMS_EPS = 1e-6
GLA_GATE_NORMALIZER = 16.0
TOP_K_INNER = 2

SUBLANES_V7X = 8
LANES_V7X = 128
VMEM_LIMIT_V7X = 56 * 1024 * 1024

NT_DIMS = (((1,), (1,)), ((), ()))
TN_DIMS = (((0,), (0,)), ((), ()))


def _params(*sem):
    return pltpu.CompilerParams(dimension_semantics=sem, vmem_limit_bytes=VMEM_LIMIT_V7X)


def _silu(x):
    return x * (1.0 / (1.0 + jnp.exp(-x)))


def _log_sigmoid(z):
    return jnp.minimum(z, 0.0) - jnp.log(1.0 + jnp.exp(-jnp.abs(z)))


def _layer_norm(x, g, b):
    mu = jnp.mean(x, axis=-1, keepdims=True)
    xc = x - mu
    var = jnp.mean(xc * xc, axis=-1, keepdims=True)
    return xc * lax.rsqrt(var + LN_EPS) * g + b


def _dot(a, b):
    return jnp.dot(a, b, preferred_element_type=F32)


def _split_bf16(x):
    hi = x.astype(BF16)
    lo = (x - hi.astype(F32)).astype(BF16)
    return hi, lo


def _mod_kernel(cp_ref, cs_ref, w_ref, b_ref, o_ref, *, n_prompt, groups_per_seq):
    w = w_ref[...].astype(BF16)
    b = b_ref[...]
    rp = _dot(_silu(cp_ref[...]).astype(BF16), w) + b
    rs = _dot(_silu(cs_ref[...]).astype(BF16), w) + b
    tn = o_ref.shape[-1]
    for s in range(n_prompt):
        o_ref[s * groups_per_seq:(s + 1) * groups_per_seq, :] = jnp.broadcast_to(
            rp[s:s + 1, :], (groups_per_seq, tn))
    o_ref[n_prompt * groups_per_seq:, :] = rs


def _mod_table(c_prompt, c_sample, w_mod, b_mod, seq):
    depth, d, n = w_mod.shape
    bp, bs = c_prompt.shape[0], c_sample.shape[0]
    gps = seq // SUBLANES_V7X
    g_total = bp * gps + bs
    cp = jnp.pad(c_prompt, ((0, (-bp) % SUBLANES_V7X), (0, 0)))
    tn = 1024
    return pl.pallas_call(
        functools.partial(_mod_kernel, n_prompt=bp, groups_per_seq=gps),
        grid=(depth, n // tn),
        in_specs=[
            pl.BlockSpec(cp.shape, lambda l, j: (0, 0)),
            pl.BlockSpec(c_sample.shape, lambda l, j: (0, 0)),
            pl.BlockSpec((None, d, tn), lambda l, j: (l, 0, j)),
            pl.BlockSpec((None, 1, tn), lambda l, j: (l, 0, j)),
        ],
        out_specs=pl.BlockSpec((None, g_total, tn), lambda l, j: (l, 0, j)),
        out_shape=jax.ShapeDtypeStruct((depth, g_total, n), F32),
        compiler_params=_params("arbitrary", "arbitrary"),
        name="mod",
    )(cp, c_sample, w_mod, b_mod.reshape(depth, 1, n))


def _pick_tile(i, n_prompt_tiles, p_ref, s_ref):
    return jnp.where(i < n_prompt_tiles, p_ref[...], s_ref[...])


def _mm_kernel(*refs, has_mod, n_prompt_tiles):
    if has_mod:
        xp_ref, xs_ref, sh_ref, sc_ref, w_ref, o_ref, wb_ref = refs
    else:
        a_ref, w_ref, o_ref, wb_ref = refs

    @pl.when(pl.program_id(1) == 0)
    def _():
        wb_ref[...] = w_ref[...].astype(BF16)

    if has_mod:
        x = _pick_tile(pl.program_id(1), n_prompt_tiles, xp_ref, xs_ref)
        g, s, k = x.shape
        h = x * (1.0 + sc_ref[...][:, None, :]) + sh_ref[...][:, None, :]
        a = h.reshape(g * s, k).astype(BF16)
    else:
        a = a_ref[...]
    o_ref[...] = _dot(a, wb_ref[...]).astype(o_ref.dtype)


def _mm(a, w3, w_idx, n_out, *, mod=None, tm=512, tn=1024, out_dtype=BF16):
    k = w3.shape[1]
    tn = min(tn, n_out)
    n_p = 0
    if mod is None:
        t = a.shape[0]
        a_specs = [pl.BlockSpec((tm, k), lambda j, i: (i, 0))]
        ins = [a]
    else:
        table, layer, sh_col, sc_col = mod
        xp, xs = a
        gt = tm // SUBLANES_V7X
        n_p = xp.shape[0] // gt
        t = (xp.shape[0] + xs.shape[0]) * SUBLANES_V7X
        a_specs = [
            pl.BlockSpec((gt, SUBLANES_V7X, k), lambda j, i: (jnp.minimum(i, n_p - 1), 0, 0)),
            pl.BlockSpec((gt, SUBLANES_V7X, k), lambda j, i: (jnp.maximum(i - n_p, 0), 0, 0)),
            pl.BlockSpec((None, gt, k), lambda j, i: (layer, i, sh_col)),
            pl.BlockSpec((None, gt, k), lambda j, i: (layer, i, sc_col)),
        ]
        ins = [xp, xs, table, table]
    return pl.pallas_call(
        functools.partial(_mm_kernel, has_mod=mod is not None, n_prompt_tiles=n_p),
        grid=(n_out // tn, t // tm),
        in_specs=a_specs + [pl.BlockSpec((None, k, tn), lambda j, i: (w_idx, 0, j))],
        out_specs=pl.BlockSpec((tm, tn), lambda j, i: (i, j)),
        out_shape=jax.ShapeDtypeStruct((t, n_out), out_dtype),
        scratch_shapes=[pltpu.VMEM((k, tn), BF16)],
        compiler_params=_params("arbitrary", "arbitrary"),
        name="mm",
    )(*ins, w3)


def _mix0_kernel(bg_ref, cg_ref, hx_ref, u_ref, v_ref, cache_ref, cw_ref, vg_ref, vb_ref, wm_ref, bias_ref,
                 y_ref, convp_ref, convs_ref, vns_ref, zprev_ref, *, n_prompt_tiles, tiles_per_seq, n_heads):
    i = pl.program_id(0)
    tm, dc = bg_ref.shape
    ns = tm // SUBLANES_V7X
    z = cg_ref[...].astype(F32) * hx_ref[...].astype(F32)
    row = lax.broadcasted_iota(I32, (tm, dc), 0)
    r1 = pltpu.roll(z, 1, 0)
    r2 = pltpu.roll(z, 2, 0)
    cw = cw_ref[...]
    bg = bg_ref[...].astype(F32)

    vn = _layer_norm(v_ref[...].astype(F32), vg_ref[...], vb_ref[...])
    vnb = vn.astype(BF16)
    hd = dc // n_heads
    mixed = jnp.concatenate(
        [_dot(wm_ref[h], vnb[:, h * hd:(h + 1) * hd]) for h in range(n_heads)], axis=-1) + bias_ref[...]
    y_ref[:, dc:] = (u_ref[...].astype(F32) * mixed).astype(BF16)

    def conv_out(zm1, zm2):
        conv = cw[0:1, :] * zm2 + cw[1:2, :] * zm1 + cw[2:3, :] * z
        y_ref[:, :dc] = (bg * conv).astype(BF16)

    @pl.when(i < n_prompt_tiles)
    def _prompt():
        @pl.when(i % tiles_per_seq == 0)
        def _():
            zprev_ref[...] = jnp.zeros_like(zprev_ref)

        zp = zprev_ref[...]
        p1 = zp[SUBLANES_V7X - 1:SUBLANES_V7X, :]
        p2 = zp[SUBLANES_V7X - 2:SUBLANES_V7X - 1, :]
        conv_out(jnp.where(row == 0, p1, r1),
                 jnp.where(row == 0, p2, jnp.where(row == 1, p1, r2)))
        zprev_ref[...] = z[tm - SUBLANES_V7X:, :]
        convp_ref[...] = z[tm - 2:, :].reshape(1, 2, dc)

    @pl.when(i >= n_prompt_tiles)
    def _sample():
        c = cache_ref[...]
        c0 = jnp.broadcast_to(c[:, 0:1, :], (ns, SUBLANES_V7X, dc)).reshape(tm, dc)
        c1 = jnp.broadcast_to(c[:, 1:2, :], (ns, SUBLANES_V7X, dc)).reshape(tm, dc)
        rr = row % SUBLANES_V7X
        conv_out(jnp.where(rr == 0, c1, r1),
                 jnp.where(rr == 0, c0, jnp.where(rr == 1, c1, r2)))
        z3 = z.reshape(ns, SUBLANES_V7X, dc)
        convs_ref[...] = z3[:, SUBLANES_V7X - 2:, :]
        vns_ref[...] = vn.reshape(ns, SUBLANES_V7X, dc)


def _mix0(p, cache, conv_w, v_g, v_b, wm, bias, *, t_prompt, seq, n_heads):
    t, n = p.shape
    bs, cwm1, dc = cache.shape
    tm = wm.shape[-1]
    assert cwm1 == 2 and conv_w.shape[0] == 3 and n == 5 * dc and seq % tm == 0
    n_p = t_prompt // tm
    n_s = (t - t_prompt) // tm
    tps = seq // tm
    bp = t_prompt // seq
    ns = tm // SUBLANES_V7X

    def col(c):
        return pl.BlockSpec((tm, dc), lambda i: (i, c))

    def s_idx(i):
        return jnp.maximum(i - n_p, 0)

    const2 = lambda i: (0, 0)
    mode = lambda i: ((i >= n_p).astype(I32), 0, 0, 0)
    return pl.pallas_call(
        functools.partial(_mix0_kernel, n_prompt_tiles=n_p, tiles_per_seq=tps, n_heads=n_heads),
        grid=(n_p + n_s,),
        in_specs=[col(0), col(1), col(2), col(3), col(4),
                  pl.BlockSpec((ns, 2, dc), lambda i: (s_idx(i), 0, 0)),
                  pl.BlockSpec((3, dc), const2),
                  pl.BlockSpec((1, dc), const2),
                  pl.BlockSpec((1, dc), const2),
                  pl.BlockSpec((None, n_heads, tm, tm), mode),
                  pl.BlockSpec((None, tm, dc), lambda i: ((i >= n_p).astype(I32), 0, 0))],
        out_specs=[pl.BlockSpec((tm, 2 * dc), lambda i: (i, 0)),
                   pl.BlockSpec((1, 2, dc), lambda i: (jnp.minimum(i // tps, bp - 1), 0, 0)),
                   pl.BlockSpec((ns, 2, dc), lambda i: (s_idx(i), 0, 0)),
                   pl.BlockSpec((ns, SUBLANES_V7X, dc), lambda i: (s_idx(i), 0, 0))],
        out_shape=[jax.ShapeDtypeStruct((t, 2 * dc), BF16),
                   jax.ShapeDtypeStruct((bp, 2, dc), F32),
                   jax.ShapeDtypeStruct((bs, 2, dc), F32),
                   jax.ShapeDtypeStruct((bs, SUBLANES_V7X, dc), F32)],
        scratch_shapes=[pltpu.VMEM((SUBLANES_V7X, dc), F32)],
        compiler_params=_params("arbitrary"),
        name="mix0",
    )(p, p, p, p, p, cache, conv_w, v_g.reshape(1, dc), v_b.reshape(1, dc), wm, bias)


def _first_index_of(vals, target):
    idx = jnp.full(target.shape, len(vals) - 1, I32)
    for j in reversed(range(len(vals))):
        idx = jnp.where(vals[j] == target, j, idx)
    return idx


def _softmax_rows(rows):
    m = functools.reduce(jnp.maximum, rows)
    e = [jnp.exp(r - m) for r in rows]
    s = functools.reduce(lambda a, b: a + b, e)
    return [x / s for x in e]


def _route(h, wr_ref, br_ref, eid_ref, wt_ref, rank_ref, cnt_ref, carry_ref, n_groups, n_exp):
    tm = h.shape[0]
    hh, hl = _split_bf16(h)
    wh, wl = _split_bf16(wr_ref[...])
    dg = lambda a, b: lax.dot_general(a, b, NT_DIMS, preferred_element_type=F32)
    logits = dg(wh, hh) + dg(wh, hl) + dg(wl, hh) + br_ref[...]

    g_prob = _softmax_rows([logits[g:g + 1, :] for g in range(n_groups)])
    g_top = functools.reduce(jnp.maximum, g_prob)
    g_idx = _first_index_of(g_prob, g_top)

    e_sel = []
    for e in range(n_exp):
        sel = logits[n_groups + e:n_groups + e + 1, :]
        for g in range(1, n_groups):
            r = n_groups + g * n_exp + e
            sel = jnp.where(g_idx == g, logits[r:r + 1, :], sel)
        e_sel.append(sel)
    e_prob = _softmax_rows(e_sel)
    p1 = functools.reduce(jnp.maximum, e_prob)
    i1 = _first_index_of(e_prob, p1)
    rest = [jnp.where(i1 == e, -1.0, e_prob[e]) for e in range(n_exp)]
    p2 = functools.reduce(jnp.maximum, rest)
    i2 = _first_index_of(rest, p2)
    den = p1 + p2
    wt_ref[0:1, :] = g_top * (p1 / den)
    wt_ref[1:2, :] = g_top * (p2 / den)
    eid0 = g_idx * n_exp + i1
    eid1 = g_idx * n_exp + i2
    eid_ref[0:1, :] = eid0
    eid_ref[1:2, :] = eid1

    n_e = n_groups * n_exp
    eio = lax.broadcasted_iota(I32, (n_e, tm), 0)
    oh0 = (eio == eid0).astype(F32)
    oh1 = (eio == eid1).astype(F32)
    oh = oh0 + oh1
    before = (lax.broadcasted_iota(I32, (tm, tm), 0) < lax.broadcasted_iota(I32, (tm, tm), 1)).astype(BF16)
    base = _dot(oh.astype(BF16), before) + carry_ref[...]
    rank_ref[0:1, :] = jnp.sum(oh0 * base, axis=0, keepdims=True).astype(I32)
    rank_ref[1:2, :] = jnp.sum(oh1 * base, axis=0, keepdims=True).astype(I32)
    total = carry_ref[...] + jnp.sum(oh, axis=1, keepdims=True)
    carry_ref[...] = total
    cnt_ref[...] = jnp.broadcast_to(total, cnt_ref.shape).astype(I32)


def _outln_kernel(*refs, alpha, n_groups, n_exp, y_split, x_split, n_prompt_tiles):
    refs = list(refs)
    i = pl.program_id(0)
    take = lambda split: [refs.pop(0) for _ in range(2 if split else 1)]
    y_refs, (w_ref,), x_refs = take(y_split), take(False), take(x_split)
    (gt_ref, sh_ref, sc_ref, lng_ref, lnb_ref, wr_ref, br_ref,
     x1_ref, h_ref, eid_ref, wt_ref, rank_ref, cnt_ref, carry_ref) = refs

    @pl.when(i == 0)
    def _():
        carry_ref[...] = jnp.zeros_like(carry_ref)

    y = _pick_tile(i, n_prompt_tiles, *y_refs) if y_split else y_refs[0][...]
    x = _pick_tile(i, n_prompt_tiles, *x_refs) if x_split else x_refs[0][...]
    g, s, d = x.shape
    m = _dot(y, w_ref[...]).reshape(g, s, d)
    x1 = _layer_norm(alpha * x + gt_ref[...][:, None, :] * m, lng_ref[...], lnb_ref[...])
    x1_ref[...] = x1
    h = (x1 * (1.0 + sc_ref[...][:, None, :]) + sh_ref[...][:, None, :]).reshape(g * s, d)
    h_ref[...] = h
    _route(h, wr_ref, br_ref, eid_ref, wt_ref, rank_ref, cnt_ref, carry_ref, n_groups, n_exp)


def _outln(y, w_bf, x, table, layer, ln_g, ln_b, wr, br, *, alpha, n_groups, n_exp, tm=512):
    y_split, x_split = isinstance(y, tuple), isinstance(x, tuple)
    ys, xs = (y if y_split else (y,)), (x if x_split else (x,))
    k, d = w_bf.shape
    gt = tm // SUBLANES_V7X
    n_e = n_groups * n_exp
    t = sum(a.shape[0] for a in ys)
    n_p = (ys[0].shape[0] // tm) if y_split else (xs[0].shape[0] // gt if x_split else 0)

    def split_specs(block, n_arrays):
        if n_arrays == 1:
            return [pl.BlockSpec(block, lambda i: (i,) + (0,) * (len(block) - 1))]
        return [pl.BlockSpec(block, lambda i: (jnp.minimum(i, n_p - 1),) + (0,) * (len(block) - 1)),
                pl.BlockSpec(block, lambda i: (jnp.maximum(i - n_p, 0),) + (0,) * (len(block) - 1))]

    def mod(c):
        return pl.BlockSpec((None, gt, d), lambda i: (layer, i, c))

    const = lambda i: (0, 0)
    pair = lambda dt: jax.ShapeDtypeStruct((TOP_K_INNER, t), dt)
    pair_spec = pl.BlockSpec((TOP_K_INNER, tm), lambda i: (0, i))
    return pl.pallas_call(
        functools.partial(_outln_kernel, alpha=alpha, n_groups=n_groups, n_exp=n_exp,
                          y_split=y_split, x_split=x_split, n_prompt_tiles=n_p),
        grid=(t // tm,),
        in_specs=split_specs((tm, k), len(ys))
                 + [pl.BlockSpec((k, d), const, pipeline_mode=pl.Buffered(1))]
                 + split_specs((gt, SUBLANES_V7X, d), len(xs))
                 + [mod(2), mod(3), mod(4),
                  pl.BlockSpec((1, d), const),
                  pl.BlockSpec((1, d), const),
                  pl.BlockSpec(wr.shape, const),
                  pl.BlockSpec(br.shape, const)],
        out_specs=[pl.BlockSpec((gt, SUBLANES_V7X, d), lambda i: (i, 0, 0)),
                   pl.BlockSpec((tm, d), lambda i: (i, 0)),
                   pair_spec, pair_spec, pair_spec,
                   pl.BlockSpec((n_e, LANES_V7X), const)],
        out_shape=[jax.ShapeDtypeStruct((t // SUBLANES_V7X, SUBLANES_V7X, d), F32),
                   jax.ShapeDtypeStruct((t, d), F32),
                   pair(I32), pair(F32), pair(I32),
                   jax.ShapeDtypeStruct((n_e, LANES_V7X), I32)],
        scratch_shapes=[pltpu.VMEM((n_e, 1), F32)],
        compiler_params=_params("arbitrary"),
        name="outln",
    )(*ys, w_bf, *xs, table, table, table, ln_g.reshape(1, d), ln_b.reshape(1, d), wr, br)


def _row_gather_start(src_hbm, dst, sem, idx_ref, base, n):
    def body(r, c):
        row = idx_ref[base + r]
        pltpu.make_async_copy(src_hbm.at[pl.ds(row, 1)], dst.at[pl.ds(r, 1)], sem).start()
        return c
    lax.fori_loop(0, n, body, 0, unroll=8)


def _row_gather_wait(src_hbm, dst, sem, n):
    pltpu.make_async_copy(src_hbm.at[pl.ds(0, n)], dst, sem).wait()


def _moe_kernel(pos_ref, te_ref, nxt_ref, nu_ref, h_hbm, w1_hbm, w3_hbm, w2_hbm, o_ref,
                xbuf, w1s, w3s, w2s, w1b, w3b, w2b, src_ref, wslot_ref, xsem, wsem, *, tg, t_total, w_base):
    i = pl.program_id(0)
    n_used = nu_ref[0]

    def weight_copies(e, slot):
        return [pltpu.make_async_copy(hbm.at[w_base + e], stage.at[slot], wsem.at[slot])
                for hbm, stage in ((w1_hbm, w1s), (w3_hbm, w3s), (w2_hbm, w2s))]

    @pl.when(i == 0)
    def _():
        def clear(p, c):
            src_ref[p] = 0
            return c
        lax.fori_loop(0, src_ref.shape[0], clear, 0, unroll=8)

        def fill(t, c):
            for k in range(TOP_K_INNER):
                src_ref[pos_ref[k * t_total + t]] = t
            return c
        lax.fori_loop(0, t_total, fill, 0, unroll=8)

        wslot_ref[0] = 1
        for cp in weight_copies(te_ref[0], 0):
            cp.start()
        _row_gather_start(h_hbm, xbuf.at[0], xsem.at[0], src_ref, 0, tg)

    @pl.when(i < n_used)
    def _():
        slot = i % 2

        @pl.when(i + 1 < n_used)
        def _():
            _row_gather_start(h_hbm, xbuf.at[1 - slot], xsem.at[1 - slot], src_ref, (i + 1) * tg, tg)

        e = te_ref[i]

        @pl.when((i == 0) | (e != te_ref[jnp.maximum(i - 1, 0)]))
        def _():
            ws = 1 - wslot_ref[0]
            wslot_ref[0] = ws
            for cp in weight_copies(e, ws):
                cp.wait()
            w1b[...] = w1s[ws].astype(BF16)
            w3b[...] = w3s[ws].astype(BF16)
            w2b[...] = w2s[ws].astype(BF16)
            ne = nxt_ref[e]

            @pl.when(ne >= 0)
            def _():
                for cp in weight_copies(ne, 1 - ws):
                    cp.start()

        _row_gather_wait(h_hbm, xbuf.at[slot], xsem.at[slot], tg)
        x = xbuf[slot].astype(BF16)
        a = _dot(x, w1b[...])
        b = _dot(x, w3b[...])
        o_ref[...] = _dot((_silu(a) * b).astype(BF16), w2b[...])

    @pl.when(i >= n_used)
    def _():
        o_ref[...] = jnp.zeros_like(o_ref)


def _moe(h, w1, w3, w2, w_base, pos, te, nxt, n_used, *, tg, n_tiles):
    t, d = h.shape
    f = w1.shape[-1]
    any_spec = pl.BlockSpec(memory_space=pl.ANY)
    grid_spec = pltpu.PrefetchScalarGridSpec(
        num_scalar_prefetch=4,
        grid=(n_tiles,),
        in_specs=[any_spec, any_spec, any_spec, any_spec],
        out_specs=pl.BlockSpec((tg, d), lambda i, *_: (i, 0)),
        scratch_shapes=[pltpu.VMEM((2, tg, d), F32),
                        pltpu.VMEM((2, d, f), F32), pltpu.VMEM((2, d, f), F32), pltpu.VMEM((2, f, d), F32),
                        pltpu.VMEM((d, f), BF16), pltpu.VMEM((d, f), BF16), pltpu.VMEM((f, d), BF16),
                        pltpu.SMEM((n_tiles * tg,), I32), pltpu.SMEM((1,), I32),
                        pltpu.SemaphoreType.DMA((2,)), pltpu.SemaphoreType.DMA((2,))],
    )
    return pl.pallas_call(
        functools.partial(_moe_kernel, tg=tg, t_total=t, w_base=w_base),
        grid_spec=grid_spec,
        out_shape=jax.ShapeDtypeStruct((n_tiles * tg, d), F32),
        compiler_params=_params("arbitrary"),
        name="moe",
    )(pos, te, nxt, n_used, h, w1, w3, w2)


def _comb_kernel(pos_ref, y_hbm, wt_ref, x_ref, gt_ref, lng_ref, lnb_ref, *rest,
                 alpha, tm, t_total, has_next, n_prompt_tiles):
    if has_next:
        shn_ref, scn_ref, x2_ref, hn_ref, ybuf, sem = rest
    else:
        x2p_ref, x2s_ref, ybuf, sem = rest
    i = pl.program_id(0)
    n = pl.num_programs(0)

    def start(tile, slot):
        for k in range(TOP_K_INNER):
            _row_gather_start(y_hbm, ybuf.at[slot, k], sem.at[slot], pos_ref, k * t_total + tile * tm, tm)

    @pl.when(i == 0)
    def _():
        start(0, 0)

    slot = i % 2

    @pl.when(i + 1 < n)
    def _():
        start(i + 1, 1 - slot)

    for k in range(TOP_K_INNER):
        _row_gather_wait(y_hbm, ybuf.at[slot, k], sem.at[slot], tm)
    w = wt_ref[...]
    f = w[:, 0:1] * ybuf[slot, 0] + w[:, 1:2] * ybuf[slot, 1]
    g, s, d = x_ref.shape
    x2 = _layer_norm(alpha * x_ref[...] + gt_ref[...][:, None, :] * f.reshape(g, s, d), lng_ref[...], lnb_ref[...])
    if has_next:
        x2_ref[...] = x2
        hn = x2 * (1.0 + scn_ref[...][:, None, :]) + shn_ref[...][:, None, :]
        hn_ref[...] = hn.reshape(g * s, d).astype(BF16)
    else:
        @pl.when(i < n_prompt_tiles)
        def _():
            x2p_ref[...] = x2

        @pl.when(i >= n_prompt_tiles)
        def _():
            x2s_ref[...] = x2


def _comb(pos_flat, y_sorted, wt_t, x1, table, layer, ln_g, ln_b, *, alpha, has_next, t_prompt, tm=256):
    g_total, s, d = x1.shape
    t = g_total * s
    gt = tm // SUBLANES_V7X
    n_p = t_prompt // tm

    def mod(l, c):
        return pl.BlockSpec((None, gt, d), lambda i, pos: (l, i, c))

    xspec = pl.BlockSpec((gt, s, d), lambda i, pos: (i, 0, 0))
    vec = pl.BlockSpec((1, d), lambda i, pos: (0, 0))
    in_specs = [pl.BlockSpec(memory_space=pl.ANY),
                pl.BlockSpec((tm, TOP_K_INNER), lambda i, pos: (i, 0)),
                xspec, mod(layer, 5), vec, vec]
    ins = [y_sorted, wt_t, x1, table, ln_g.reshape(1, d), ln_b.reshape(1, d)]
    if has_next:
        in_specs += [mod(layer + 1, 0), mod(layer + 1, 1)]
        ins += [table, table]
        out_specs = [xspec, pl.BlockSpec((tm, d), lambda i, pos: (i, 0))]
        out_shape = [jax.ShapeDtypeStruct(x1.shape, F32), jax.ShapeDtypeStruct((t, d), BF16)]
    else:
        out_specs = [pl.BlockSpec((gt, s, d), lambda i, pos: (jnp.minimum(i, n_p - 1), 0, 0)),
                     pl.BlockSpec((gt, s, d), lambda i, pos: (jnp.maximum(i - n_p, 0), 0, 0))]
        out_shape = [jax.ShapeDtypeStruct((t_prompt // s, s, d), F32),
                     jax.ShapeDtypeStruct(((t - t_prompt) // s, s, d), F32)]
    grid_spec = pltpu.PrefetchScalarGridSpec(
        num_scalar_prefetch=1,
        grid=(t // tm,),
        in_specs=in_specs,
        out_specs=out_specs,
        scratch_shapes=[pltpu.VMEM((2, TOP_K_INNER, tm, d), F32), pltpu.SemaphoreType.DMA((2,))],
    )
    return pl.pallas_call(
        functools.partial(_comb_kernel, alpha=alpha, tm=tm, t_total=t, has_next=has_next, n_prompt_tiles=n_p),
        grid_spec=grid_spec,
        out_shape=out_shape,
        compiler_params=_params("arbitrary"),
        name="comb",
    )(pos_flat, *ins)


def _gla_decay(gl, wgk, bg_row, wgk_t, bg_col):
    la = _log_sigmoid(_dot(gl, wgk) + bg_row) * (1.0 / GLA_GATE_NORMALIZER)
    la_t = _log_sigmoid(lax.dot_general(wgk_t, gl, NT_DIMS, preferred_element_type=F32) + bg_col) * (
        1.0 / GLA_GATE_NORMALIZER)
    return la, la_t


def _rms_gate(o, ng, gate):
    on = o * lax.rsqrt(jnp.mean(o * o, axis=-1, keepdims=True) + RMS_EPS) * ng
    return (on * _silu(gate)).astype(BF16)


def _gla_prompt_kernel(q_ref, k_ref, v_ref, g_ref, gl_ref, wgk_ref, wgkt_ref, bgr_ref, bgc_ref, ng_ref,
                       y_ref, sout_ref, st_ref, *, n_heads, scale):
    j = pl.program_id(1)

    @pl.when(j == 0)
    def _():
        st_ref[...] = jnp.zeros_like(st_ref)

    bk = q_ref.shape[0]
    dk = q_ref.shape[1] // n_heads
    dv = v_ref.shape[1] // n_heads
    gl = gl_ref[...]
    causal = lax.broadcasted_iota(I32, (bk, bk), 0) >= lax.broadcasted_iota(I32, (bk, bk), 1)
    tri = causal.astype(BF16)
    for h in range(n_heads):
        ks, vs = slice(h * dk, (h + 1) * dk), slice(h * dv, (h + 1) * dv)
        q = q_ref[:, ks].astype(F32) * scale
        k = k_ref[:, ks].astype(F32)
        v = v_ref[:, vs]
        la, la_t = _gla_decay(gl, wgk_ref[:, ks], bgr_ref[:, ks], wgkt_ref[ks, :], bgc_ref[ks, :])
        la_hi, la_lo = _split_bf16(la)
        cum = _dot(tri, la_hi) + _dot(tri, la_lo)
        last = cum[bk - 1:bk, :]
        last_t = jnp.sum(la_t, axis=1, keepdims=True)
        qs = (q * jnp.exp(cum)).astype(BF16)
        kn = (k * jnp.exp(-cum)).astype(BF16)
        sc = lax.dot_general(qs, kn, NT_DIMS, preferred_element_type=F32)
        sc = jnp.where(causal, sc, 0.0).astype(BF16)
        s_old = st_ref[h]
        o = _dot(sc, v) + _dot(qs, s_old.astype(BF16))
        kd = (k * jnp.exp(last - cum)).astype(BF16)
        st_ref[h] = s_old * jnp.exp(last_t) + lax.dot_general(kd, v, TN_DIMS, preferred_element_type=F32)
        y_ref[:, vs] = _rms_gate(o, ng_ref[...], g_ref[:, vs].astype(F32))

    @pl.when(j == pl.num_programs(1) - 1)
    def _():
        sout_ref[...] = st_ref[...]


def _gla_sample_kernel(q_ref, k_ref, v_ref, g_ref, gl_ref, wgk_ref, wgkt_ref, bgr_ref, bgc_ref, ng_ref, sin_ref,
                       y_ref, sout_ref, *, n_heads, scale, seq):
    rows = q_ref.shape[0]
    nb = rows // seq
    dk = q_ref.shape[1] // n_heads
    dv = v_ref.shape[1] // n_heads
    gl = gl_ref[...]
    ri = lax.broadcasted_iota(I32, (rows, rows), 0)
    ci = lax.broadcasted_iota(I32, (rows, rows), 1)
    same = (ri // seq) == (ci // seq)
    causal = same & (ri >= ci)
    tri = causal.astype(BF16)
    ones_bd = same.astype(BF16)
    lane_seq = lax.broadcasted_iota(I32, (dk, rows), 1) // seq
    for h in range(n_heads):
        ks, vs = slice(h * dk, (h + 1) * dk), slice(h * dv, (h + 1) * dv)
        q = q_ref[:, ks].astype(F32) * scale
        k = k_ref[:, ks].astype(F32)
        v = v_ref[:, vs]
        la, la_t = _gla_decay(gl, wgk_ref[:, ks], bgr_ref[:, ks], wgkt_ref[ks, :], bgc_ref[ks, :])
        la_hi, la_lo = _split_bf16(la)
        cum = _dot(tri, la_hi) + _dot(tri, la_lo)
        last = _dot(ones_bd, la_hi) + _dot(ones_bd, la_lo)
        q_dec = q * jnp.exp(cum)
        qs = q_dec.astype(BF16)
        kn = (k * jnp.exp(-cum)).astype(BF16)
        sc = lax.dot_general(qs, kn, NT_DIMS, preferred_element_type=F32)
        sc = jnp.where(causal, sc, 0.0).astype(BF16)
        kd = k * jnp.exp(last - cum)
        v32 = v.astype(F32)
        o_state = []
        for s in range(nb):
            rs = slice(s * seq, (s + 1) * seq)
            s_old = sin_ref[s, h]
            o_state.append(_dot(q_dec[rs, :].astype(BF16), s_old.astype(BF16)))
            last_t = jnp.sum(jnp.where(lane_seq == s, la_t, 0.0), axis=1, keepdims=True)
            sout_ref[s, h] = s_old * jnp.exp(last_t) + lax.dot_general(
                kd[rs, :].astype(BF16), v32[rs, :].astype(BF16), TN_DIMS, preferred_element_type=F32)
        o = _dot(sc, v) + jnp.concatenate(o_state, axis=0)
        y_ref[:, vs] = _rms_gate(o, ng_ref[...], g_ref[:, vs].astype(F32))


def _gla(p, gl, wgk, wgk_t, bg_row, bg_col, ng, state_s, *, t_prompt, seq, n_heads, dk, dv, bk=128, nb=4):
    t = p.shape[0]
    bs, _, _, _ = state_s.shape
    s_len = (t - t_prompt) // bs
    bp = t_prompt // seq
    dkt, dvt = n_heads * dk, n_heads * dv
    assert dvt == 2 * dkt
    scale = dk ** -0.5
    r = gl.shape[1]
    const = lambda *_: (0, 0)
    w_specs = [pl.BlockSpec((r, dkt), const), pl.BlockSpec((dkt, r), const),
               pl.BlockSpec((1, dkt), const), pl.BlockSpec((dkt, 1), const), pl.BlockSpec((1, dv), const)]
    w_ins = [wgk, wgk_t, bg_row, bg_col, ng]

    nblk = seq // bk
    rowp = lambda b, j: b * nblk + j
    y_p, s_p = pl.pallas_call(
        functools.partial(_gla_prompt_kernel, n_heads=n_heads, scale=scale),
        grid=(bp, nblk),
        in_specs=[pl.BlockSpec((bk, dkt), lambda b, j: (rowp(b, j), 0)),
                  pl.BlockSpec((bk, dkt), lambda b, j: (rowp(b, j), 1)),
                  pl.BlockSpec((bk, dvt), lambda b, j: (rowp(b, j), 1)),
                  pl.BlockSpec((bk, dvt), lambda b, j: (rowp(b, j), 2)),
                  pl.BlockSpec((bk, r), lambda b, j: (rowp(b, j), 0))] + w_specs,
        out_specs=[pl.BlockSpec((bk, dvt), lambda b, j: (rowp(b, j), 0)),
                   pl.BlockSpec((None, n_heads, dk, dv), lambda b, j: (b, 0, 0, 0))],
        out_shape=[jax.ShapeDtypeStruct((t_prompt, dvt), BF16),
                   jax.ShapeDtypeStruct((bp, n_heads, dk, dv), F32)],
        scratch_shapes=[pltpu.VMEM((n_heads, dk, dv), F32)],
        compiler_params=_params("arbitrary", "arbitrary"),
        name="gla_prompt",
    )(p, p, p, p, gl, *w_ins)

    rows = nb * s_len
    off = t_prompt // rows
    y_s, s_s = pl.pallas_call(
        functools.partial(_gla_sample_kernel, n_heads=n_heads, scale=scale, seq=s_len),
        grid=(bs // nb,),
        in_specs=[pl.BlockSpec((rows, dkt), lambda i: (off + i, 0)),
                  pl.BlockSpec((rows, dkt), lambda i: (off + i, 1)),
                  pl.BlockSpec((rows, dvt), lambda i: (off + i, 1)),
                  pl.BlockSpec((rows, dvt), lambda i: (off + i, 2)),
                  pl.BlockSpec((rows, r), lambda i: (off + i, 0))] + w_specs + [
                  pl.BlockSpec((nb, n_heads, dk, dv), lambda i: (i, 0, 0, 0))],
        out_specs=[pl.BlockSpec((rows, dvt), lambda i: (i, 0)),
                   pl.BlockSpec((nb, n_heads, dk, dv), lambda i: (i, 0, 0, 0))],
        out_shape=[jax.ShapeDtypeStruct((t - t_prompt, dvt), BF16),
                   jax.ShapeDtypeStruct(state_s.shape, F32)],
        compiler_params=_params("arbitrary"),
        name="gla_sample",
    )(p, p, p, p, gl, *w_ins, state_s)
    return (y_p, y_s), s_p, s_s


def _moe_schedule(eid, rank, counts, *, tg, n_tiles):
    n_e = counts.shape[0]
    e_ids = jnp.arange(n_e, dtype=I32)
    padded = ((counts + tg - 1) // tg) * tg
    ends = jnp.sum(jnp.where(e_ids[None, :] <= e_ids[:, None], padded[None, :], 0), axis=1)
    starts = ends - padded
    pos = jnp.sum(jnp.where(eid[None] == e_ids[:, None, None], starts[:, None, None], 0), axis=0) + rank
    n_used = ends[n_e - 1] // tg
    tile_start = jnp.arange(n_tiles, dtype=I32) * tg
    te = jnp.sum((ends[None, :] <= tile_start[:, None]).astype(I32), axis=1)
    te_last = jnp.sum((ends <= (n_used - 1) * tg).astype(I32))
    te = jnp.where(jnp.arange(n_tiles) < n_used, te, te_last)
    later = (e_ids[None, :] > e_ids[:, None]) & (counts[None, :] > 0)
    nxt = jnp.min(jnp.where(later, e_ids[None, :], n_e), axis=1)
    nxt = jnp.where(nxt == n_e, -1, nxt)
    return pos.reshape(-1).astype(I32), te.astype(I32), nxt.astype(I32), n_used.reshape(1).astype(I32)


def kernel(x_prompt, x_sample, cache_conv, state_gla, c_prompt, c_sample, w_mod, b_mod, ln_g, ln_b, ab_w_in, ab_conv_w, ab_v_ln_g, ab_v_ln_b, ab_w_s, ab_b_s, ab_w_out, gla_w_in, gla_w_gk, gla_b_gk, gla_norm_g, gla_w_out, moe_w_grp, moe_b_grp, moe_w_rt, moe_b_rt, moe_w1, moe_w3, moe_w2):
    bp, seq, d = x_prompt.shape
    bs, s_len, _ = x_sample.shape
    assert s_len == SUBLANES_V7X and seq % SUBLANES_V7X == 0
    depth = w_mod.shape[0]
    alpha = float((2 * depth) ** 0.25)
    t_p, t_s = bp * seq, bs * s_len
    t = t_p + t_s
    n_groups, n_exp = moe_w_rt.shape[1], moe_w_rt.shape[3]
    n_e = n_groups * n_exp
    d_ff = moe_w1.shape[-1]
    tg = 256
    n_tiles = (TOP_K_INNER * t) // tg + n_e

    x = (x_prompt.reshape(t_p // SUBLANES_V7X, SUBLANES_V7X, d), x_sample)
    table = _mod_table(c_prompt, c_sample, w_mod, b_mod, seq)

    w1 = moe_w1.reshape(depth * n_e, d, d_ff)
    w3 = moe_w3.reshape(depth * n_e, d, d_ff)
    w2 = moe_w2.reshape(depth * n_e, d_ff, d)

    conv_p, conv_s, chunk_v, gla_p, gla_s = [], [], [], [], []
    h_bf = None
    for layer in range(depth):
        li = layer // 2
        if layer % 2 == 0:
            n_heads, chunk = ab_w_s.shape[1], ab_w_s.shape[2]
            dc = ab_conv_w.shape[-1]
            if h_bf is None:
                p = _mm(x, ab_w_in, li, ab_w_in.shape[-1], mod=(table, layer, 0, 1))
            else:
                p = _mm(h_bf, ab_w_in, li, ab_w_in.shape[-1])
            w_s = ab_w_s[li]
            wm_p = jnp.tril(w_s)
            reps = chunk // s_len
            blk = jnp.tril(w_s[:, :s_len, :s_len])
            wm_s = jnp.einsum("ab,hts->hatbs", jnp.eye(reps, dtype=F32), blk).reshape(n_heads, chunk, chunk)
            wm = jnp.stack([wm_p, wm_s]).astype(BF16)
            b_s = ab_b_s[li]
            hd = dc // n_heads
            bias_p = jnp.repeat(b_s.T, hd, axis=1)
            bias_s = jnp.repeat(jnp.tile(b_s[:, :s_len].T, (reps, 1)), hd, axis=1)
            bias = jnp.stack([bias_p, bias_s])
            y, cp_new, cs_new, vn_s = _mix0(p, cache_conv[li], ab_conv_w[li], ab_v_ln_g[li], ab_v_ln_b[li],
                                            wm, bias, t_prompt=t_p, seq=seq, n_heads=n_heads)
            conv_p.append(cp_new)
            conv_s.append(cs_new)
            chunk_v.append(vn_s)
            w_out = ab_w_out[li].astype(BF16)
        else:
            n_heads, dk, dv = state_gla.shape[2], state_gla.shape[3], state_gla.shape[4]
            dkt, dvt = n_heads * dk, n_heads * dv
            rank = gla_w_gk.shape[1]
            n_main = 2 * dkt + 2 * dvt
            p = _mm(h_bf, gla_w_in[li][:, :n_main].astype(BF16)[None], 0, n_main, tm=1024)
            w_lo = jnp.pad(gla_w_in[li][:, n_main:], ((0, 0), (0, LANES_V7X - rank)))[None]
            gl = _mm(h_bf, w_lo, 0, LANES_V7X)
            wgk = jnp.pad(gla_w_gk[li], ((0, LANES_V7X - rank), (0, 0))).astype(BF16)
            y, sp_new, ss_new = _gla(p, gl, wgk, wgk.T, gla_b_gk[li].reshape(1, dkt), gla_b_gk[li].reshape(dkt, 1),
                                     gla_norm_g[li].reshape(1, dv), state_gla[li],
                                     t_prompt=t_p, seq=seq, n_heads=n_heads, dk=dk, dv=dv)
            gla_p.append(sp_new)
            gla_s.append(ss_new)
            w_out = gla_w_out[li].astype(BF16)

        wr = jnp.concatenate([moe_w_grp[layer].T,
                              jnp.transpose(moe_w_rt[layer], (0, 2, 1)).reshape(n_e, d)], axis=0)
        wr = jnp.pad(wr, ((0, LANES_V7X - wr.shape[0]), (0, 0)))
        br = jnp.concatenate([moe_b_grp[layer], moe_b_rt[layer].reshape(n_e)])
        br = jnp.pad(br, (0, LANES_V7X - br.shape[0])).reshape(LANES_V7X, 1)
        x1, h2, eid, wt, rank_, cnt = _outln(y, w_out, x, table, layer, ln_g[layer, 0], ln_b[layer, 0], wr, br,
                                             alpha=alpha, n_groups=n_groups, n_exp=n_exp)
        pos, te, nxt, n_used = _moe_schedule(eid, rank_, cnt[:, 0], tg=tg, n_tiles=n_tiles)
        ys = _moe(h2, w1, w3, w2, layer * n_e, pos, te, nxt, n_used, tg=tg, n_tiles=n_tiles)
        has_next = layer + 1 < depth
        outs = _comb(pos, ys, wt.T, x1, table, layer, ln_g[layer, 1], ln_b[layer, 1],
                     alpha=alpha, has_next=has_next, t_prompt=t_p)
        if has_next:
            x, h_bf = outs

    y_prompt = outs[0].reshape(bp, seq, d)
    y_sample = outs[1].reshape(bs, s_len, d)
    return (y_prompt, y_sample, jnp.stack(conv_p), jnp.stack(conv_s), jnp.stack(chunk_v),
            jnp.stack(gla_p), jnp.stack(gla_s))
```

```python
import functools

import jax
import numpy as np
import jax.numpy as jnp
from jax import lax
from jax.experimental import pallas as pl
from jax.experimental.pallas import tpu as pltpu

F32 = jnp.float32
BF16 = jnp.bfloat16
I32 = jnp.int32

LN_EPS = 1e-5
RMS_EPS = 1e-6
GLA_GATE_NORMALIZER = 16.0
TOP_K_INNER = 2

SUBLANES_V7X = 8
LANES_V7X = 128
VMEM_LIMIT_V7X = 56 * 1024 * 1024

NT_DIMS = (((1,), (1,)), ((), ()))
TN_DIMS = (((0,), (0,)), ((), ()))


def _params(*sem):
    return pltpu.CompilerParams(dimension_semantics=sem, vmem_limit_bytes=VMEM_LIMIT_V7X)


def _silu(x):
    return x * (1.0 / (1.0 + jnp.exp(-x)))


def _log_sigmoid(z):
    return jnp.minimum(z, 0.0) - jnp.log(1.0 + jnp.exp(-jnp.abs(z)))


def _layer_norm(x, g, b):
    mu = jnp.mean(x, axis=-1, keepdims=True)
    xc = x - mu
    var = jnp.mean(xc * xc, axis=-1, keepdims=True)
    return xc * lax.rsqrt(var + LN_EPS) * g + b


def _dot(a, b):
    return jnp.dot(a, b, preferred_element_type=F32)


def _split_bf16(x):
    hi = x.astype(BF16)
    lo = (x - hi.astype(F32)).astype(BF16)
    return hi, lo


def _mod_kernel(cp_ref, cs_ref, w_ref, b_ref, o_ref, *, n_prompt, groups_per_seq):
    w = w_ref[...].astype(BF16)
    b = b_ref[...]
    rp = _dot(_silu(cp_ref[...]).astype(BF16), w) + b
    rs = _dot(_silu(cs_ref[...]).astype(BF16), w) + b
    tn = o_ref.shape[-1]
    for s in range(n_prompt):
        o_ref[s * groups_per_seq:(s + 1) * groups_per_seq, :] = jnp.broadcast_to(
            rp[s:s + 1, :], (groups_per_seq, tn))
    o_ref[n_prompt * groups_per_seq:, :] = rs


def _mod_table(c_prompt, c_sample, w_mod, b_mod, seq):
    depth, d, n = w_mod.shape
    bp, bs = c_prompt.shape[0], c_sample.shape[0]
    gps = seq // SUBLANES_V7X
    g_total = bp * gps + bs
    cp = jnp.pad(c_prompt, ((0, (-bp) % SUBLANES_V7X), (0, 0)))
    tn = 1024
    return pl.pallas_call(
        functools.partial(_mod_kernel, n_prompt=bp, groups_per_seq=gps),
        grid=(depth, n // tn),
        in_specs=[
            pl.BlockSpec(cp.shape, lambda l, j: (0, 0)),
            pl.BlockSpec(c_sample.shape, lambda l, j: (0, 0)),
            pl.BlockSpec((None, d, tn), lambda l, j: (l, 0, j)),
            pl.BlockSpec((None, 1, tn), lambda l, j: (l, 0, j)),
        ],
        out_specs=pl.BlockSpec((None, g_total, tn), lambda l, j: (l, 0, j)),
        out_shape=jax.ShapeDtypeStruct((depth, g_total, n), F32),
        compiler_params=_params("arbitrary", "arbitrary"),
        name="mod",
    )(cp, c_sample, w_mod, b_mod.reshape(depth, 1, n))


def _pick_tile(i, n_prompt_tiles, p_ref, s_ref):
    return jnp.where(i < n_prompt_tiles, p_ref[...], s_ref[...])


def _mm_kernel(*refs, has_mod, n_prompt_tiles, w_transposed):
    if has_mod:
        xp_ref, xs_ref, sh_ref, sc_ref, w_ref, o_ref, wb_ref = refs
    else:
        a_ref, w_ref, o_ref, wb_ref = refs

    @pl.when(pl.program_id(1) == 0)
    def _():
        w = w_ref[...]
        wb_ref[...] = (w.T if w_transposed else w).astype(BF16)

    if has_mod:
        x = _pick_tile(pl.program_id(1), n_prompt_tiles, xp_ref, xs_ref)
        g, s, k = x.shape
        h = x * (1.0 + sc_ref[...][:, None, :]) + sh_ref[...][:, None, :]
        a = h.reshape(g * s, k).astype(BF16)
    else:
        a = a_ref[...]
    o_ref[...] = _dot(a, wb_ref[...]).astype(o_ref.dtype)


def _mm(a, w3, w_idx, n_out, *, mod=None, tm=512, tn=1024, out_dtype=BF16, w_transposed=False):
    k = w3.shape[2 if w_transposed else 1]
    tn = min(tn, n_out)
    n_p = 0
    if mod is None:
        t = a.shape[0]
        a_specs = [pl.BlockSpec((tm, k), lambda j, i: (i, 0))]
        ins = [a]
    else:
        table, layer, sh_col, sc_col = mod
        xp, xs = a
        gt = tm // SUBLANES_V7X
        n_p = xp.shape[0] // gt
        t = (xp.shape[0] + xs.shape[0]) * SUBLANES_V7X
        a_specs = [
            pl.BlockSpec((gt, SUBLANES_V7X, k), lambda j, i: (jnp.minimum(i, n_p - 1), 0, 0)),
            pl.BlockSpec((gt, SUBLANES_V7X, k), lambda j, i: (jnp.maximum(i - n_p, 0), 0, 0)),
            pl.BlockSpec((None, gt, k), lambda j, i: (layer, i, sh_col)),
            pl.BlockSpec((None, gt, k), lambda j, i: (layer, i, sc_col)),
        ]
        ins = [xp, xs, table, table]
    return pl.pallas_call(
        functools.partial(_mm_kernel, has_mod=mod is not None, n_prompt_tiles=n_p, w_transposed=w_transposed),
        grid=(n_out // tn, t // tm),
        in_specs=a_specs + [pl.BlockSpec((None, tn, k), lambda j, i: (w_idx, j, 0)) if w_transposed
                            else pl.BlockSpec((None, k, tn), lambda j, i: (w_idx, 0, j))],
        out_specs=pl.BlockSpec((tm, tn), lambda j, i: (i, j)),
        out_shape=jax.ShapeDtypeStruct((t, n_out), out_dtype),
        scratch_shapes=[pltpu.VMEM((k, tn), BF16)],
        compiler_params=_params("arbitrary", "arbitrary"),
        name="mm",
    )(*ins, w3)


def _mix0_kernel(bg_ref, cg_ref, hx_ref, u_ref, v_ref, cache_ref, cw_ref, vg_ref, vb_ref, wm_ref, bias_ref,
                 y_ref, convp_ref, convs_ref, vns_ref, zprev_ref, *, n_prompt_tiles, tiles_per_seq, n_heads):
    i = pl.program_id(0)
    tm, dc = bg_ref.shape
    ns = tm // SUBLANES_V7X
    z = cg_ref[...].astype(F32) * hx_ref[...].astype(F32)
    row = lax.broadcasted_iota(I32, (tm, dc), 0)
    r1 = pltpu.roll(z, 1, 0)
    r2 = pltpu.roll(z, 2, 0)
    cw = cw_ref[...]
    bg = bg_ref[...].astype(F32)

    vn = _layer_norm(v_ref[...].astype(F32), vg_ref[...], vb_ref[...])
    vnb = vn.astype(BF16)
    hd = dc // n_heads
    mixed = jnp.concatenate(
        [_dot(wm_ref[h], vnb[:, h * hd:(h + 1) * hd]) for h in range(n_heads)], axis=-1) + bias_ref[...]
    y_ref[:, dc:] = (u_ref[...].astype(F32) * mixed).astype(BF16)

    def conv_out(zm1, zm2):
        conv = cw[0:1, :] * zm2 + cw[1:2, :] * zm1 + cw[2:3, :] * z
        y_ref[:, :dc] = (bg * conv).astype(BF16)

    @pl.when(i < n_prompt_tiles)
    def _prompt():
        @pl.when(i % tiles_per_seq == 0)
        def _():
            zprev_ref[...] = jnp.zeros_like(zprev_ref)

        zp = zprev_ref[...]
        p1 = zp[SUBLANES_V7X - 1:SUBLANES_V7X, :]
        p2 = zp[SUBLANES_V7X - 2:SUBLANES_V7X - 1, :]
        conv_out(jnp.where(row == 0, p1, r1),
                 jnp.where(row == 0, p2, jnp.where(row == 1, p1, r2)))
        zprev_ref[...] = z[tm - SUBLANES_V7X:, :]
        convp_ref[...] = z[tm - 2:, :].reshape(1, 2, dc)

    @pl.when(i >= n_prompt_tiles)
    def _sample():
        c = cache_ref[...]
        c0 = jnp.broadcast_to(c[:, 0:1, :], (ns, SUBLANES_V7X, dc)).reshape(tm, dc)
        c1 = jnp.broadcast_to(c[:, 1:2, :], (ns, SUBLANES_V7X, dc)).reshape(tm, dc)
        rr = row % SUBLANES_V7X
        conv_out(jnp.where(rr == 0, c1, r1),
                 jnp.where(rr == 0, c0, jnp.where(rr == 1, c1, r2)))
        z3 = z.reshape(ns, SUBLANES_V7X, dc)
        convs_ref[...] = z3[:, SUBLANES_V7X - 2:, :]
        vns_ref[...] = vn.reshape(ns, SUBLANES_V7X, dc)


def _mix0(p, cache, conv_w, v_g, v_b, wm, bias, *, t_prompt, seq, n_heads):
    t, n = p.shape
    bs, cwm1, dc = cache.shape
    tm = wm.shape[-1]
    assert cwm1 == 2 and conv_w.shape[0] == 3 and n == 5 * dc and seq % tm == 0
    n_p = t_prompt // tm
    n_s = (t - t_prompt) // tm
    tps = seq // tm
    bp = t_prompt // seq
    ns = tm // SUBLANES_V7X

    def col(c):
        return pl.BlockSpec((tm, dc), lambda i: (i, c))

    def s_idx(i):
        return jnp.maximum(i - n_p, 0)

    const2 = lambda i: (0, 0)
    mode = lambda i: ((i >= n_p).astype(I32), 0, 0, 0)
    return pl.pallas_call(
        functools.partial(_mix0_kernel, n_prompt_tiles=n_p, tiles_per_seq=tps, n_heads=n_heads),
        grid=(n_p + n_s,),
        in_specs=[col(0), col(1), col(2), col(3), col(4),
                  pl.BlockSpec((ns, 2, dc), lambda i: (s_idx(i), 0, 0)),
                  pl.BlockSpec((3, dc), const2),
                  pl.BlockSpec((1, dc), const2),
                  pl.BlockSpec((1, dc), const2),
                  pl.BlockSpec((None, n_heads, tm, tm), mode),
                  pl.BlockSpec((None, tm, dc), lambda i: ((i >= n_p).astype(I32), 0, 0))],
        out_specs=[pl.BlockSpec((tm, 2 * dc), lambda i: (i, 0)),
                   pl.BlockSpec((1, 2, dc), lambda i: (jnp.minimum(i // tps, bp - 1), 0, 0)),
                   pl.BlockSpec((ns, 2, dc), lambda i: (s_idx(i), 0, 0)),
                   pl.BlockSpec((ns, SUBLANES_V7X, dc), lambda i: (s_idx(i), 0, 0))],
        out_shape=[jax.ShapeDtypeStruct((t, 2 * dc), BF16),
                   jax.ShapeDtypeStruct((bp, 2, dc), F32),
                   jax.ShapeDtypeStruct((bs, 2, dc), F32),
                   jax.ShapeDtypeStruct((bs, SUBLANES_V7X, dc), F32)],
        scratch_shapes=[pltpu.VMEM((SUBLANES_V7X, dc), F32)],
        compiler_params=_params("arbitrary"),
        name="mix0",
    )(p, p, p, p, p, cache, conv_w, v_g.reshape(1, dc), v_b.reshape(1, dc), wm, bias)


def _first_index_of(vals, target):
    idx = jnp.full(target.shape, len(vals) - 1, I32)
    for j in reversed(range(len(vals))):
        idx = jnp.where(vals[j] == target, j, idx)
    return idx


def _softmax_rows(rows):
    m = functools.reduce(jnp.maximum, rows)
    e = [jnp.exp(r - m) for r in rows]
    s = functools.reduce(lambda a, b: a + b, e)
    return [x / s for x in e]


def _route(h, wr_ref, br_ref, eid_ref, wt_ref, rank_ref, cnt_ref, carry_ref, n_groups, n_exp):
    tm = h.shape[0]
    hh, hl = _split_bf16(h)
    wh, wl = _split_bf16(wr_ref[...])
    dg = lambda a, b: lax.dot_general(a, b, NT_DIMS, preferred_element_type=F32)
    logits = dg(wh, hh) + dg(wh, hl) + dg(wl, hh) + br_ref[...]

    g_prob = _softmax_rows([logits[g:g + 1, :] for g in range(n_groups)])
    g_top = functools.reduce(jnp.maximum, g_prob)
    g_idx = _first_index_of(g_prob, g_top)

    e_sel = []
    for e in range(n_exp):
        sel = logits[n_groups + e:n_groups + e + 1, :]
        for g in range(1, n_groups):
            r = n_groups + g * n_exp + e
            sel = jnp.where(g_idx == g, logits[r:r + 1, :], sel)
        e_sel.append(sel)
    e_prob = _softmax_rows(e_sel)
    p1 = functools.reduce(jnp.maximum, e_prob)
    i1 = _first_index_of(e_prob, p1)
    rest = [jnp.where(i1 == e, -1.0, e_prob[e]) for e in range(n_exp)]
    p2 = functools.reduce(jnp.maximum, rest)
    i2 = _first_index_of(rest, p2)
    den = p1 + p2
    wt_ref[0:1, :] = g_top * (p1 / den)
    wt_ref[1:2, :] = g_top * (p2 / den)
    eid0 = g_idx * n_exp + i1
    eid1 = g_idx * n_exp + i2
    eid_ref[0:1, :] = eid0
    eid_ref[1:2, :] = eid1

    n_e = n_groups * n_exp
    eio = lax.broadcasted_iota(I32, (n_e, tm), 0)
    oh0 = (eio == eid0).astype(F32)
    oh1 = (eio == eid1).astype(F32)
    oh = oh0 + oh1
    before = (lax.broadcasted_iota(I32, (tm, tm), 0) < lax.broadcasted_iota(I32, (tm, tm), 1)).astype(BF16)
    base = _dot(oh.astype(BF16), before) + carry_ref[...]
    rank_ref[0:1, :] = jnp.sum(oh0 * base, axis=0, keepdims=True).astype(I32)
    rank_ref[1:2, :] = jnp.sum(oh1 * base, axis=0, keepdims=True).astype(I32)
    total = carry_ref[...] + jnp.sum(oh, axis=1, keepdims=True)
    carry_ref[...] = total
    cnt_ref[...] = jnp.broadcast_to(total, cnt_ref.shape).astype(I32)


def _outln_kernel(*refs, alpha, n_groups, n_exp, y_split, x_split, n_prompt_tiles):
    refs = list(refs)
    i = pl.program_id(0)
    take = lambda split: [refs.pop(0) for _ in range(2 if split else 1)]
    y_refs, (w_ref,), x_refs = take(y_split), take(False), take(x_split)
    (gt_ref, sh_ref, sc_ref, lng_ref, lnb_ref, wr_ref, br_ref,
     x1_ref, h_ref, eid_ref, wt_ref, rank_ref, cnt_ref, carry_ref) = refs

    @pl.when(i == 0)
    def _():
        carry_ref[...] = jnp.zeros_like(carry_ref)

    y = _pick_tile(i, n_prompt_tiles, *y_refs) if y_split else y_refs[0][...]
    x = _pick_tile(i, n_prompt_tiles, *x_refs) if x_split else x_refs[0][...]
    g, s, d = x.shape
    m = _dot(y, w_ref[...]).reshape(g, s, d)
    x1 = _layer_norm(alpha * x + gt_ref[...][:, None, :] * m, lng_ref[...], lnb_ref[...])
    x1_ref[...] = x1
    h = (x1 * (1.0 + sc_ref[...][:, None, :]) + sh_ref[...][:, None, :]).reshape(g * s, d)
    h_ref[...] = h
    _route(h, wr_ref, br_ref, eid_ref, wt_ref, rank_ref, cnt_ref, carry_ref, n_groups, n_exp)


def _outln(y, w_bf, x, table, layer, ln_g, ln_b, wr, br, *, alpha, n_groups, n_exp, tm=512):
    y_split, x_split = isinstance(y, tuple), isinstance(x, tuple)
    ys, xs = (y if y_split else (y,)), (x if x_split else (x,))
    k, d = w_bf.shape
    gt = tm // SUBLANES_V7X
    n_e = n_groups * n_exp
    t = sum(a.shape[0] for a in ys)
    n_p = (ys[0].shape[0] // tm) if y_split else (xs[0].shape[0] // gt if x_split else 0)

    def split_specs(block, n_arrays):
        if n_arrays == 1:
            return [pl.BlockSpec(block, lambda i: (i,) + (0,) * (len(block) - 1))]
        return [pl.BlockSpec(block, lambda i: (jnp.minimum(i, n_p - 1),) + (0,) * (len(block) - 1)),
                pl.BlockSpec(block, lambda i: (jnp.maximum(i - n_p, 0),) + (0,) * (len(block) - 1))]

    def mod(c):
        return pl.BlockSpec((None, gt, d), lambda i: (layer, i, c))

    const = lambda i: (0, 0)
    pair = lambda dt: jax.ShapeDtypeStruct((TOP_K_INNER, t), dt)
    pair_spec = pl.BlockSpec((TOP_K_INNER, tm), lambda i: (0, i))
    return pl.pallas_call(
        functools.partial(_outln_kernel, alpha=alpha, n_groups=n_groups, n_exp=n_exp,
                          y_split=y_split, x_split=x_split, n_prompt_tiles=n_p),
        grid=(t // tm,),
        in_specs=split_specs((tm, k), len(ys))
                 + [pl.BlockSpec((k, d), const, pipeline_mode=pl.Buffered(1))]
                 + split_specs((gt, SUBLANES_V7X, d), len(xs))
                 + [mod(2), mod(3), mod(4),
                  pl.BlockSpec((1, d), const),
                  pl.BlockSpec((1, d), const),
                  pl.BlockSpec(wr.shape, const),
                  pl.BlockSpec(br.shape, const)],
        out_specs=[pl.BlockSpec((gt, SUBLANES_V7X, d), lambda i: (i, 0, 0)),
                   pl.BlockSpec((tm, d), lambda i: (i, 0)),
                   pair_spec, pair_spec, pair_spec,
                   pl.BlockSpec((n_e, LANES_V7X), const)],
        out_shape=[jax.ShapeDtypeStruct((t // SUBLANES_V7X, SUBLANES_V7X, d), F32),
                   jax.ShapeDtypeStruct((t, d), F32),
                   pair(I32), pair(F32), pair(I32),
                   jax.ShapeDtypeStruct((n_e, LANES_V7X), I32)],
        scratch_shapes=[pltpu.VMEM((n_e, 1), F32)],
        compiler_params=_params("arbitrary"),
        name="outln",
    )(*ys, w_bf, *xs, table, table, table, ln_g.reshape(1, d), ln_b.reshape(1, d), wr, br)


def _row_gather_start(src_hbm, dst, sem, idx_ref, base, n):
    def body(r, c):
        row = idx_ref[base + r]
        pltpu.make_async_copy(src_hbm.at[pl.ds(row, 1)], dst.at[pl.ds(r, 1)], sem).start()
        return c
    lax.fori_loop(0, n, body, 0, unroll=8)


def _row_gather_wait(src_hbm, dst, sem, n):
    pltpu.make_async_copy(src_hbm.at[pl.ds(0, n)], dst, sem).wait()


def _moe_kernel(pos_ref, te_ref, nxt_ref, nu_ref, h_hbm, w1_hbm, w3_hbm, w2_hbm, o_ref,
                xbuf, w1s, w3s, w2s, w1b, w3b, w2b, src_ref, wslot_ref, xsem, wsem, *, tg, t_total, w_base):
    i = pl.program_id(0)
    n_used = nu_ref[0]

    def weight_copies(e, slot):
        return [pltpu.make_async_copy(hbm.at[w_base + e], stage.at[slot], wsem.at[slot])
                for hbm, stage in ((w1_hbm, w1s), (w3_hbm, w3s), (w2_hbm, w2s))]

    @pl.when(i == 0)
    def _():
        def clear(p, c):
            src_ref[p] = 0
            return c
        lax.fori_loop(0, src_ref.shape[0], clear, 0, unroll=8)

        def fill(t, c):
            for k in range(TOP_K_INNER):
                src_ref[pos_ref[k * t_total + t]] = t
            return c
        lax.fori_loop(0, t_total, fill, 0, unroll=8)

        wslot_ref[0] = 1
        for cp in weight_copies(te_ref[0], 0):
            cp.start()
        _row_gather_start(h_hbm, xbuf.at[0], xsem.at[0], src_ref, 0, tg)

    @pl.when(i < n_used)
    def _():
        slot = i % 2

        @pl.when(i + 1 < n_used)
        def _():
            _row_gather_start(h_hbm, xbuf.at[1 - slot], xsem.at[1 - slot], src_ref, (i + 1) * tg, tg)

        e = te_ref[i]

        @pl.when((i == 0) | (e != te_ref[jnp.maximum(i - 1, 0)]))
        def _():
            ws = 1 - wslot_ref[0]
            wslot_ref[0] = ws
            for cp in weight_copies(e, ws):
                cp.wait()
            w1b[...] = w1s[ws].astype(BF16)
            w3b[...] = w3s[ws].astype(BF16)
            w2b[...] = w2s[ws].astype(BF16)
            ne = nxt_ref[e]

            @pl.when(ne >= 0)
            def _():
                for cp in weight_copies(ne, 1 - ws):
                    cp.start()

        _row_gather_wait(h_hbm, xbuf.at[slot], xsem.at[slot], tg)
        x = xbuf[slot].astype(BF16)
        a = _dot(x, w1b[...])
        b = _dot(x, w3b[...])
        o_ref[...] = _dot((_silu(a) * b).astype(BF16), w2b[...])

    @pl.when(i >= n_used)
    def _():
        o_ref[...] = jnp.zeros_like(o_ref)


def _moe(h, w1, w3, w2, w_base, pos, te, nxt, n_used, *, tg, n_tiles):
    t, d = h.shape
    f = w1.shape[-1]
    any_spec = pl.BlockSpec(memory_space=pl.ANY)
    grid_spec = pltpu.PrefetchScalarGridSpec(
        num_scalar_prefetch=4,
        grid=(n_tiles,),
        in_specs=[any_spec, any_spec, any_spec, any_spec],
        out_specs=pl.BlockSpec((tg, d), lambda i, *_: (i, 0)),
        scratch_shapes=[pltpu.VMEM((2, tg, d), F32),
                        pltpu.VMEM((2, d, f), F32), pltpu.VMEM((2, d, f), F32), pltpu.VMEM((2, f, d), F32),
                        pltpu.VMEM((d, f), BF16), pltpu.VMEM((d, f), BF16), pltpu.VMEM((f, d), BF16),
                        pltpu.SMEM((n_tiles * tg,), I32), pltpu.SMEM((1,), I32),
                        pltpu.SemaphoreType.DMA((2,)), pltpu.SemaphoreType.DMA((2,))],
    )
    return pl.pallas_call(
        functools.partial(_moe_kernel, tg=tg, t_total=t, w_base=w_base),
        grid_spec=grid_spec,
        out_shape=jax.ShapeDtypeStruct((n_tiles * tg, d), F32),
        compiler_params=_params("arbitrary"),
        name="moe",
    )(pos, te, nxt, n_used, h, w1, w3, w2)


def _comb_kernel(pos_ref, y_hbm, wt_ref, x_ref, gt_ref, lng_ref, lnb_ref, *rest,
                 alpha, tm, t_total, has_next, n_prompt_tiles):
    if has_next:
        shn_ref, scn_ref, x2_ref, hn_ref, ybuf, sem = rest
    else:
        x2p_ref, x2s_ref, ybuf, sem = rest
    i = pl.program_id(0)
    n = pl.num_programs(0)

    def start(tile, slot):
        for k in range(TOP_K_INNER):
            _row_gather_start(y_hbm, ybuf.at[slot, k], sem.at[slot], pos_ref, k * t_total + tile * tm, tm)

    @pl.when(i == 0)
    def _():
        start(0, 0)

    slot = i % 2

    @pl.when(i + 1 < n)
    def _():
        start(i + 1, 1 - slot)

    for k in range(TOP_K_INNER):
        _row_gather_wait(y_hbm, ybuf.at[slot, k], sem.at[slot], tm)
    w = wt_ref[...]
    f = w[:, 0:1] * ybuf[slot, 0] + w[:, 1:2] * ybuf[slot, 1]
    g, s, d = x_ref.shape
    x2 = _layer_norm(alpha * x_ref[...] + gt_ref[...][:, None, :] * f.reshape(g, s, d), lng_ref[...], lnb_ref[...])
    if has_next:
        x2_ref[...] = x2
        hn = x2 * (1.0 + scn_ref[...][:, None, :]) + shn_ref[...][:, None, :]
        hn_ref[...] = hn.reshape(g * s, d).astype(BF16)
    else:
        @pl.when(i < n_prompt_tiles)
        def _():
            x2p_ref[...] = x2

        @pl.when(i >= n_prompt_tiles)
        def _():
            x2s_ref[...] = x2


def _comb(pos_flat, y_sorted, wt_t, x1, table, layer, ln_g, ln_b, *, alpha, has_next, t_prompt, tm=256):
    g_total, s, d = x1.shape
    t = g_total * s
    gt = tm // SUBLANES_V7X
    n_p = t_prompt // tm

    def mod(l, c):
        return pl.BlockSpec((None, gt, d), lambda i, pos: (l, i, c))

    xspec = pl.BlockSpec((gt, s, d), lambda i, pos: (i, 0, 0))
    vec = pl.BlockSpec((1, d), lambda i, pos: (0, 0))
    in_specs = [pl.BlockSpec(memory_space=pl.ANY),
                pl.BlockSpec((tm, TOP_K_INNER), lambda i, pos: (i, 0)),
                xspec, mod(layer, 5), vec, vec]
    ins = [y_sorted, wt_t, x1, table, ln_g.reshape(1, d), ln_b.reshape(1, d)]
    if has_next:
        in_specs += [mod(layer + 1, 0), mod(layer + 1, 1)]
        ins += [table, table]
        out_specs = [xspec, pl.BlockSpec((tm, d), lambda i, pos: (i, 0))]
        out_shape = [jax.ShapeDtypeStruct(x1.shape, F32), jax.ShapeDtypeStruct((t, d), BF16)]
    else:
        out_specs = [pl.BlockSpec((gt, s, d), lambda i, pos: (jnp.minimum(i, n_p - 1), 0, 0)),
                     pl.BlockSpec((gt, s, d), lambda i, pos: (jnp.maximum(i - n_p, 0), 0, 0))]
        out_shape = [jax.ShapeDtypeStruct((t_prompt // s, s, d), F32),
                     jax.ShapeDtypeStruct(((t - t_prompt) // s, s, d), F32)]
    grid_spec = pltpu.PrefetchScalarGridSpec(
        num_scalar_prefetch=1,
        grid=(t // tm,),
        in_specs=in_specs,
        out_specs=out_specs,
        scratch_shapes=[pltpu.VMEM((2, TOP_K_INNER, tm, d), F32), pltpu.SemaphoreType.DMA((2,))],
    )
    return pl.pallas_call(
        functools.partial(_comb_kernel, alpha=alpha, tm=tm, t_total=t, has_next=has_next, n_prompt_tiles=n_p),
        grid_spec=grid_spec,
        out_shape=out_shape,
        compiler_params=_params("arbitrary"),
        name="comb",
    )(pos_flat, *ins)


def _gla_tables(rows, span):
    t = np.arange(rows)[:, None]
    u = np.arange(rows)[None, :]
    same = (t // span) == (u // span)
    mats = [same & (u <= t), same]
    n_levels, m = 0, 1
    while m < span:
        mid = (t // (2 * m)) * (2 * m) + m - 1
        mats.append(same & np.where(t > mid, (u > mid) & (u <= t), (u > t) & (u <= mid)))
        m *= 2
        n_levels += 1
    x = t ^ u
    level = np.where(x > 0, np.floor(np.log2(np.maximum(x, 1))), n_levels).astype(np.int32)
    lid = np.where(same & (u <= t), level, -1).astype(np.int32)
    grp = (t // span) == np.arange(LANES_V7X)[None, :]
    stack = np.concatenate(mats, axis=0).astype(np.float32)
    return jnp.asarray(stack, BF16), jnp.asarray(lid), jnp.asarray(grp.astype(np.float32), BF16), n_levels


def _gla_block(q, k, la, stack, lid, grp, n_levels):
    rows = q.shape[0]
    hi, lo = _split_bf16(la)
    dall = _dot(stack, hi) + _dot(stack, lo)
    cum, total = dall[0:rows], dall[rows:2 * rows]
    group_total = (lax.dot_general(hi, grp, TN_DIMS, preferred_element_type=F32)
                   + lax.dot_general(lo, grp, TN_DIMS, preferred_element_type=F32))
    nt = lambda a, b: lax.dot_general(a.astype(BF16), b.astype(BF16), NT_DIMS, preferred_element_type=F32)
    sc = jnp.where(lid == n_levels, nt(q, k), 0.0)
    for l in range(n_levels):
        e = jnp.exp(dall[(2 + l) * rows:(3 + l) * rows])
        sc = jnp.where(lid == l, nt(q * e, k * e), sc)
    return sc.astype(BF16), cum, total, group_total


def _gla_log_decay(gl, wgk, bg_row):
    return _log_sigmoid(_dot(gl, wgk) + bg_row) * (1.0 / GLA_GATE_NORMALIZER)


def _rms_gate(o, ng, gate):
    on = o * lax.rsqrt(jnp.mean(o * o, axis=-1, keepdims=True) + RMS_EPS) * ng
    return (on * _silu(gate)).astype(BF16)


def _gla_prompt_kernel(q_ref, k_ref, v_ref, g_ref, gl_ref, wgk_ref, bgr_ref, ng_ref, stack_ref, lid_ref, grp_ref,
                       y_ref, sout_ref, st_ref, *, n_heads, scale, n_levels):
    j = pl.program_id(1)

    @pl.when(j == 0)
    def _():
        st_ref[...] = jnp.zeros_like(st_ref)

    dk = q_ref.shape[1] // n_heads
    dv = v_ref.shape[1] // n_heads
    gl = gl_ref[...]
    for h in range(n_heads):
        ks, vs = slice(h * dk, (h + 1) * dk), slice(h * dv, (h + 1) * dv)
        q = q_ref[:, ks].astype(F32) * scale
        k = k_ref[:, ks].astype(F32)
        v = v_ref[:, vs]
        la = _gla_log_decay(gl, wgk_ref[:, ks], bgr_ref[:, ks])
        sc, cum, total, group_total = _gla_block(q, k, la, stack_ref[...], lid_ref[...], grp_ref[...], n_levels)
        s_old = st_ref[h]
        o = _dot(sc, v) + _dot((q * jnp.exp(cum)).astype(BF16), s_old.astype(BF16))
        kd = (k * jnp.exp(total - cum)).astype(BF16)
        st_ref[h] = s_old * jnp.exp(group_total[:, 0:1]) + lax.dot_general(
            kd, v, TN_DIMS, preferred_element_type=F32)
        y_ref[:, vs] = _rms_gate(o, ng_ref[...], g_ref[:, vs].astype(F32))

    @pl.when(j == pl.num_programs(1) - 1)
    def _():
        sout_ref[...] = st_ref[...]


def _gla_sample_kernel(q_ref, k_ref, v_ref, g_ref, gl_ref, wgk_ref, bgr_ref, ng_ref, stack_ref, lid_ref, grp_ref,
                       sin_ref, y_ref, sout_ref, *, n_heads, scale, seq, n_levels):
    rows = q_ref.shape[0]
    nb = rows // seq
    dk = q_ref.shape[1] // n_heads
    dv = v_ref.shape[1] // n_heads
    gl = gl_ref[...]
    for h in range(n_heads):
        ks, vs = slice(h * dk, (h + 1) * dk), slice(h * dv, (h + 1) * dv)
        q = q_ref[:, ks].astype(F32) * scale
        k = k_ref[:, ks].astype(F32)
        v = v_ref[:, vs]
        la = _gla_log_decay(gl, wgk_ref[:, ks], bgr_ref[:, ks])
        sc, cum, total, group_total = _gla_block(q, k, la, stack_ref[...], lid_ref[...], grp_ref[...], n_levels)
        q_dec = q * jnp.exp(cum)
        kd = k * jnp.exp(total - cum)
        v32 = v.astype(F32)
        o_state = []
        for s in range(nb):
            rs = slice(s * seq, (s + 1) * seq)
            s_old = sin_ref[s, h]
            o_state.append(_dot(q_dec[rs, :].astype(BF16), s_old.astype(BF16)))
            sout_ref[s, h] = s_old * jnp.exp(group_total[:, s:s + 1]) + lax.dot_general(
                kd[rs, :].astype(BF16), v32[rs, :].astype(BF16), TN_DIMS, preferred_element_type=F32)
        o = _dot(sc, v) + jnp.concatenate(o_state, axis=0)
        y_ref[:, vs] = _rms_gate(o, ng_ref[...], g_ref[:, vs].astype(F32))


def _gla(p, gl, wgk, bg_row, ng, state_s, *, t_prompt, seq, n_heads, dk, dv, bk=128, nb=4):
    t = p.shape[0]
    bs, _, _, _ = state_s.shape
    s_len = (t - t_prompt) // bs
    bp = t_prompt // seq
    dkt, dvt = n_heads * dk, n_heads * dv
    assert dvt == 2 * dkt
    scale = dk ** -0.5
    r = gl.shape[1]
    const = lambda *_: (0, 0)

    def tables(rows, span):
        stack, lid, grp, n_levels = _gla_tables(rows, span)
        specs = [pl.BlockSpec((r, dkt), const), pl.BlockSpec((1, dkt), const), pl.BlockSpec((1, dv), const),
                 pl.BlockSpec(stack.shape, const), pl.BlockSpec(lid.shape, const), pl.BlockSpec(grp.shape, const)]
        return specs, [wgk, bg_row, ng, stack, lid, grp], n_levels

    nblk = seq // bk
    rowp = lambda b, j: b * nblk + j
    w_specs, w_ins, n_levels = tables(bk, bk)
    y_p, s_p = pl.pallas_call(
        functools.partial(_gla_prompt_kernel, n_heads=n_heads, scale=scale, n_levels=n_levels),
        grid=(bp, nblk),
        in_specs=[pl.BlockSpec((bk, dkt), lambda b, j: (rowp(b, j), 0)),
                  pl.BlockSpec((bk, dkt), lambda b, j: (rowp(b, j), 1)),
                  pl.BlockSpec((bk, dvt), lambda b, j: (rowp(b, j), 1)),
                  pl.BlockSpec((bk, dvt), lambda b, j: (rowp(b, j), 2)),
                  pl.BlockSpec((bk, r), lambda b, j: (rowp(b, j), 0))] + w_specs,
        out_specs=[pl.BlockSpec((bk, dvt), lambda b, j: (rowp(b, j), 0)),
                   pl.BlockSpec((None, n_heads, dk, dv), lambda b, j: (b, 0, 0, 0))],
        out_shape=[jax.ShapeDtypeStruct((t_prompt, dvt), BF16),
                   jax.ShapeDtypeStruct((bp, n_heads, dk, dv), F32)],
        scratch_shapes=[pltpu.VMEM((n_heads, dk, dv), F32)],
        compiler_params=_params("arbitrary", "arbitrary"),
        name="gla_prompt",
    )(p, p, p, p, gl, *w_ins)

    rows = nb * s_len
    off = t_prompt // rows
    w_specs, w_ins, n_levels = tables(rows, s_len)
    y_s, s_s = pl.pallas_call(
        functools.partial(_gla_sample_kernel, n_heads=n_heads, scale=scale, seq=s_len, n_levels=n_levels),
        grid=(bs // nb,),
        in_specs=[pl.BlockSpec((rows, dkt), lambda i: (off + i, 0)),
                  pl.BlockSpec((rows, dkt), lambda i: (off + i, 1)),
                  pl.BlockSpec((rows, dvt), lambda i: (off + i, 1)),
                  pl.BlockSpec((rows, dvt), lambda i: (off + i, 2)),
                  pl.BlockSpec((rows, r), lambda i: (off + i, 0))] + w_specs + [
                  pl.BlockSpec((nb, n_heads, dk, dv), lambda i: (i, 0, 0, 0))],
        out_specs=[pl.BlockSpec((rows, dvt), lambda i: (i, 0)),
                   pl.BlockSpec((nb, n_heads, dk, dv), lambda i: (i, 0, 0, 0))],
        out_shape=[jax.ShapeDtypeStruct((t - t_prompt, dvt), BF16),
                   jax.ShapeDtypeStruct(state_s.shape, F32)],
        compiler_params=_params("arbitrary"),
        name="gla_sample",
    )(p, p, p, p, gl, *w_ins, state_s)
    return (y_p, y_s), s_p, s_s


def _moe_schedule(eid, rank, counts, *, tg, n_tiles):
    n_e = counts.shape[0]
    e_ids = jnp.arange(n_e, dtype=I32)
    padded = ((counts + tg - 1) // tg) * tg
    ends = jnp.sum(jnp.where(e_ids[None, :] <= e_ids[:, None], padded[None, :], 0), axis=1)
    starts = ends - padded
    pos = jnp.sum(jnp.where(eid[None] == e_ids[:, None, None], starts[:, None, None], 0), axis=0) + rank
    n_used = ends[n_e - 1] // tg
    tile_start = jnp.arange(n_tiles, dtype=I32) * tg
    te = jnp.sum((ends[None, :] <= tile_start[:, None]).astype(I32), axis=1)
    te_last = jnp.sum((ends <= (n_used - 1) * tg).astype(I32))
    te = jnp.where(jnp.arange(n_tiles) < n_used, te, te_last)
    later = (e_ids[None, :] > e_ids[:, None]) & (counts[None, :] > 0)
    nxt = jnp.min(jnp.where(later, e_ids[None, :], n_e), axis=1)
    nxt = jnp.where(nxt == n_e, -1, nxt)
    return pos.reshape(-1).astype(I32), te.astype(I32), nxt.astype(I32), n_used.reshape(1).astype(I32)


def kernel(x_prompt, x_sample, cache_conv, state_gla, c_prompt, c_sample, w_mod, b_mod, ln_g, ln_b, ab_w_in, ab_conv_w, ab_v_ln_g, ab_v_ln_b, ab_w_s, ab_b_s, ab_w_out, gla_w_in, gla_w_gk, gla_b_gk, gla_norm_g, gla_w_out, moe_w_grp, moe_b_grp, moe_w_rt, moe_b_rt, moe_w1, moe_w3, moe_w2):
    bp, seq, d = x_prompt.shape
    bs, s_len, _ = x_sample.shape
    assert s_len == SUBLANES_V7X and seq % SUBLANES_V7X == 0
    depth = w_mod.shape[0]
    alpha = float((2 * depth) ** 0.25)
    t_p, t_s = bp * seq, bs * s_len
    t = t_p + t_s
    n_groups, n_exp = moe_w_rt.shape[1], moe_w_rt.shape[3]
    n_e = n_groups * n_exp
    d_ff = moe_w1.shape[-1]
    tg = 256
    n_tiles = (TOP_K_INNER * t) // tg + n_e

    x = (x_prompt.reshape(t_p // SUBLANES_V7X, SUBLANES_V7X, d), x_sample)
    table = _mod_table(c_prompt, c_sample, w_mod, b_mod, seq)

    w1 = moe_w1.reshape(depth * n_e, d, d_ff)
    w3 = moe_w3.reshape(depth * n_e, d, d_ff)
    w2 = moe_w2.reshape(depth * n_e, d_ff, d)

    conv_p, conv_s, chunk_v, gla_p, gla_s = [], [], [], [], []
    h_bf = None
    for layer in range(depth):
        li = layer // 2
        if layer % 2 == 0:
            n_heads, chunk = ab_w_s.shape[1], ab_w_s.shape[2]
            dc = ab_conv_w.shape[-1]
            if h_bf is None:
                p = _mm(x, ab_w_in, li, ab_w_in.shape[-1], mod=(table, layer, 0, 1))
            else:
                p = _mm(h_bf, ab_w_in, li, ab_w_in.shape[-1])
            w_s = ab_w_s[li]
            wm_p = jnp.tril(w_s)
            reps = chunk // s_len
            blk = jnp.tril(w_s[:, :s_len, :s_len])
            wm_s = jnp.einsum("ab,hts->hatbs", jnp.eye(reps, dtype=F32), blk).reshape(n_heads, chunk, chunk)
            wm = jnp.stack([wm_p, wm_s]).astype(BF16)
            b_s = ab_b_s[li]
            hd = dc // n_heads
            bias_p = jnp.repeat(b_s.T, hd, axis=1)
            bias_s = jnp.repeat(jnp.tile(b_s[:, :s_len].T, (reps, 1)), hd, axis=1)
            bias = jnp.stack([bias_p, bias_s])
            y, cp_new, cs_new, vn_s = _mix0(p, cache_conv[li], ab_conv_w[li], ab_v_ln_g[li], ab_v_ln_b[li],
                                            wm, bias, t_prompt=t_p, seq=seq, n_heads=n_heads)
            conv_p.append(cp_new)
            conv_s.append(cs_new)
            chunk_v.append(vn_s)
            w_out = ab_w_out[li].astype(BF16)
        else:
            n_heads, dk, dv = state_gla.shape[2], state_gla.shape[3], state_gla.shape[4]
            dkt, dvt = n_heads * dk, n_heads * dv
            rank = gla_w_gk.shape[1]
            n_main = 2 * dkt + 2 * dvt
            w_in_t = jnp.swapaxes(gla_w_in, 1, 2)
            p = _mm(h_bf, w_in_t, li, n_main, tm=1024, w_transposed=True)
            w_lo = jnp.pad(w_in_t[li, n_main:, :], ((0, LANES_V7X - rank), (0, 0)))[None]
            gl = _mm(h_bf, w_lo, 0, LANES_V7X, w_transposed=True)
            wgk = jnp.pad(gla_w_gk[li], ((0, LANES_V7X - rank), (0, 0))).astype(BF16)
            y, sp_new, ss_new = _gla(p, gl, wgk, gla_b_gk[li].reshape(1, dkt),
                                     gla_norm_g[li].reshape(1, dv), state_gla[li],
                                     t_prompt=t_p, seq=seq, n_heads=n_heads, dk=dk, dv=dv)
            gla_p.append(sp_new)
            gla_s.append(ss_new)
            w_out = gla_w_out[li].astype(BF16)

        wr = jnp.concatenate([moe_w_grp[layer].T,
                              jnp.transpose(moe_w_rt[layer], (0, 2, 1)).reshape(n_e, d)], axis=0)
        wr = jnp.pad(wr, ((0, LANES_V7X - wr.shape[0]), (0, 0)))
        br = jnp.concatenate([moe_b_grp[layer], moe_b_rt[layer].reshape(n_e)])
        br = jnp.pad(br, (0, LANES_V7X - br.shape[0])).reshape(LANES_V7X, 1)
        x1, h2, eid, wt, rank_, cnt = _outln(y, w_out, x, table, layer, ln_g[layer, 0], ln_b[layer, 0], wr, br,
                                             alpha=alpha, n_groups=n_groups, n_exp=n_exp)
        pos, te, nxt, n_used = _moe_schedule(eid, rank_, cnt[:, 0], tg=tg, n_tiles=n_tiles)
        ys = _moe(h2, w1, w3, w2, layer * n_e, pos, te, nxt, n_used, tg=tg, n_tiles=n_tiles)
        has_next = layer + 1 < depth
        outs = _comb(pos, ys, wt.T, x1, table, layer, ln_g[layer, 1], ln_b[layer, 1],
                     alpha=alpha, has_next=has_next, t_prompt=t_p)
        if has_next:
            x, h_bf = outs

    y_prompt = outs[0].reshape(bp, seq, d)
    y_sample = outs[1].reshape(bs, s_len, d)
    return (y_prompt, y_sample, jnp.stack(conv_p), jnp.stack(conv_s), jnp.stack(chunk_v),
            jnp.stack(gla_p), jnp.stack(gla_s))
```

```python
import functools

import jax
import numpy as np
import jax.numpy as jnp
from jax import lax
from jax.experimental import pallas as pl
from jax.experimental.pallas import tpu as pltpu

F32 = jnp.float32
BF16 = jnp.bfloat16
I32 = jnp.int32

LN_EPS = 1e-5
RMS_EPS = 1e-6
GLA_GATE_NORMALIZER = 16.0
TOP_K_INNER = 2

SUBLANES_V7X = 8
LANES_V7X = 128
VMEM_LIMIT_V7X = 56 * 1024 * 1024

NT_DIMS = (((1,), (1,)), ((), ()))
TN_DIMS = (((0,), (0,)), ((), ()))


def _params(*sem):
    return pltpu.CompilerParams(dimension_semantics=sem, vmem_limit_bytes=VMEM_LIMIT_V7X)


def _silu(x):
    return x * (1.0 / (1.0 + jnp.exp(-x)))


def _log_sigmoid(z):
    return jnp.minimum(z, 0.0) - jnp.log(1.0 + jnp.exp(-jnp.abs(z)))


def _layer_norm(x, g, b):
    mu = jnp.mean(x, axis=-1, keepdims=True)
    xc = x - mu
    var = jnp.mean(xc * xc, axis=-1, keepdims=True)
    return xc * lax.rsqrt(var + LN_EPS) * g + b


def _dot(a, b):
    return jnp.dot(a, b, preferred_element_type=F32)


def _split_bf16(x):
    hi = x.astype(BF16)
    lo = (x - hi.astype(F32)).astype(BF16)
    return hi, lo


def _mod_kernel(c_ref, w_ref, b_ref, o_ref, *, n_prompt, n_prompt_rows, groups_per_seq):
    r = _dot(_silu(c_ref[...]).astype(BF16), w_ref[...].astype(BF16)) + b_ref[...]
    tn = o_ref.shape[-1]
    for s in range(n_prompt):
        o_ref[s * groups_per_seq:(s + 1) * groups_per_seq, :] = jnp.broadcast_to(
            r[s:s + 1, :], (groups_per_seq, tn))
    o_ref[n_prompt * groups_per_seq:, :] = r[n_prompt_rows:, :]


def _mod_table(c_prompt, c_sample, w_mod, b_mod, seq):
    depth, d, n = w_mod.shape
    bp, bs = c_prompt.shape[0], c_sample.shape[0]
    gps = seq // SUBLANES_V7X
    g_total = bp * gps + bs
    bp_rows = bp + (-bp) % SUBLANES_V7X
    c_all = jnp.concatenate([jnp.pad(c_prompt, ((0, bp_rows - bp), (0, 0))), c_sample], axis=0)
    tn = 1024
    return pl.pallas_call(
        functools.partial(_mod_kernel, n_prompt=bp, n_prompt_rows=bp_rows, groups_per_seq=gps),
        grid=(depth, n // tn),
        in_specs=[
            pl.BlockSpec(c_all.shape, lambda l, j: (0, 0)),
            pl.BlockSpec((None, d, tn), lambda l, j: (l, 0, j)),
            pl.BlockSpec((None, 1, tn), lambda l, j: (l, 0, j)),
        ],
        out_specs=pl.BlockSpec((None, g_total, tn), lambda l, j: (l, 0, j)),
        out_shape=jax.ShapeDtypeStruct((depth, g_total, n), F32),
        compiler_params=_params("arbitrary", "arbitrary"),
        name="mod",
    )(c_all, w_mod, b_mod.reshape(depth, 1, n))


def _pick_tile(i, n_prompt_tiles, p_ref, s_ref):
    return jnp.where(i < n_prompt_tiles, p_ref[...], s_ref[...])


def _mm_kernel(*refs, has_mod, n_prompt_tiles, w_transposed):
    if has_mod:
        xp_ref, xs_ref, sh_ref, sc_ref, w_ref, o_ref, wb_ref = refs
    else:
        a_ref, w_ref, o_ref, wb_ref = refs

    @pl.when(pl.program_id(1) == 0)
    def _():
        w = w_ref[...]
        wb_ref[...] = (w.T if w_transposed else w).astype(BF16)

    if has_mod:
        x = _pick_tile(pl.program_id(1), n_prompt_tiles, xp_ref, xs_ref)
        g, s, k = x.shape
        h = x * (1.0 + sc_ref[...][:, None, :]) + sh_ref[...][:, None, :]
        a = h.reshape(g * s, k).astype(BF16)
    else:
        a = a_ref[...]
    o_ref[...] = _dot(a, wb_ref[...]).astype(o_ref.dtype)


def _mm(a, w3, w_idx, n_out, *, mod=None, tm=512, tn=1024, out_dtype=BF16, w_transposed=False):
    k = w3.shape[2 if w_transposed else 1]
    tn = min(tn, n_out)
    n_p = 0
    if mod is None:
        t = a.shape[0]
        a_specs = [pl.BlockSpec((tm, k), lambda j, i: (i, 0))]
        ins = [a]
    else:
        table, layer, sh_col, sc_col = mod
        xp, xs = a
        gt = tm // SUBLANES_V7X
        n_p = xp.shape[0] // gt
        t = (xp.shape[0] + xs.shape[0]) * SUBLANES_V7X
        a_specs = [
            pl.BlockSpec((gt, SUBLANES_V7X, k), lambda j, i: (jnp.minimum(i, n_p - 1), 0, 0)),
            pl.BlockSpec((gt, SUBLANES_V7X, k), lambda j, i: (jnp.maximum(i - n_p, 0), 0, 0)),
            pl.BlockSpec((None, gt, k), lambda j, i: (layer, i, sh_col)),
            pl.BlockSpec((None, gt, k), lambda j, i: (layer, i, sc_col)),
        ]
        ins = [xp, xs, table, table]
    return pl.pallas_call(
        functools.partial(_mm_kernel, has_mod=mod is not None, n_prompt_tiles=n_p, w_transposed=w_transposed),
        grid=(n_out // tn, t // tm),
        in_specs=a_specs + [pl.BlockSpec((None, tn, k), lambda j, i: (w_idx, j, 0)) if w_transposed
                            else pl.BlockSpec((None, k, tn), lambda j, i: (w_idx, 0, j))],
        out_specs=pl.BlockSpec((tm, tn), lambda j, i: (i, j)),
        out_shape=jax.ShapeDtypeStruct((t, n_out), out_dtype),
        scratch_shapes=[pltpu.VMEM((k, tn), BF16)],
        compiler_params=_params("arbitrary", "arbitrary"),
        name="mm",
    )(*ins, w3)


def _mix0_kernel(bg_ref, cg_ref, hx_ref, u_ref, v_ref, cache_ref, cw_ref, vg_ref, vb_ref, wm_ref, bias_ref,
                 y_ref, convp_ref, convs_ref, vns_ref, zprev_ref, *, n_prompt_tiles, tiles_per_seq, n_heads):
    i = pl.program_id(0)
    tm, dc = bg_ref.shape
    ns = tm // SUBLANES_V7X
    z = cg_ref[...].astype(F32) * hx_ref[...].astype(F32)
    row = lax.broadcasted_iota(I32, (tm, dc), 0)
    r1 = pltpu.roll(z, 1, 0)
    r2 = pltpu.roll(z, 2, 0)
    cw = cw_ref[...]
    bg = bg_ref[...].astype(F32)

    vn = _layer_norm(v_ref[...].astype(F32), vg_ref[...], vb_ref[...])
    vnb = vn.astype(BF16)
    hd = dc // n_heads
    mixed = jnp.concatenate(
        [_dot(wm_ref[h], vnb[:, h * hd:(h + 1) * hd]) for h in range(n_heads)], axis=-1) + bias_ref[...]
    y_ref[:, dc:] = (u_ref[...].astype(F32) * mixed).astype(BF16)

    def conv_out(zm1, zm2):
        conv = cw[0:1, :] * zm2 + cw[1:2, :] * zm1 + cw[2:3, :] * z
        y_ref[:, :dc] = (bg * conv).astype(BF16)

    @pl.when(i < n_prompt_tiles)
    def _prompt():
        @pl.when(i % tiles_per_seq == 0)
        def _():
            zprev_ref[...] = jnp.zeros_like(zprev_ref)

        zp = zprev_ref[...]
        p1 = zp[SUBLANES_V7X - 1:SUBLANES_V7X, :]
        p2 = zp[SUBLANES_V7X - 2:SUBLANES_V7X - 1, :]
        conv_out(jnp.where(row == 0, p1, r1),
                 jnp.where(row == 0, p2, jnp.where(row == 1, p1, r2)))
        zprev_ref[...] = z[tm - SUBLANES_V7X:, :]
        convp_ref[...] = z[tm - 2:, :].reshape(1, 2, dc)

    @pl.when(i >= n_prompt_tiles)
    def _sample():
        c = cache_ref[...]
        c0 = jnp.broadcast_to(c[:, 0:1, :], (ns, SUBLANES_V7X, dc)).reshape(tm, dc)
        c1 = jnp.broadcast_to(c[:, 1:2, :], (ns, SUBLANES_V7X, dc)).reshape(tm, dc)
        rr = row % SUBLANES_V7X
        conv_out(jnp.where(rr == 0, c1, r1),
                 jnp.where(rr == 0, c0, jnp.where(rr == 1, c1, r2)))
        z3 = z.reshape(ns, SUBLANES_V7X, dc)
        convs_ref[...] = z3[:, SUBLANES_V7X - 2:, :]
        vns_ref[...] = vn.reshape(ns, SUBLANES_V7X, dc)


def _mix0(p, cache, conv_w, v_g, v_b, wm, bias, *, t_prompt, seq, n_heads):
    t, n = p.shape
    bs, cwm1, dc = cache.shape
    tm = wm.shape[-1]
    assert cwm1 == 2 and conv_w.shape[0] == 3 and n == 5 * dc and seq % tm == 0
    n_p = t_prompt // tm
    n_s = (t - t_prompt) // tm
    tps = seq // tm
    bp = t_prompt // seq
    ns = tm // SUBLANES_V7X

    def col(c):
        return pl.BlockSpec((tm, dc), lambda i: (i, c))

    def s_idx(i):
        return jnp.maximum(i - n_p, 0)

    const2 = lambda i: (0, 0)
    mode = lambda i: ((i >= n_p).astype(I32), 0, 0, 0)
    return pl.pallas_call(
        functools.partial(_mix0_kernel, n_prompt_tiles=n_p, tiles_per_seq=tps, n_heads=n_heads),
        grid=(n_p + n_s,),
        in_specs=[col(0), col(1), col(2), col(3), col(4),
                  pl.BlockSpec((ns, 2, dc), lambda i: (s_idx(i), 0, 0)),
                  pl.BlockSpec((3, dc), const2),
                  pl.BlockSpec((1, dc), const2),
                  pl.BlockSpec((1, dc), const2),
                  pl.BlockSpec((None, n_heads, tm, tm), mode),
                  pl.BlockSpec((None, tm, dc), lambda i: ((i >= n_p).astype(I32), 0, 0))],
        out_specs=[pl.BlockSpec((tm, 2 * dc), lambda i: (i, 0)),
                   pl.BlockSpec((1, 2, dc), lambda i: (jnp.minimum(i // tps, bp - 1), 0, 0)),
                   pl.BlockSpec((ns, 2, dc), lambda i: (s_idx(i), 0, 0)),
                   pl.BlockSpec((ns, SUBLANES_V7X, dc), lambda i: (s_idx(i), 0, 0))],
        out_shape=[jax.ShapeDtypeStruct((t, 2 * dc), BF16),
                   jax.ShapeDtypeStruct((bp, 2, dc), F32),
                   jax.ShapeDtypeStruct((bs, 2, dc), F32),
                   jax.ShapeDtypeStruct((bs, SUBLANES_V7X, dc), F32)],
        scratch_shapes=[pltpu.VMEM((SUBLANES_V7X, dc), F32)],
        compiler_params=_params("arbitrary"),
        name="mix0",
    )(p, p, p, p, p, cache, conv_w, v_g.reshape(1, dc), v_b.reshape(1, dc), wm, bias)


def _first_index_of(vals, target):
    idx = jnp.full(target.shape, len(vals) - 1, I32)
    for j in reversed(range(len(vals))):
        idx = jnp.where(vals[j] == target, j, idx)
    return idx


def _softmax_rows(rows):
    m = functools.reduce(jnp.maximum, rows)
    e = [jnp.exp(r - m) for r in rows]
    s = functools.reduce(lambda a, b: a + b, e)
    return [x / s for x in e]


def _route(h, wr_ref, br_ref, eid_ref, wt_ref, rank_ref, cnt_ref, carry_ref, n_groups, n_exp):
    tm = h.shape[0]
    hh, hl = _split_bf16(h)
    wh, wl = _split_bf16(wr_ref[...])
    dg = lambda a, b: lax.dot_general(a, b, NT_DIMS, preferred_element_type=F32)
    logits = dg(wh, hh) + dg(wh, hl) + dg(wl, hh) + br_ref[...]

    g_prob = _softmax_rows([logits[g:g + 1, :] for g in range(n_groups)])
    g_top = functools.reduce(jnp.maximum, g_prob)
    g_idx = _first_index_of(g_prob, g_top)

    e_sel = []
    for e in range(n_exp):
        sel = logits[n_groups + e:n_groups + e + 1, :]
        for g in range(1, n_groups):
            r = n_groups + g * n_exp + e
            sel = jnp.where(g_idx == g, logits[r:r + 1, :], sel)
        e_sel.append(sel)
    e_prob = _softmax_rows(e_sel)
    p1 = functools.reduce(jnp.maximum, e_prob)
    i1 = _first_index_of(e_prob, p1)
    rest = [jnp.where(i1 == e, -1.0, e_prob[e]) for e in range(n_exp)]
    p2 = functools.reduce(jnp.maximum, rest)
    i2 = _first_index_of(rest, p2)
    den = p1 + p2
    wt_ref[0:1, :] = g_top * (p1 / den)
    wt_ref[1:2, :] = g_top * (p2 / den)
    eid0 = g_idx * n_exp + i1
    eid1 = g_idx * n_exp + i2
    eid_ref[0:1, :] = eid0
    eid_ref[1:2, :] = eid1

    n_e = n_groups * n_exp
    eio = lax.broadcasted_iota(I32, (n_e, tm), 0)
    oh0 = (eio == eid0).astype(F32)
    oh1 = (eio == eid1).astype(F32)
    oh = oh0 + oh1
    before = (lax.broadcasted_iota(I32, (tm, tm), 0) < lax.broadcasted_iota(I32, (tm, tm), 1)).astype(BF16)
    base = _dot(oh.astype(BF16), before) + carry_ref[...]
    rank_ref[0:1, :] = jnp.sum(oh0 * base, axis=0, keepdims=True).astype(I32)
    rank_ref[1:2, :] = jnp.sum(oh1 * base, axis=0, keepdims=True).astype(I32)
    total = carry_ref[...] + jnp.sum(oh, axis=1, keepdims=True)
    carry_ref[...] = total
    cnt_ref[...] = jnp.broadcast_to(total, cnt_ref.shape).astype(I32)


def _outln_kernel(*refs, alpha, n_groups, n_exp, y_split, x_split, n_prompt_tiles):
    refs = list(refs)
    i = pl.program_id(0)
    take = lambda split: [refs.pop(0) for _ in range(2 if split else 1)]
    y_refs, (w_ref,), x_refs = take(y_split), take(False), take(x_split)
    (gt_ref, sh_ref, sc_ref, lng_ref, lnb_ref, wr_ref, br_ref,
     x1_ref, h_ref, eid_ref, wt_ref, rank_ref, cnt_ref, carry_ref) = refs

    @pl.when(i == 0)
    def _():
        carry_ref[...] = jnp.zeros_like(carry_ref)

    y = _pick_tile(i, n_prompt_tiles, *y_refs) if y_split else y_refs[0][...]
    x = _pick_tile(i, n_prompt_tiles, *x_refs) if x_split else x_refs[0][...]
    g, s, d = x.shape
    m = _dot(y, w_ref[...]).reshape(g, s, d)
    x1 = _layer_norm(alpha * x + gt_ref[...][:, None, :] * m, lng_ref[...], lnb_ref[...])
    x1_ref[...] = x1
    h = (x1 * (1.0 + sc_ref[...][:, None, :]) + sh_ref[...][:, None, :]).reshape(g * s, d)
    h_ref[...] = h
    _route(h, wr_ref, br_ref, eid_ref, wt_ref, rank_ref, cnt_ref, carry_ref, n_groups, n_exp)


def _outln(y, w_bf, x, table, layer, ln_g, ln_b, wr, br, *, alpha, n_groups, n_exp, tm=512):
    y_split, x_split = isinstance(y, tuple), isinstance(x, tuple)
    ys, xs = (y if y_split else (y,)), (x if x_split else (x,))
    k, d = w_bf.shape
    gt = tm // SUBLANES_V7X
    n_e = n_groups * n_exp
    t = sum(a.shape[0] for a in ys)
    n_p = (ys[0].shape[0] // tm) if y_split else (xs[0].shape[0] // gt if x_split else 0)

    def split_specs(block, n_arrays):
        if n_arrays == 1:
            return [pl.BlockSpec(block, lambda i: (i,) + (0,) * (len(block) - 1))]
        return [pl.BlockSpec(block, lambda i: (jnp.minimum(i, n_p - 1),) + (0,) * (len(block) - 1)),
                pl.BlockSpec(block, lambda i: (jnp.maximum(i - n_p, 0),) + (0,) * (len(block) - 1))]

    def mod(c):
        return pl.BlockSpec((None, gt, d), lambda i: (layer, i, c))

    const = lambda i: (0, 0)
    pair = lambda dt: jax.ShapeDtypeStruct((TOP_K_INNER, t), dt)
    pair_spec = pl.BlockSpec((TOP_K_INNER, tm), lambda i: (0, i))
    return pl.pallas_call(
        functools.partial(_outln_kernel, alpha=alpha, n_groups=n_groups, n_exp=n_exp,
                          y_split=y_split, x_split=x_split, n_prompt_tiles=n_p),
        grid=(t // tm,),
        in_specs=split_specs((tm, k), len(ys))
                 + [pl.BlockSpec((k, d), const, pipeline_mode=pl.Buffered(1))]
                 + split_specs((gt, SUBLANES_V7X, d), len(xs))
                 + [mod(2), mod(3), mod(4),
                  pl.BlockSpec((1, d), const),
                  pl.BlockSpec((1, d), const),
                  pl.BlockSpec(wr.shape, const),
                  pl.BlockSpec(br.shape, const)],
        out_specs=[pl.BlockSpec((gt, SUBLANES_V7X, d), lambda i: (i, 0, 0)),
                   pl.BlockSpec((tm, d), lambda i: (i, 0)),
                   pair_spec, pair_spec, pair_spec,
                   pl.BlockSpec((n_e, LANES_V7X), const)],
        out_shape=[jax.ShapeDtypeStruct((t // SUBLANES_V7X, SUBLANES_V7X, d), F32),
                   jax.ShapeDtypeStruct((t, d), F32),
                   pair(I32), pair(F32), pair(I32),
                   jax.ShapeDtypeStruct((n_e, LANES_V7X), I32)],
        scratch_shapes=[pltpu.VMEM((n_e, 1), F32)],
        compiler_params=_params("arbitrary"),
        name="outln",
    )(*ys, w_bf, *xs, table, table, table, ln_g.reshape(1, d), ln_b.reshape(1, d), wr, br)


def _row_gather_start(src_hbm, dst, sem, idx_ref, base, n):
    def body(r, c):
        row = idx_ref[base + r]
        pltpu.make_async_copy(src_hbm.at[pl.ds(row, 1)], dst.at[pl.ds(r, 1)], sem).start()
        return c
    lax.fori_loop(0, n, body, 0, unroll=8)


def _row_gather_wait(src_hbm, dst, sem, n):
    pltpu.make_async_copy(src_hbm.at[pl.ds(0, n)], dst, sem).wait()


def _dispatch_kernel(pos_ref, zs_ref, zn_ref, nu_ref, h_ref, xs_hbm, zrow_ref, ztile_ref, sem, zsem,
                     *, t_total, n_e, first_free_tile):
    i = pl.program_id(0)
    tm = h_ref.shape[0]
    tg = ztile_ref.shape[0]
    n_tiles = xs_hbm.shape[0] // tg
    n_used = nu_ref[0]

    @pl.when(i == 0)
    def _():
        zrow_ref[...] = jnp.zeros_like(zrow_ref)
        ztile_ref[...] = jnp.zeros_like(ztile_ref)
        pad_row = lambda p: pltpu.make_async_copy(zrow_ref.at[pl.ds(0, 1)], xs_hbm.at[pl.ds(p, 1)], zsem)
        tail_tile = lambda j: pltpu.make_async_copy(ztile_ref, xs_hbm.at[pl.ds(j * tg, tg)], zsem)

        def pads(do):
            for e in range(n_e):
                base = zs_ref[e]

                def body(r, c):
                    do(pad_row(base + r))
                    return c
                lax.fori_loop(0, zn_ref[e], body, 0)
            for j in range(first_free_tile, n_tiles):
                @pl.when(j >= n_used)
                def _():
                    do(tail_tile(j))

        pads(lambda cp: cp.start())
        pads(lambda cp: cp.wait())

    for k in range(TOP_K_INNER):
        def body(r, c):
            p = pos_ref[k * t_total + i * tm + r]
            pltpu.make_async_copy(h_ref.at[pl.ds(r, 1)], xs_hbm.at[pl.ds(p, 1)], sem).start()
            return c
        lax.fori_loop(0, tm, body, 0, unroll=8)
    for k in range(TOP_K_INNER):
        pltpu.make_async_copy(h_ref, xs_hbm.at[pl.ds(0, tm)], sem).wait()


def _dispatch(pos, zero_start, zero_count, n_used, h, *, n_tiles, tg, tm=512):
    t, d = h.shape
    n_e = zero_start.shape[0]
    grid_spec = pltpu.PrefetchScalarGridSpec(
        num_scalar_prefetch=4,
        grid=(t // tm,),
        in_specs=[pl.BlockSpec((tm, d), lambda i, *_: (i, 0))],
        out_specs=pl.BlockSpec(memory_space=pl.ANY),
        scratch_shapes=[pltpu.VMEM((SUBLANES_V7X, d), F32), pltpu.VMEM((tg, d), F32),
                        pltpu.SemaphoreType.DMA(()), pltpu.SemaphoreType.DMA(())],
    )
    return pl.pallas_call(
        functools.partial(_dispatch_kernel, t_total=t, n_e=n_e, first_free_tile=(TOP_K_INNER * t) // tg),
        grid_spec=grid_spec,
        out_shape=jax.ShapeDtypeStruct((n_tiles * tg, d), F32),
        compiler_params=_params("arbitrary"),
        name="dispatch",
    )(pos, zero_start, zero_count, n_used, h)


def _moe_kernel(te_ref, nxt_ref, nu_ref, xs_ref, w1_hbm, w3_hbm, w2_hbm, o_ref,
                w1s, w3s, w2s, w1b, w3b, w2b, wslot_ref, wsem, *, w_base):
    i = pl.program_id(0)
    n_used = nu_ref[0]

    def weight_copies(e, slot):
        return [pltpu.make_async_copy(hbm.at[w_base + e], stage.at[slot], wsem.at[slot])
                for hbm, stage in ((w1_hbm, w1s), (w3_hbm, w3s), (w2_hbm, w2s))]

    @pl.when(i == 0)
    def _():
        wslot_ref[0] = 1
        for cp in weight_copies(te_ref[0], 0):
            cp.start()

    @pl.when(i < n_used)
    def _():
        e = te_ref[i]

        @pl.when((i == 0) | (e != te_ref[jnp.maximum(i - 1, 0)]))
        def _():
            ws = 1 - wslot_ref[0]
            wslot_ref[0] = ws
            for cp in weight_copies(e, ws):
                cp.wait()
            w1b[...] = w1s[ws].astype(BF16)
            w3b[...] = w3s[ws].astype(BF16)
            w2b[...] = w2s[ws].astype(BF16)
            ne = nxt_ref[e]

            @pl.when(ne >= 0)
            def _():
                for cp in weight_copies(ne, 1 - ws):
                    cp.start()

        x = xs_ref[...].astype(BF16)
        a = _dot(x, w1b[...])
        b = _dot(x, w3b[...])
        o_ref[...] = _dot((_silu(a) * b).astype(BF16), w2b[...])

    @pl.when(i >= n_used)
    def _():
        o_ref[...] = jnp.zeros_like(o_ref)


def _moe(xs, w1, w3, w2, w_base, te, nxt, n_used, *, tg):
    p_tot, d = xs.shape
    f = w1.shape[-1]
    any_spec = pl.BlockSpec(memory_space=pl.ANY)
    grid_spec = pltpu.PrefetchScalarGridSpec(
        num_scalar_prefetch=3,
        grid=(p_tot // tg,),
        in_specs=[pl.BlockSpec((tg, d), lambda i, te, nxt, nu: (jnp.minimum(i, nu[0] - 1), 0)),
                  any_spec, any_spec, any_spec],
        out_specs=pl.BlockSpec((tg, d), lambda i, *_: (i, 0)),
        scratch_shapes=[pltpu.VMEM((2, d, f), F32), pltpu.VMEM((2, d, f), F32), pltpu.VMEM((2, f, d), F32),
                        pltpu.VMEM((d, f), BF16), pltpu.VMEM((d, f), BF16), pltpu.VMEM((f, d), BF16),
                        pltpu.SMEM((1,), I32), pltpu.SemaphoreType.DMA((2,))],
    )
    return pl.pallas_call(
        functools.partial(_moe_kernel, w_base=w_base),
        grid_spec=grid_spec,
        out_shape=jax.ShapeDtypeStruct((p_tot, d), F32),
        compiler_params=_params("arbitrary"),
        name="moe",
    )(te, nxt, n_used, xs, w1, w3, w2)


def _comb_kernel(pos_ref, y_hbm, wt_ref, x_ref, gt_ref, lng_ref, lnb_ref, *rest,
                 alpha, tm, t_total, has_next, n_prompt_tiles):
    if has_next:
        shn_ref, scn_ref, x2_ref, hn_ref, ybuf, sem = rest
    else:
        x2p_ref, x2s_ref, ybuf, sem = rest
    i = pl.program_id(0)
    n = pl.num_programs(0)

    def start(tile, slot):
        for k in range(TOP_K_INNER):
            _row_gather_start(y_hbm, ybuf.at[slot, k], sem.at[slot], pos_ref, k * t_total + tile * tm, tm)

    @pl.when(i == 0)
    def _():
        start(0, 0)

    slot = i % 2

    @pl.when(i + 1 < n)
    def _():
        start(i + 1, 1 - slot)

    for k in range(TOP_K_INNER):
        _row_gather_wait(y_hbm, ybuf.at[slot, k], sem.at[slot], tm)
    w = wt_ref[...]
    f = w[:, 0:1] * ybuf[slot, 0] + w[:, 1:2] * ybuf[slot, 1]
    g, s, d = x_ref.shape
    x2 = _layer_norm(alpha * x_ref[...] + gt_ref[...][:, None, :] * f.reshape(g, s, d), lng_ref[...], lnb_ref[...])
    if has_next:
        x2_ref[...] = x2
        hn = x2 * (1.0 + scn_ref[...][:, None, :]) + shn_ref[...][:, None, :]
        hn_ref[...] = hn.reshape(g * s, d).astype(BF16)
    else:
        @pl.when(i < n_prompt_tiles)
        def _():
            x2p_ref[...] = x2

        @pl.when(i >= n_prompt_tiles)
        def _():
            x2s_ref[...] = x2


def _comb(pos_flat, y_sorted, wt_t, x1, table, layer, ln_g, ln_b, *, alpha, has_next, t_prompt, tm=256):
    g_total, s, d = x1.shape
    t = g_total * s
    gt = tm // SUBLANES_V7X
    n_p = t_prompt // tm

    def mod(l, c):
        return pl.BlockSpec((None, gt, d), lambda i, pos: (l, i, c))

    xspec = pl.BlockSpec((gt, s, d), lambda i, pos: (i, 0, 0))
    vec = pl.BlockSpec((1, d), lambda i, pos: (0, 0))
    in_specs = [pl.BlockSpec(memory_space=pl.ANY),
                pl.BlockSpec((tm, TOP_K_INNER), lambda i, pos: (i, 0)),
                xspec, mod(layer, 5), vec, vec]
    ins = [y_sorted, wt_t, x1, table, ln_g.reshape(1, d), ln_b.reshape(1, d)]
    if has_next:
        in_specs += [mod(layer + 1, 0), mod(layer + 1, 1)]
        ins += [table, table]
        out_specs = [xspec, pl.BlockSpec((tm, d), lambda i, pos: (i, 0))]
        out_shape = [jax.ShapeDtypeStruct(x1.shape, F32), jax.ShapeDtypeStruct((t, d), BF16)]
    else:
        out_specs = [pl.BlockSpec((gt, s, d), lambda i, pos: (jnp.minimum(i, n_p - 1), 0, 0)),
                     pl.BlockSpec((gt, s, d), lambda i, pos: (jnp.maximum(i - n_p, 0), 0, 0))]
        out_shape = [jax.ShapeDtypeStruct((t_prompt // s, s, d), F32),
                     jax.ShapeDtypeStruct(((t - t_prompt) // s, s, d), F32)]
    grid_spec = pltpu.PrefetchScalarGridSpec(
        num_scalar_prefetch=1,
        grid=(t // tm,),
        in_specs=in_specs,
        out_specs=out_specs,
        scratch_shapes=[pltpu.VMEM((2, TOP_K_INNER, tm, d), F32), pltpu.SemaphoreType.DMA((2,))],
    )
    return pl.pallas_call(
        functools.partial(_comb_kernel, alpha=alpha, tm=tm, t_total=t, has_next=has_next, n_prompt_tiles=n_p),
        grid_spec=grid_spec,
        out_shape=out_shape,
        compiler_params=_params("arbitrary"),
        name="comb",
    )(pos_flat, *ins)


def _gla_tables(rows, span):
    t = np.arange(rows)[:, None]
    u = np.arange(rows)[None, :]
    same = (t // span) == (u // span)
    mats = [same & (u <= t), same]
    n_levels, m = 0, 1
    while m < span:
        mid = (t // (2 * m)) * (2 * m) + m - 1
        mats.append(same & np.where(t > mid, (u > mid) & (u <= t), (u > t) & (u <= mid)))
        m *= 2
        n_levels += 1
    x = t ^ u
    level = np.where(x > 0, np.floor(np.log2(np.maximum(x, 1))), n_levels).astype(np.int32)
    lid = np.where(same & (u <= t), level, -1).astype(np.int32)
    grp = (t // span) == np.arange(LANES_V7X)[None, :]
    stack = np.concatenate(mats, axis=0).astype(np.float32)
    return jnp.asarray(stack, BF16), jnp.asarray(lid), jnp.asarray(grp.astype(np.float32), BF16), n_levels


def _gla_block(q, k, la, stack, lid, grp, n_levels):
    rows = q.shape[0]
    hi, lo = _split_bf16(la)
    dall = _dot(stack, hi) + _dot(stack, lo)
    cum, total = dall[0:rows], dall[rows:2 * rows]
    group_total = (lax.dot_general(hi, grp, TN_DIMS, preferred_element_type=F32)
                   + lax.dot_general(lo, grp, TN_DIMS, preferred_element_type=F32))
    nt = lambda a, b: lax.dot_general(a.astype(BF16), b.astype(BF16), NT_DIMS, preferred_element_type=F32)
    sc = jnp.where(lid == n_levels, nt(q, k), 0.0)
    for l in range(n_levels):
        e = jnp.exp(dall[(2 + l) * rows:(3 + l) * rows])
        sc = jnp.where(lid == l, nt(q * e, k * e), sc)
    return sc.astype(BF16), cum, total, group_total


def _gla_log_decay(gl, wgk, bg_row):
    return _log_sigmoid(_dot(gl, wgk) + bg_row) * (1.0 / GLA_GATE_NORMALIZER)


def _rms_gate(o, ng, gate):
    on = o * lax.rsqrt(jnp.mean(o * o, axis=-1, keepdims=True) + RMS_EPS) * ng
    return (on * _silu(gate)).astype(BF16)


def _gla_prompt_kernel(q_ref, k_ref, v_ref, g_ref, gl_ref, wgk_ref, bgr_ref, ng_ref, stack_ref, lid_ref, grp_ref,
                       y_ref, sout_ref, st_ref, *, n_heads, scale, n_levels):
    j = pl.program_id(1)

    @pl.when(j == 0)
    def _():
        st_ref[...] = jnp.zeros_like(st_ref)

    dk = q_ref.shape[1] // n_heads
    dv = v_ref.shape[1] // n_heads
    gl = gl_ref[...]
    for h in range(n_heads):
        ks, vs = slice(h * dk, (h + 1) * dk), slice(h * dv, (h + 1) * dv)
        q = q_ref[:, ks].astype(F32) * scale
        k = k_ref[:, ks].astype(F32)
        v = v_ref[:, vs]
        la = _gla_log_decay(gl, wgk_ref[:, ks], bgr_ref[:, ks])
        sc, cum, total, group_total = _gla_block(q, k, la, stack_ref[...], lid_ref[...], grp_ref[...], n_levels)
        s_old = st_ref[h]
        o = _dot(sc, v) + _dot((q * jnp.exp(cum)).astype(BF16), s_old.astype(BF16))
        kd = (k * jnp.exp(total - cum)).astype(BF16)
        st_ref[h] = s_old * jnp.exp(group_total[:, 0:1]) + lax.dot_general(
            kd, v, TN_DIMS, preferred_element_type=F32)
        y_ref[:, vs] = _rms_gate(o, ng_ref[...], g_ref[:, vs].astype(F32))

    @pl.when(j == pl.num_programs(1) - 1)
    def _():
        sout_ref[...] = st_ref[...]


def _gla_sample_kernel(q_ref, k_ref, v_ref, g_ref, gl_ref, wgk_ref, bgr_ref, ng_ref, stack_ref, lid_ref, grp_ref,
                       sin_ref, y_ref, sout_ref, *, n_heads, scale, seq, n_levels):
    rows = q_ref.shape[0]
    nb = rows // seq
    dk = q_ref.shape[1] // n_heads
    dv = v_ref.shape[1] // n_heads
    gl = gl_ref[...]
    for h in range(n_heads):
        ks, vs = slice(h * dk, (h + 1) * dk), slice(h * dv, (h + 1) * dv)
        q = q_ref[:, ks].astype(F32) * scale
        k = k_ref[:, ks].astype(F32)
        v = v_ref[:, vs]
        la = _gla_log_decay(gl, wgk_ref[:, ks], bgr_ref[:, ks])
        sc, cum, total, group_total = _gla_block(q, k, la, stack_ref[...], lid_ref[...], grp_ref[...], n_levels)
        q_dec = q * jnp.exp(cum)
        kd = k * jnp.exp(total - cum)
        v32 = v.astype(F32)
        o_state = []
        for s in range(nb):
            rs = slice(s * seq, (s + 1) * seq)
            s_old = sin_ref[s, h]
            o_state.append(_dot(q_dec[rs, :].astype(BF16), s_old.astype(BF16)))
            sout_ref[s, h] = s_old * jnp.exp(group_total[:, s:s + 1]) + lax.dot_general(
                kd[rs, :].astype(BF16), v32[rs, :].astype(BF16), TN_DIMS, preferred_element_type=F32)
        o = _dot(sc, v) + jnp.concatenate(o_state, axis=0)
        y_ref[:, vs] = _rms_gate(o, ng_ref[...], g_ref[:, vs].astype(F32))


def _gla(p, gl, wgk, bg_row, ng, state_s, *, t_prompt, seq, n_heads, dk, dv, bk=128, nb=4):
    t = p.shape[0]
    bs, _, _, _ = state_s.shape
    s_len = (t - t_prompt) // bs
    bp = t_prompt // seq
    dkt, dvt = n_heads * dk, n_heads * dv
    assert dvt == 2 * dkt
    scale = dk ** -0.5
    r = gl.shape[1]
    const = lambda *_: (0, 0)

    def tables(rows, span):
        stack, lid, grp, n_levels = _gla_tables(rows, span)
        specs = [pl.BlockSpec((r, dkt), const), pl.BlockSpec((1, dkt), const), pl.BlockSpec((1, dv), const),
                 pl.BlockSpec(stack.shape, const), pl.BlockSpec(lid.shape, const), pl.BlockSpec(grp.shape, const)]
        return specs, [wgk, bg_row, ng, stack, lid, grp], n_levels

    nblk = seq // bk
    rowp = lambda b, j: b * nblk + j
    w_specs, w_ins, n_levels = tables(bk, bk)
    y_p, s_p = pl.pallas_call(
        functools.partial(_gla_prompt_kernel, n_heads=n_heads, scale=scale, n_levels=n_levels),
        grid=(bp, nblk),
        in_specs=[pl.BlockSpec((bk, dkt), lambda b, j: (rowp(b, j), 0)),
                  pl.BlockSpec((bk, dkt), lambda b, j: (rowp(b, j), 1)),
                  pl.BlockSpec((bk, dvt), lambda b, j: (rowp(b, j), 1)),
                  pl.BlockSpec((bk, dvt), lambda b, j: (rowp(b, j), 2)),
                  pl.BlockSpec((bk, r), lambda b, j: (rowp(b, j), 0))] + w_specs,
        out_specs=[pl.BlockSpec((bk, dvt), lambda b, j: (rowp(b, j), 0)),
                   pl.BlockSpec((None, n_heads, dk, dv), lambda b, j: (b, 0, 0, 0))],
        out_shape=[jax.ShapeDtypeStruct((t_prompt, dvt), BF16),
                   jax.ShapeDtypeStruct((bp, n_heads, dk, dv), F32)],
        scratch_shapes=[pltpu.VMEM((n_heads, dk, dv), F32)],
        compiler_params=_params("arbitrary", "arbitrary"),
        name="gla_prompt",
    )(p, p, p, p, gl, *w_ins)

    rows = nb * s_len
    off = t_prompt // rows
    w_specs, w_ins, n_levels = tables(rows, s_len)
    y_s, s_s = pl.pallas_call(
        functools.partial(_gla_sample_kernel, n_heads=n_heads, scale=scale, seq=s_len, n_levels=n_levels),
        grid=(bs // nb,),
        in_specs=[pl.BlockSpec((rows, dkt), lambda i: (off + i, 0)),
                  pl.BlockSpec((rows, dkt), lambda i: (off + i, 1)),
                  pl.BlockSpec((rows, dvt), lambda i: (off + i, 1)),
                  pl.BlockSpec((rows, dvt), lambda i: (off + i, 2)),
                  pl.BlockSpec((rows, r), lambda i: (off + i, 0))] + w_specs + [
                  pl.BlockSpec((nb, n_heads, dk, dv), lambda i: (i, 0, 0, 0))],
        out_specs=[pl.BlockSpec((rows, dvt), lambda i: (i, 0)),
                   pl.BlockSpec((nb, n_heads, dk, dv), lambda i: (i, 0, 0, 0))],
        out_shape=[jax.ShapeDtypeStruct((t - t_prompt, dvt), BF16),
                   jax.ShapeDtypeStruct(state_s.shape, F32)],
        compiler_params=_params("arbitrary"),
        name="gla_sample",
    )(p, p, p, p, gl, *w_ins, state_s)
    return (y_p, y_s), s_p, s_s


def _moe_schedule(eid, rank, counts, *, tg, n_tiles):
    n_e = counts.shape[0]
    e_ids = jnp.arange(n_e, dtype=I32)
    padded = ((counts + tg - 1) // tg) * tg
    ends = jnp.sum(jnp.where(e_ids[None, :] <= e_ids[:, None], padded[None, :], 0), axis=1)
    starts = ends - padded
    pos = jnp.sum(jnp.where(eid[None] == e_ids[:, None, None], starts[:, None, None], 0), axis=0) + rank
    n_used = ends[n_e - 1] // tg
    tile_start = jnp.arange(n_tiles, dtype=I32) * tg
    te = jnp.sum((ends[None, :] <= tile_start[:, None]).astype(I32), axis=1)
    te_last = jnp.sum((ends <= (n_used - 1) * tg).astype(I32))
    te = jnp.where(jnp.arange(n_tiles) < n_used, te, te_last)
    later = (e_ids[None, :] > e_ids[:, None]) & (counts[None, :] > 0)
    nxt = jnp.min(jnp.where(later, e_ids[None, :], n_e), axis=1)
    nxt = jnp.where(nxt == n_e, -1, nxt)
    i32 = lambda a: a.astype(I32)
    return (i32(pos.reshape(-1)), i32(starts + counts), i32(padded - counts), i32(te), i32(nxt),
            i32(n_used.reshape(1)))


def kernel(x_prompt, x_sample, cache_conv, state_gla, c_prompt, c_sample, w_mod, b_mod, ln_g, ln_b, ab_w_in, ab_conv_w, ab_v_ln_g, ab_v_ln_b, ab_w_s, ab_b_s, ab_w_out, gla_w_in, gla_w_gk, gla_b_gk, gla_norm_g, gla_w_out, moe_w_grp, moe_b_grp, moe_w_rt, moe_b_rt, moe_w1, moe_w3, moe_w2):
    bp, seq, d = x_prompt.shape
    bs, s_len, _ = x_sample.shape
    assert s_len == SUBLANES_V7X and seq % SUBLANES_V7X == 0
    depth = w_mod.shape[0]
    alpha = float((2 * depth) ** 0.25)
    t_p, t_s = bp * seq, bs * s_len
    t = t_p + t_s
    n_groups, n_exp = moe_w_rt.shape[1], moe_w_rt.shape[3]
    n_e = n_groups * n_exp
    d_ff = moe_w1.shape[-1]
    tg = 256
    n_tiles = (TOP_K_INNER * t) // tg + n_e

    x = (x_prompt.reshape(t_p // SUBLANES_V7X, SUBLANES_V7X, d), x_sample)
    table = _mod_table(c_prompt, c_sample, w_mod, b_mod, seq)

    w1 = moe_w1.reshape(depth * n_e, d, d_ff)
    w3 = moe_w3.reshape(depth * n_e, d, d_ff)
    w2 = moe_w2.reshape(depth * n_e, d_ff, d)

    conv_p, conv_s, chunk_v, gla_p, gla_s = [], [], [], [], []
    h_bf = None
    for layer in range(depth):
        li = layer // 2
        if layer % 2 == 0:
            n_heads, chunk = ab_w_s.shape[1], ab_w_s.shape[2]
            dc = ab_conv_w.shape[-1]
            if h_bf is None:
                p = _mm(x, ab_w_in, li, ab_w_in.shape[-1], mod=(table, layer, 0, 1))
            else:
                p = _mm(h_bf, ab_w_in, li, ab_w_in.shape[-1])
            w_s = ab_w_s[li]
            wm_p = jnp.tril(w_s)
            reps = chunk // s_len
            blk = jnp.tril(w_s[:, :s_len, :s_len])
            wm_s = jnp.einsum("ab,hts->hatbs", jnp.eye(reps, dtype=F32), blk).reshape(n_heads, chunk, chunk)
            wm = jnp.stack([wm_p, wm_s]).astype(BF16)
            b_s = ab_b_s[li]
            hd = dc // n_heads
            bias_p = jnp.repeat(b_s.T, hd, axis=1)
            bias_s = jnp.repeat(jnp.tile(b_s[:, :s_len].T, (reps, 1)), hd, axis=1)
            bias = jnp.stack([bias_p, bias_s])
            y, cp_new, cs_new, vn_s = _mix0(p, cache_conv[li], ab_conv_w[li], ab_v_ln_g[li], ab_v_ln_b[li],
                                            wm, bias, t_prompt=t_p, seq=seq, n_heads=n_heads)
            conv_p.append(cp_new)
            conv_s.append(cs_new)
            chunk_v.append(vn_s)
            w_out = ab_w_out[li].astype(BF16)
        else:
            n_heads, dk, dv = state_gla.shape[2], state_gla.shape[3], state_gla.shape[4]
            dkt, dvt = n_heads * dk, n_heads * dv
            rank = gla_w_gk.shape[1]
            n_main = 2 * dkt + 2 * dvt
            w_in_t = jnp.swapaxes(gla_w_in, 1, 2)
            p = _mm(h_bf, w_in_t, li, n_main, tm=1024, w_transposed=True)
            w_lo = jnp.pad(w_in_t[li, n_main:, :], ((0, LANES_V7X - rank), (0, 0)))[None]
            gl = _mm(h_bf, w_lo, 0, LANES_V7X, w_transposed=True)
            wgk = jnp.pad(gla_w_gk[li], ((0, LANES_V7X - rank), (0, 0))).astype(BF16)
            y, sp_new, ss_new = _gla(p, gl, wgk, gla_b_gk[li].reshape(1, dkt),
                                     gla_norm_g[li].reshape(1, dv), state_gla[li],
                                     t_prompt=t_p, seq=seq, n_heads=n_heads, dk=dk, dv=dv)
            gla_p.append(sp_new)
            gla_s.append(ss_new)
            w_out = gla_w_out[li].astype(BF16)

        wr = jnp.concatenate([moe_w_grp[layer].T,
                              jnp.transpose(moe_w_rt[layer], (0, 2, 1)).reshape(n_e, d)], axis=0)
        wr = jnp.pad(wr, ((0, LANES_V7X - wr.shape[0]), (0, 0)))
        br = jnp.concatenate([moe_b_grp[layer], moe_b_rt[layer].reshape(n_e)])
        br = jnp.pad(br, (0, LANES_V7X - br.shape[0])).reshape(LANES_V7X, 1)
        x1, h2, eid, wt, rank_, cnt = _outln(y, w_out, x, table, layer, ln_g[layer, 0], ln_b[layer, 0], wr, br,
                                             alpha=alpha, n_groups=n_groups, n_exp=n_exp)
        pos, zero_start, zero_count, te, nxt, n_used = _moe_schedule(eid, rank_, cnt[:, 0], tg=tg, n_tiles=n_tiles)
        xs = _dispatch(pos, zero_start, zero_count, n_used, h2, n_tiles=n_tiles, tg=tg)
        ys = _moe(xs, w1, w3, w2, layer * n_e, te, nxt, n_used, tg=tg)
        has_next = layer + 1 < depth
        outs = _comb(pos, ys, wt.T, x1, table, layer, ln_g[layer, 1], ln_b[layer, 1],
                     alpha=alpha, has_next=has_next, t_prompt=t_p)
        if has_next:
            x, h_bf = outs

    y_prompt = outs[0].reshape(bp, seq, d)
    y_sample = outs[1].reshape(bs, s_len, d)
    return (y_prompt, y_sample, jnp.stack(conv_p), jnp.stack(conv_s), jnp.stack(chunk_v),
            jnp.stack(gla_p), jnp.stack(gla_s))
```

```python
import functools

import jax
import numpy as np
import jax.numpy as jnp
from jax import lax
from jax.experimental import pallas as pl
from jax.experimental.pallas import tpu as pltpu

F32 = jnp.float32
BF16 = jnp.bfloat16
I32 = jnp.int32

LN_EPS = 1e-5
RMS_EPS = 1e-6
GLA_GATE_NORMALIZER = 16.0
TOP_K_INNER = 2

SUBLANES_V7X = 8
LANES_V7X = 128
VMEM_LIMIT_V7X = 56 * 1024 * 1024

NT_DIMS = (((1,), (1,)), ((), ()))
TN_DIMS = (((0,), (0,)), ((), ()))


def _params(*sem):
    return pltpu.CompilerParams(dimension_semantics=sem, vmem_limit_bytes=VMEM_LIMIT_V7X)


def _silu(x):
    return x * (1.0 / (1.0 + jnp.exp(-x)))


def _log_sigmoid(z):
    return jnp.minimum(z, 0.0) - jnp.log(1.0 + jnp.exp(-jnp.abs(z)))


def _layer_norm(x, g, b):
    mu = jnp.mean(x, axis=-1, keepdims=True)
    xc = x - mu
    var = jnp.mean(xc * xc, axis=-1, keepdims=True)
    return xc * lax.rsqrt(var + LN_EPS) * g + b


def _dot(a, b):
    return jnp.dot(a, b, preferred_element_type=F32)


U32 = jnp.uint32
HI_HALF = np.uint32(0xFFFF0000)


def _pack_rows(x):
    half = x.shape[1] // 2
    bits = lambda v: lax.bitcast_convert_type(v.astype(BF16).astype(F32), U32)
    return (bits(x[:, :half]) >> 16) | (bits(x[:, half:]) & HI_HALF)


def _unpack_rows(w):
    return lax.bitcast_convert_type(w << 16, F32), lax.bitcast_convert_type(w & HI_HALF, F32)


def _split_bf16(x):
    hi = x.astype(BF16)
    lo = (x - hi.astype(F32)).astype(BF16)
    return hi, lo


def _mod_kernel(c_ref, w_ref, b_ref, o_ref, *, n_prompt, n_prompt_rows, groups_per_seq):
    r = _dot(_silu(c_ref[...]).astype(BF16), w_ref[...].astype(BF16)) + b_ref[...]
    tn = o_ref.shape[-1]
    for s in range(n_prompt):
        o_ref[s * groups_per_seq:(s + 1) * groups_per_seq, :] = jnp.broadcast_to(
            r[s:s + 1, :], (groups_per_seq, tn))
    o_ref[n_prompt * groups_per_seq:, :] = r[n_prompt_rows:, :]


def _mod_table(c_prompt, c_sample, w_mod, b_mod, seq):
    depth, d, n = w_mod.shape
    bp, bs = c_prompt.shape[0], c_sample.shape[0]
    gps = seq // SUBLANES_V7X
    g_total = bp * gps + bs
    bp_rows = bp + (-bp) % SUBLANES_V7X
    c_all = jnp.concatenate([jnp.pad(c_prompt, ((0, bp_rows - bp), (0, 0))), c_sample], axis=0)
    tn = 1024
    return pl.pallas_call(
        functools.partial(_mod_kernel, n_prompt=bp, n_prompt_rows=bp_rows, groups_per_seq=gps),
        grid=(depth, n // tn),
        in_specs=[
            pl.BlockSpec(c_all.shape, lambda l, j: (0, 0)),
            pl.BlockSpec((None, d, tn), lambda l, j: (l, 0, j)),
            pl.BlockSpec((None, 1, tn), lambda l, j: (l, 0, j)),
        ],
        out_specs=pl.BlockSpec((None, g_total, tn), lambda l, j: (l, 0, j)),
        out_shape=jax.ShapeDtypeStruct((depth, g_total, n), F32),
        compiler_params=_params("arbitrary", "arbitrary"),
        name="mod",
    )(c_all, w_mod, b_mod.reshape(depth, 1, n))


def _pick_tile(i, n_prompt_tiles, p_ref, s_ref):
    return jnp.where(i < n_prompt_tiles, p_ref[...], s_ref[...])


def _mm_kernel(*refs, has_mod, n_prompt_tiles, w_transposed):
    if has_mod:
        xp_ref, xs_ref, sh_ref, sc_ref, w_ref, o_ref, wb_ref = refs
    else:
        a_ref, w_ref, o_ref, wb_ref = refs

    @pl.when(pl.program_id(1) == 0)
    def _():
        w = w_ref[...]
        wb_ref[...] = (w.T if w_transposed else w).astype(BF16)

    if has_mod:
        x = _pick_tile(pl.program_id(1), n_prompt_tiles, xp_ref, xs_ref)
        g, s, k = x.shape
        h = x * (1.0 + sc_ref[...][:, None, :]) + sh_ref[...][:, None, :]
        a = h.reshape(g * s, k).astype(BF16)
    else:
        a = a_ref[...]
    o_ref[...] = _dot(a, wb_ref[...]).astype(o_ref.dtype)


def _mm(a, w3, w_idx, n_out, *, mod=None, tm=512, tn=1024, out_dtype=BF16, w_transposed=False):
    k = w3.shape[2 if w_transposed else 1]
    tn = min(tn, n_out)
    n_p = 0
    if mod is None:
        t = a.shape[0]
        a_specs = [pl.BlockSpec((tm, k), lambda j, i: (i, 0))]
        ins = [a]
    else:
        table, layer, sh_col, sc_col = mod
        xp, xs = a
        gt = tm // SUBLANES_V7X
        n_p = xp.shape[0] // gt
        t = (xp.shape[0] + xs.shape[0]) * SUBLANES_V7X
        a_specs = [
            pl.BlockSpec((gt, SUBLANES_V7X, k), lambda j, i: (jnp.minimum(i, n_p - 1), 0, 0)),
            pl.BlockSpec((gt, SUBLANES_V7X, k), lambda j, i: (jnp.maximum(i - n_p, 0), 0, 0)),
            pl.BlockSpec((None, gt, k), lambda j, i: (layer, i, sh_col)),
            pl.BlockSpec((None, gt, k), lambda j, i: (layer, i, sc_col)),
        ]
        ins = [xp, xs, table, table]
    return pl.pallas_call(
        functools.partial(_mm_kernel, has_mod=mod is not None, n_prompt_tiles=n_p, w_transposed=w_transposed),
        grid=(n_out // tn, t // tm),
        in_specs=a_specs + [pl.BlockSpec((None, tn, k), lambda j, i: (w_idx, j, 0)) if w_transposed
                            else pl.BlockSpec((None, k, tn), lambda j, i: (w_idx, 0, j))],
        out_specs=pl.BlockSpec((tm, tn), lambda j, i: (i, j)),
        out_shape=jax.ShapeDtypeStruct((t, n_out), out_dtype),
        scratch_shapes=[pltpu.VMEM((k, tn), BF16)],
        compiler_params=_params("arbitrary", "arbitrary"),
        name="mm",
    )(*ins, w3)


def _modulate_kernel(xp_ref, xs_ref, sh_ref, sc_ref, o_ref, *, n_prompt_tiles):
    x = _pick_tile(pl.program_id(0), n_prompt_tiles, xp_ref, xs_ref)
    g, s, k = x.shape
    h = x * (1.0 + sc_ref[...][:, None, :]) + sh_ref[...][:, None, :]
    o_ref[...] = h.reshape(g * s, k).astype(BF16)


def _modulate(x_pair, table, layer, sh_col, sc_col, *, tm=512):
    xp, xs = x_pair
    k = xp.shape[-1]
    gt = tm // SUBLANES_V7X
    n_p = xp.shape[0] // gt
    t = (xp.shape[0] + xs.shape[0]) * SUBLANES_V7X
    return pl.pallas_call(
        functools.partial(_modulate_kernel, n_prompt_tiles=n_p),
        grid=(t // tm,),
        in_specs=[pl.BlockSpec((gt, SUBLANES_V7X, k), lambda i: (jnp.minimum(i, n_p - 1), 0, 0)),
                  pl.BlockSpec((gt, SUBLANES_V7X, k), lambda i: (jnp.maximum(i - n_p, 0), 0, 0)),
                  pl.BlockSpec((None, gt, k), lambda i: (layer, i, sh_col)),
                  pl.BlockSpec((None, gt, k), lambda i: (layer, i, sc_col))],
        out_specs=pl.BlockSpec((tm, k), lambda i: (i, 0)),
        out_shape=jax.ShapeDtypeStruct((t, k), BF16),
        compiler_params=_params("arbitrary"),
        name="modulate",
    )(xp, xs, table, table)


def _mix0_kernel(bg_ref, cg_ref, hx_ref, u_ref, v_ref, cache_ref, cw_ref, vg_ref, vb_ref, wm_ref, bias_ref,
                 y_ref, convp_ref, convs_ref, vns_ref, zprev_ref, *, n_prompt_tiles, tiles_per_seq, n_heads):
    i = pl.program_id(0)
    tm, dc = bg_ref.shape
    ns = tm // SUBLANES_V7X
    z = cg_ref[...].astype(F32) * hx_ref[...].astype(F32)
    row = lax.broadcasted_iota(I32, (tm, dc), 0)
    r1 = pltpu.roll(z, 1, 0)
    r2 = pltpu.roll(z, 2, 0)
    cw = cw_ref[...]
    bg = bg_ref[...].astype(F32)

    vn = _layer_norm(v_ref[...].astype(F32), vg_ref[...], vb_ref[...])
    vnb = vn.astype(BF16)
    hd = dc // n_heads
    mixed = jnp.concatenate(
        [_dot(wm_ref[h], vnb[:, h * hd:(h + 1) * hd]) for h in range(n_heads)], axis=-1) + bias_ref[...]
    y_ref[:, dc:] = (u_ref[...].astype(F32) * mixed).astype(BF16)

    def conv_out(zm1, zm2):
        conv = cw[0:1, :] * zm2 + cw[1:2, :] * zm1 + cw[2:3, :] * z
        y_ref[:, :dc] = (bg * conv).astype(BF16)

    @pl.when(i < n_prompt_tiles)
    def _prompt():
        @pl.when(i % tiles_per_seq == 0)
        def _():
            zprev_ref[...] = jnp.zeros_like(zprev_ref)

        zp = zprev_ref[...]
        p1 = zp[SUBLANES_V7X - 1:SUBLANES_V7X, :]
        p2 = zp[SUBLANES_V7X - 2:SUBLANES_V7X - 1, :]
        conv_out(jnp.where(row == 0, p1, r1),
                 jnp.where(row == 0, p2, jnp.where(row == 1, p1, r2)))
        zprev_ref[...] = z[tm - SUBLANES_V7X:, :]
        convp_ref[...] = z[tm - 2:, :].reshape(1, 2, dc)

    @pl.when(i >= n_prompt_tiles)
    def _sample():
        c = cache_ref[...]
        c0 = jnp.broadcast_to(c[:, 0:1, :], (ns, SUBLANES_V7X, dc)).reshape(tm, dc)
        c1 = jnp.broadcast_to(c[:, 1:2, :], (ns, SUBLANES_V7X, dc)).reshape(tm, dc)
        rr = row % SUBLANES_V7X
        conv_out(jnp.where(rr == 0, c1, r1),
                 jnp.where(rr == 0, c0, jnp.where(rr == 1, c1, r2)))
        z3 = z.reshape(ns, SUBLANES_V7X, dc)
        convs_ref[...] = z3[:, SUBLANES_V7X - 2:, :]
        vns_ref[...] = vn.reshape(ns, SUBLANES_V7X, dc)


def _mix0(p, cache, conv_w, v_g, v_b, wm, bias, *, t_prompt, seq, n_heads):
    t, n = p.shape
    bs, cwm1, dc = cache.shape
    tm = wm.shape[-1]
    assert cwm1 == 2 and conv_w.shape[0] == 3 and n == 5 * dc and seq % tm == 0
    n_p = t_prompt // tm
    n_s = (t - t_prompt) // tm
    tps = seq // tm
    bp = t_prompt // seq
    ns = tm // SUBLANES_V7X

    def col(c):
        return pl.BlockSpec((tm, dc), lambda i: (i, c))

    def s_idx(i):
        return jnp.maximum(i - n_p, 0)

    const2 = lambda i: (0, 0)
    mode = lambda i: ((i >= n_p).astype(I32), 0, 0, 0)
    return pl.pallas_call(
        functools.partial(_mix0_kernel, n_prompt_tiles=n_p, tiles_per_seq=tps, n_heads=n_heads),
        grid=(n_p + n_s,),
        in_specs=[col(0), col(1), col(2), col(3), col(4),
                  pl.BlockSpec((ns, 2, dc), lambda i: (s_idx(i), 0, 0)),
                  pl.BlockSpec((3, dc), const2),
                  pl.BlockSpec((1, dc), const2),
                  pl.BlockSpec((1, dc), const2),
                  pl.BlockSpec((None, n_heads, tm, tm), mode),
                  pl.BlockSpec((None, tm, dc), lambda i: ((i >= n_p).astype(I32), 0, 0))],
        out_specs=[pl.BlockSpec((tm, 2 * dc), lambda i: (i, 0)),
                   pl.BlockSpec((1, 2, dc), lambda i: (jnp.minimum(i // tps, bp - 1), 0, 0)),
                   pl.BlockSpec((ns, 2, dc), lambda i: (s_idx(i), 0, 0)),
                   pl.BlockSpec((ns, SUBLANES_V7X, dc), lambda i: (s_idx(i), 0, 0))],
        out_shape=[jax.ShapeDtypeStruct((t, 2 * dc), BF16),
                   jax.ShapeDtypeStruct((bp, 2, dc), F32),
                   jax.ShapeDtypeStruct((bs, 2, dc), F32),
                   jax.ShapeDtypeStruct((bs, SUBLANES_V7X, dc), F32)],
        scratch_shapes=[pltpu.VMEM((SUBLANES_V7X, dc), F32)],
        compiler_params=_params("arbitrary"),
        name="mix0",
    )(p, p, p, p, p, cache, conv_w, v_g.reshape(1, dc), v_b.reshape(1, dc), wm, bias)


def _first_index_of(vals, target):
    idx = jnp.full(target.shape, len(vals) - 1, I32)
    for j in reversed(range(len(vals))):
        idx = jnp.where(vals[j] == target, j, idx)
    return idx


def _softmax_rows(rows):
    m = functools.reduce(jnp.maximum, rows)
    e = [jnp.exp(r - m) for r in rows]
    s = functools.reduce(lambda a, b: a + b, e)
    return [x / s for x in e]


def _route(h, wr_ref, br_ref, eid_ref, wt_ref, rank_ref, cnt_ref, carry_ref, n_groups, n_exp):
    tm = h.shape[0]
    hh, hl = _split_bf16(h)
    wh, wl = _split_bf16(wr_ref[...])
    dg = lambda a, b: lax.dot_general(a, b, NT_DIMS, preferred_element_type=F32)
    logits = dg(wh, hh) + dg(wh, hl) + dg(wl, hh) + br_ref[...]

    g_prob = _softmax_rows([logits[g:g + 1, :] for g in range(n_groups)])
    g_top = functools.reduce(jnp.maximum, g_prob)
    g_idx = _first_index_of(g_prob, g_top)

    e_sel = []
    for e in range(n_exp):
        sel = logits[n_groups + e:n_groups + e + 1, :]
        for g in range(1, n_groups):
            r = n_groups + g * n_exp + e
            sel = jnp.where(g_idx == g, logits[r:r + 1, :], sel)
        e_sel.append(sel)
    e_prob = _softmax_rows(e_sel)
    p1 = functools.reduce(jnp.maximum, e_prob)
    i1 = _first_index_of(e_prob, p1)
    rest = [jnp.where(i1 == e, -1.0, e_prob[e]) for e in range(n_exp)]
    p2 = functools.reduce(jnp.maximum, rest)
    i2 = _first_index_of(rest, p2)
    den = p1 + p2
    wt_ref[0:1, :] = g_top * (p1 / den)
    wt_ref[1:2, :] = g_top * (p2 / den)
    eid0 = g_idx * n_exp + i1
    eid1 = g_idx * n_exp + i2
    eid_ref[0:1, :] = eid0
    eid_ref[1:2, :] = eid1

    n_e = n_groups * n_exp
    eio = lax.broadcasted_iota(I32, (n_e, tm), 0)
    oh0 = (eio == eid0).astype(F32)
    oh1 = (eio == eid1).astype(F32)
    oh = oh0 + oh1
    before = (lax.broadcasted_iota(I32, (tm, tm), 0) < lax.broadcasted_iota(I32, (tm, tm), 1)).astype(BF16)
    base = _dot(oh.astype(BF16), before) + carry_ref[...]
    rank_ref[0:1, :] = jnp.sum(oh0 * base, axis=0, keepdims=True).astype(I32)
    rank_ref[1:2, :] = jnp.sum(oh1 * base, axis=0, keepdims=True).astype(I32)
    total = carry_ref[...] + jnp.sum(oh, axis=1, keepdims=True)
    carry_ref[...] = total
    cnt_ref[...] = jnp.broadcast_to(total, cnt_ref.shape).astype(I32)


def _outln_kernel(*refs, alpha, n_groups, n_exp, y_split, x_split, n_prompt_tiles):
    refs = list(refs)
    i = pl.program_id(0)
    take = lambda split: [refs.pop(0) for _ in range(2 if split else 1)]
    y_refs, (w_ref,), x_refs = take(y_split), take(False), take(x_split)
    (gt_ref, sh_ref, sc_ref, lng_ref, lnb_ref, wr_ref, br_ref,
     x1_ref, h_ref, eid_ref, wt_ref, rank_ref, cnt_ref, carry_ref) = refs

    @pl.when(i == 0)
    def _():
        carry_ref[...] = jnp.zeros_like(carry_ref)

    y = _pick_tile(i, n_prompt_tiles, *y_refs) if y_split else y_refs[0][...]
    x = _pick_tile(i, n_prompt_tiles, *x_refs) if x_split else x_refs[0][...]
    g, s, d = x.shape
    m = _dot(y, w_ref[...]).reshape(g, s, d)
    x1 = _layer_norm(alpha * x + gt_ref[...][:, None, :] * m, lng_ref[...], lnb_ref[...])
    x1_ref[...] = x1
    h = (x1 * (1.0 + sc_ref[...][:, None, :]) + sh_ref[...][:, None, :]).reshape(g * s, d)
    h_ref[...] = _pack_rows(h)
    _route(h, wr_ref, br_ref, eid_ref, wt_ref, rank_ref, cnt_ref, carry_ref, n_groups, n_exp)


def _outln(y, w_bf, x, table, layer, ln_g, ln_b, wr, br, *, alpha, n_groups, n_exp, tm=512):
    y_split, x_split = isinstance(y, tuple), isinstance(x, tuple)
    ys, xs = (y if y_split else (y,)), (x if x_split else (x,))
    k, d = w_bf.shape
    gt = tm // SUBLANES_V7X
    n_e = n_groups * n_exp
    t = sum(a.shape[0] for a in ys)
    n_p = (ys[0].shape[0] // tm) if y_split else (xs[0].shape[0] // gt if x_split else 0)

    def split_specs(block, n_arrays):
        if n_arrays == 1:
            return [pl.BlockSpec(block, lambda i: (i,) + (0,) * (len(block) - 1))]
        return [pl.BlockSpec(block, lambda i: (jnp.minimum(i, n_p - 1),) + (0,) * (len(block) - 1)),
                pl.BlockSpec(block, lambda i: (jnp.maximum(i - n_p, 0),) + (0,) * (len(block) - 1))]

    def mod(c):
        return pl.BlockSpec((None, gt, d), lambda i: (layer, i, c))

    const = lambda i: (0, 0)
    pair = lambda dt: jax.ShapeDtypeStruct((TOP_K_INNER, t), dt)
    pair_spec = pl.BlockSpec((TOP_K_INNER, tm), lambda i: (0, i))
    return pl.pallas_call(
        functools.partial(_outln_kernel, alpha=alpha, n_groups=n_groups, n_exp=n_exp,
                          y_split=y_split, x_split=x_split, n_prompt_tiles=n_p),
        grid=(t // tm,),
        in_specs=split_specs((tm, k), len(ys))
                 + [pl.BlockSpec((k, d), const, pipeline_mode=pl.Buffered(1))]
                 + split_specs((gt, SUBLANES_V7X, d), len(xs))
                 + [mod(2), mod(3), mod(4),
                  pl.BlockSpec((1, d), const),
                  pl.BlockSpec((1, d), const),
                  pl.BlockSpec(wr.shape, const),
                  pl.BlockSpec(br.shape, const)],
        out_specs=[pl.BlockSpec((gt, SUBLANES_V7X, d), lambda i: (i, 0, 0)),
                   pl.BlockSpec((tm, d // 2), lambda i: (i, 0)),
                   pair_spec, pair_spec, pair_spec,
                   pl.BlockSpec((n_e, LANES_V7X), const)],
        out_shape=[jax.ShapeDtypeStruct((t // SUBLANES_V7X, SUBLANES_V7X, d), F32),
                   jax.ShapeDtypeStruct((t, d // 2), U32),
                   pair(I32), pair(F32), pair(I32),
                   jax.ShapeDtypeStruct((n_e, LANES_V7X), I32)],
        scratch_shapes=[pltpu.VMEM((n_e, 1), F32)],
        compiler_params=_params("arbitrary"),
        name="outln",
    )(*ys, w_bf, *xs, table, table, table, ln_g.reshape(1, d), ln_b.reshape(1, d), wr, br)


ROW_DMA_UNROLL = 8


def _row_gather_start(src_hbm, dst, sem, idx_ref, base, n):
    def body(q, c):
        for u in range(ROW_DMA_UNROLL):
            r = q * ROW_DMA_UNROLL + u
            row = idx_ref[base + r]
            pltpu.make_async_copy(src_hbm.at[pl.ds(row, 1)], dst.at[pl.ds(r, 1)], sem).start(priority=u % 2)
        return c
    lax.fori_loop(0, n // ROW_DMA_UNROLL, body, 0)


def _row_gather_wait(src_hbm, dst, sem, n):
    pltpu.make_async_copy(src_hbm.at[pl.ds(0, n)], dst, sem).wait()


def _dispatch_kernel(pos_ref, zs_ref, zn_ref, nu_ref, h_ref, xs_hbm, zrow_ref, ztile_ref, sem, zsem,
                     *, t_total, n_e, first_free_tile):
    i = pl.program_id(0)
    tm = h_ref.shape[0]
    tg = ztile_ref.shape[0]
    n_tiles = xs_hbm.shape[0] // tg
    n_used = nu_ref[0]

    @pl.when(i == 0)
    def _():
        zrow_ref[...] = jnp.zeros_like(zrow_ref)
        ztile_ref[...] = jnp.zeros_like(ztile_ref)
        pad_row = lambda p: pltpu.make_async_copy(zrow_ref.at[pl.ds(0, 1)], xs_hbm.at[pl.ds(p, 1)], zsem)
        tail_tile = lambda j: pltpu.make_async_copy(ztile_ref, xs_hbm.at[pl.ds(j * tg, tg)], zsem)

        def pads(do):
            for e in range(n_e):
                base = zs_ref[e]

                def body(r, c):
                    do(pad_row(base + r))
                    return c
                lax.fori_loop(0, zn_ref[e], body, 0)
            for j in range(first_free_tile, n_tiles):
                @pl.when(j >= n_used)
                def _():
                    do(tail_tile(j))

        pads(lambda cp: cp.start())
        pads(lambda cp: cp.wait())

    for k in range(TOP_K_INNER):
        def body(q, c):
            for u in range(ROW_DMA_UNROLL):
                r = q * ROW_DMA_UNROLL + u
                p = pos_ref[k * t_total + i * tm + r]
                pltpu.make_async_copy(h_ref.at[pl.ds(r, 1)], xs_hbm.at[pl.ds(p, 1)], sem).start(priority=u % 2)
            return c
        lax.fori_loop(0, tm // ROW_DMA_UNROLL, body, 0)
    for k in range(TOP_K_INNER):
        pltpu.make_async_copy(h_ref, xs_hbm.at[pl.ds(0, tm)], sem).wait()


def _dispatch(pos, zero_start, zero_count, n_used, h, *, n_tiles, tg, tm=1024):
    t, d = h.shape
    n_e = zero_start.shape[0]
    grid_spec = pltpu.PrefetchScalarGridSpec(
        num_scalar_prefetch=4,
        grid=(t // tm,),
        in_specs=[pl.BlockSpec((tm, d), lambda i, *_: (i, 0))],
        out_specs=pl.BlockSpec(memory_space=pl.ANY),
        scratch_shapes=[pltpu.VMEM((SUBLANES_V7X, d), h.dtype), pltpu.VMEM((tg, d), h.dtype),
                        pltpu.SemaphoreType.DMA(()), pltpu.SemaphoreType.DMA(())],
    )
    return pl.pallas_call(
        functools.partial(_dispatch_kernel, t_total=t, n_e=n_e, first_free_tile=(TOP_K_INNER * t) // tg),
        grid_spec=grid_spec,
        out_shape=jax.ShapeDtypeStruct((n_tiles * tg, d), h.dtype),
        compiler_params=_params("arbitrary"),
        name="dispatch",
    )(pos, zero_start, zero_count, n_used, h)


def _moe_kernel(te_ref, nxt_ref, nu_ref, xs_ref, w1_hbm, w3_hbm, w2_hbm, o_ref,
                w1s, w3s, w2s, w1b, w3b, w2b, wslot_ref, wsem, *, w_base):
    i = pl.program_id(0)
    n_used = nu_ref[0]

    def weight_copies(e, slot):
        return [pltpu.make_async_copy(hbm.at[w_base + e], stage.at[slot], wsem.at[slot])
                for hbm, stage in ((w1_hbm, w1s), (w3_hbm, w3s), (w2_hbm, w2s))]

    @pl.when(i == 0)
    def _():
        wslot_ref[0] = 1
        for cp in weight_copies(te_ref[0], 0):
            cp.start()

    @pl.when(i < n_used)
    def _():
        e = te_ref[i]

        @pl.when((i == 0) | (e != te_ref[jnp.maximum(i - 1, 0)]))
        def _():
            ws = 1 - wslot_ref[0]
            wslot_ref[0] = ws
            for cp in weight_copies(e, ws):
                cp.wait()
            w1b[...] = w1s[ws].astype(BF16)
            w3b[...] = w3s[ws].astype(BF16)
            w2b[...] = w2s[ws].astype(BF16)
            ne = nxt_ref[e]

            @pl.when(ne >= 0)
            def _():
                for cp in weight_copies(ne, 1 - ws):
                    cp.start()

        x_lo, x_hi = (v.astype(BF16) for v in _unpack_rows(xs_ref[...]))
        half = x_lo.shape[1]
        a = _dot(x_lo, w1b[:half, :]) + _dot(x_hi, w1b[half:, :])
        b = _dot(x_lo, w3b[:half, :]) + _dot(x_hi, w3b[half:, :])
        o_ref[...] = _pack_rows(_dot((_silu(a) * b).astype(BF16), w2b[...]))

    @pl.when(i >= n_used)
    def _():
        o_ref[...] = jnp.zeros_like(o_ref)


def _moe(xs, w1, w3, w2, w_base, te, nxt, n_used, *, tg):
    p_tot, dp = xs.shape
    d, f = w1.shape[-2:]
    any_spec = pl.BlockSpec(memory_space=pl.ANY)
    grid_spec = pltpu.PrefetchScalarGridSpec(
        num_scalar_prefetch=3,
        grid=(p_tot // tg,),
        in_specs=[pl.BlockSpec((tg, dp), lambda i, te, nxt, nu: (jnp.minimum(i, nu[0] - 1), 0)),
                  any_spec, any_spec, any_spec],
        out_specs=pl.BlockSpec((tg, dp), lambda i, *_: (i, 0)),
        scratch_shapes=[pltpu.VMEM((2, d, f), F32), pltpu.VMEM((2, d, f), F32), pltpu.VMEM((2, f, d), F32),
                        pltpu.VMEM((d, f), BF16), pltpu.VMEM((d, f), BF16), pltpu.VMEM((f, d), BF16),
                        pltpu.SMEM((1,), I32), pltpu.SemaphoreType.DMA((2,))],
    )
    return pl.pallas_call(
        functools.partial(_moe_kernel, w_base=w_base),
        grid_spec=grid_spec,
        out_shape=jax.ShapeDtypeStruct((p_tot, dp), U32),
        compiler_params=_params("arbitrary"),
        name="moe",
    )(te, nxt, n_used, xs, w1, w3, w2)


def _comb_kernel(pos_ref, y_hbm, wt_ref, x_ref, gt_ref, lng_ref, lnb_ref, *rest,
                 alpha, tm, t_total, has_next, n_prompt_tiles):
    if has_next:
        shn_ref, scn_ref, x2_ref, hn_ref, ybuf, sem = rest
    else:
        x2p_ref, x2s_ref, ybuf, sem = rest
    i = pl.program_id(0)
    n = pl.num_programs(0)

    def start(tile, slot):
        for k in range(TOP_K_INNER):
            _row_gather_start(y_hbm, ybuf.at[slot, k], sem.at[slot], pos_ref, k * t_total + tile * tm, tm)

    @pl.when(i == 0)
    def _():
        start(0, 0)

    slot = i % 2

    @pl.when(i + 1 < n)
    def _():
        start(i + 1, 1 - slot)

    for k in range(TOP_K_INNER):
        _row_gather_wait(y_hbm, ybuf.at[slot, k], sem.at[slot], tm)
    w = wt_ref[...]
    f = (w[:, 0:1] * jnp.concatenate(_unpack_rows(ybuf[slot, 0]), axis=-1)
         + w[:, 1:2] * jnp.concatenate(_unpack_rows(ybuf[slot, 1]), axis=-1))
    g, s, d = x_ref.shape
    x2 = _layer_norm(alpha * x_ref[...] + gt_ref[...][:, None, :] * f.reshape(g, s, d), lng_ref[...], lnb_ref[...])
    if has_next:
        x2_ref[...] = x2
        hn = x2 * (1.0 + scn_ref[...][:, None, :]) + shn_ref[...][:, None, :]
        hn_ref[...] = hn.reshape(g * s, d).astype(BF16)
    else:
        @pl.when(i < n_prompt_tiles)
        def _():
            x2p_ref[...] = x2

        @pl.when(i >= n_prompt_tiles)
        def _():
            x2s_ref[...] = x2


def _comb(pos_flat, y_sorted, wt_t, x1, table, layer, ln_g, ln_b, *, alpha, has_next, t_prompt, tm=256):
    g_total, s, d = x1.shape
    t = g_total * s
    gt = tm // SUBLANES_V7X
    n_p = t_prompt // tm

    def mod(l, c):
        return pl.BlockSpec((None, gt, d), lambda i, pos: (l, i, c))

    xspec = pl.BlockSpec((gt, s, d), lambda i, pos: (i, 0, 0))
    vec = pl.BlockSpec((1, d), lambda i, pos: (0, 0))
    in_specs = [pl.BlockSpec(memory_space=pl.ANY),
                pl.BlockSpec((tm, TOP_K_INNER), lambda i, pos: (i, 0)),
                xspec, mod(layer, 5), vec, vec]
    ins = [y_sorted, wt_t, x1, table, ln_g.reshape(1, d), ln_b.reshape(1, d)]
    if has_next:
        in_specs += [mod(layer + 1, 0), mod(layer + 1, 1)]
        ins += [table, table]
        out_specs = [xspec, pl.BlockSpec((tm, d), lambda i, pos: (i, 0))]
        out_shape = [jax.ShapeDtypeStruct(x1.shape, F32), jax.ShapeDtypeStruct((t, d), BF16)]
    else:
        out_specs = [pl.BlockSpec((gt, s, d), lambda i, pos: (jnp.minimum(i, n_p - 1), 0, 0)),
                     pl.BlockSpec((gt, s, d), lambda i, pos: (jnp.maximum(i - n_p, 0), 0, 0))]
        out_shape = [jax.ShapeDtypeStruct((t_prompt // s, s, d), F32),
                     jax.ShapeDtypeStruct(((t - t_prompt) // s, s, d), F32)]
    grid_spec = pltpu.PrefetchScalarGridSpec(
        num_scalar_prefetch=1,
        grid=(t // tm,),
        in_specs=in_specs,
        out_specs=out_specs,
        scratch_shapes=[pltpu.VMEM((2, TOP_K_INNER, tm, d // 2), U32), pltpu.SemaphoreType.DMA((2,))],
    )
    return pl.pallas_call(
        functools.partial(_comb_kernel, alpha=alpha, tm=tm, t_total=t, has_next=has_next, n_prompt_tiles=n_p),
        grid_spec=grid_spec,
        out_shape=out_shape,
        compiler_params=_params("arbitrary"),
        name="comb",
    )(pos_flat, *ins)


def _gla_tables(rows, span):
    t = np.arange(rows)[:, None]
    u = np.arange(rows)[None, :]
    same = (t // span) == (u // span)
    mats = [same & (u <= t), same]
    n_levels, m = 0, 1
    while m < span:
        mid = (t // (2 * m)) * (2 * m) + m - 1
        mats.append(same & np.where(t > mid, (u > mid) & (u <= t), (u > t) & (u <= mid)))
        m *= 2
        n_levels += 1
    x = t ^ u
    level = np.where(x > 0, np.floor(np.log2(np.maximum(x, 1))), n_levels).astype(np.int32)
    lid = np.where(same & (u <= t), level, -1).astype(np.int32)
    grp = (t // span) == np.arange(LANES_V7X)[None, :]
    stack = np.concatenate(mats, axis=0).astype(np.float32)
    return jnp.asarray(stack, BF16), jnp.asarray(lid), jnp.asarray(grp.astype(np.float32), BF16), n_levels


def _gla_block(q, k, la, stack, lid, grp, n_levels):
    rows = q.shape[0]
    hi, lo = _split_bf16(la)
    dall = _dot(stack, hi)
    d_lo = _dot(stack[0:2 * rows, :], lo)
    cum, total = dall[0:rows] + d_lo[0:rows], dall[rows:2 * rows] + d_lo[rows:2 * rows]
    group_total = (lax.dot_general(hi, grp, TN_DIMS, preferred_element_type=F32)
                   + lax.dot_general(lo, grp, TN_DIMS, preferred_element_type=F32))
    nt = lambda a, b: lax.dot_general(a.astype(BF16), b.astype(BF16), NT_DIMS, preferred_element_type=F32)
    sc = jnp.where(lid == n_levels, nt(q, k), 0.0)
    for l in range(n_levels):
        e = jnp.exp(dall[(2 + l) * rows:(3 + l) * rows])
        sc = jnp.where(lid == l, nt(q * e, k * e), sc)
    return sc.astype(BF16), cum, total, group_total


def _gla_log_decay(gl, wgk, bg_row):
    return _log_sigmoid(_dot(gl, wgk) + bg_row) * (1.0 / GLA_GATE_NORMALIZER)


def _rms_gate(o, ng, gate):
    on = o * lax.rsqrt(jnp.mean(o * o, axis=-1, keepdims=True) + RMS_EPS) * ng
    return (on * _silu(gate)).astype(BF16)


def _gla_prompt_kernel(q_ref, k_ref, v_ref, g_ref, gl_ref, wgk_ref, bgr_ref, ng_ref, stack_ref, lid_ref, grp_ref,
                       y_ref, sout_ref, st_ref, *, n_heads, scale, n_levels):
    j = pl.program_id(1)

    @pl.when(j == 0)
    def _():
        st_ref[...] = jnp.zeros_like(st_ref)

    dk = q_ref.shape[1] // n_heads
    dv = v_ref.shape[1] // n_heads
    gl = gl_ref[...]
    for h in range(n_heads):
        ks, vs = slice(h * dk, (h + 1) * dk), slice(h * dv, (h + 1) * dv)
        q = q_ref[:, ks].astype(F32) * scale
        k = k_ref[:, ks].astype(F32)
        v = v_ref[:, vs]
        la = _gla_log_decay(gl, wgk_ref[:, ks], bgr_ref[:, ks])
        sc, cum, total, group_total = _gla_block(q, k, la, stack_ref[...], lid_ref[...], grp_ref[...], n_levels)
        s_old = st_ref[h]
        o = _dot(sc, v) + _dot((q * jnp.exp(cum)).astype(BF16), s_old.astype(BF16))
        kd = (k * jnp.exp(total - cum)).astype(BF16)
        st_ref[h] = s_old * jnp.exp(group_total[:, 0:1]) + lax.dot_general(
            kd, v, TN_DIMS, preferred_element_type=F32)
        y_ref[:, vs] = _rms_gate(o, ng_ref[...], g_ref[:, vs].astype(F32))

    @pl.when(j == pl.num_programs(1) - 1)
    def _():
        sout_ref[...] = st_ref[...]


def _gla_sample_kernel(q_ref, k_ref, v_ref, g_ref, gl_ref, wgk_ref, bgr_ref, ng_ref, stack_ref, lid_ref, grp_ref,
                       sin_ref, y_ref, sout_ref, *, n_heads, scale, seq, n_levels):
    rows = q_ref.shape[0]
    nb = rows // seq
    dk = q_ref.shape[1] // n_heads
    dv = v_ref.shape[1] // n_heads
    gl = gl_ref[...]
    for h in range(n_heads):
        ks, vs = slice(h * dk, (h + 1) * dk), slice(h * dv, (h + 1) * dv)
        q = q_ref[:, ks].astype(F32) * scale
        k = k_ref[:, ks].astype(F32)
        v = v_ref[:, vs]
        la = _gla_log_decay(gl, wgk_ref[:, ks], bgr_ref[:, ks])
        sc, cum, total, group_total = _gla_block(q, k, la, stack_ref[...], lid_ref[...], grp_ref[...], n_levels)
        q_dec = q * jnp.exp(cum)
        kd = k * jnp.exp(total - cum)
        v32 = v.astype(F32)
        o_state = []
        for s in range(nb):
            rs = slice(s * seq, (s + 1) * seq)
            s_old = sin_ref[s, h]
            o_state.append(_dot(q_dec[rs, :].astype(BF16), s_old.astype(BF16)))
            sout_ref[s, h] = s_old * jnp.exp(group_total[:, s:s + 1]) + lax.dot_general(
                kd[rs, :].astype(BF16), v32[rs, :].astype(BF16), TN_DIMS, preferred_element_type=F32)
        o = _dot(sc, v) + jnp.concatenate(o_state, axis=0)
        y_ref[:, vs] = _rms_gate(o, ng_ref[...], g_ref[:, vs].astype(F32))


def _gla(p, gl, wgk, bg_row, ng, state_s, *, t_prompt, seq, n_heads, dk, dv, bk=256, nb=4):
    t = p.shape[0]
    bs, _, _, _ = state_s.shape
    s_len = (t - t_prompt) // bs
    bp = t_prompt // seq
    dkt, dvt = n_heads * dk, n_heads * dv
    assert dvt == 2 * dkt
    scale = dk ** -0.5
    r = gl.shape[1]
    const = lambda *_: (0, 0)

    def tables(rows, span):
        stack, lid, grp, n_levels = _gla_tables(rows, span)
        specs = [pl.BlockSpec((r, dkt), const), pl.BlockSpec((1, dkt), const), pl.BlockSpec((1, dv), const),
                 pl.BlockSpec(stack.shape, const), pl.BlockSpec(lid.shape, const), pl.BlockSpec(grp.shape, const)]
        return specs, [wgk, bg_row, ng, stack, lid, grp], n_levels

    nblk = seq // bk
    rowp = lambda b, j: b * nblk + j
    w_specs, w_ins, n_levels = tables(bk, bk)
    y_p, s_p = pl.pallas_call(
        functools.partial(_gla_prompt_kernel, n_heads=n_heads, scale=scale, n_levels=n_levels),
        grid=(bp, nblk),
        in_specs=[pl.BlockSpec((bk, dkt), lambda b, j: (rowp(b, j), 0)),
                  pl.BlockSpec((bk, dkt), lambda b, j: (rowp(b, j), 1)),
                  pl.BlockSpec((bk, dvt), lambda b, j: (rowp(b, j), 1)),
                  pl.BlockSpec((bk, dvt), lambda b, j: (rowp(b, j), 2)),
                  pl.BlockSpec((bk, r), lambda b, j: (rowp(b, j), 0))] + w_specs,
        out_specs=[pl.BlockSpec((bk, dvt), lambda b, j: (rowp(b, j), 0)),
                   pl.BlockSpec((None, n_heads, dk, dv), lambda b, j: (b, 0, 0, 0))],
        out_shape=[jax.ShapeDtypeStruct((t_prompt, dvt), BF16),
                   jax.ShapeDtypeStruct((bp, n_heads, dk, dv), F32)],
        scratch_shapes=[pltpu.VMEM((n_heads, dk, dv), F32)],
        compiler_params=_params("arbitrary", "arbitrary"),
        name="gla_prompt",
    )(p, p, p, p, gl, *w_ins)

    rows = nb * s_len
    off = t_prompt // rows
    w_specs, w_ins, n_levels = tables(rows, s_len)
    y_s, s_s = pl.pallas_call(
        functools.partial(_gla_sample_kernel, n_heads=n_heads, scale=scale, seq=s_len, n_levels=n_levels),
        grid=(bs // nb,),
        in_specs=[pl.BlockSpec((rows, dkt), lambda i: (off + i, 0)),
                  pl.BlockSpec((rows, dkt), lambda i: (off + i, 1)),
                  pl.BlockSpec((rows, dvt), lambda i: (off + i, 1)),
                  pl.BlockSpec((rows, dvt), lambda i: (off + i, 2)),
                  pl.BlockSpec((rows, r), lambda i: (off + i, 0))] + w_specs + [
                  pl.BlockSpec((nb, n_heads, dk, dv), lambda i: (i, 0, 0, 0))],
        out_specs=[pl.BlockSpec((rows, dvt), lambda i: (i, 0)),
                   pl.BlockSpec((nb, n_heads, dk, dv), lambda i: (i, 0, 0, 0))],
        out_shape=[jax.ShapeDtypeStruct((t - t_prompt, dvt), BF16),
                   jax.ShapeDtypeStruct(state_s.shape, F32)],
        compiler_params=_params("arbitrary"),
        name="gla_sample",
    )(p, p, p, p, gl, *w_ins, state_s)
    return (y_p, y_s), s_p, s_s


def _moe_schedule(eid, rank, counts, *, tg, n_tiles):
    n_e = counts.shape[0]
    e_ids = jnp.arange(n_e, dtype=I32)
    padded = ((counts + tg - 1) // tg) * tg
    ends = jnp.sum(jnp.where(e_ids[None, :] <= e_ids[:, None], padded[None, :], 0), axis=1)
    starts = ends - padded
    pos = jnp.sum(jnp.where(eid[None] == e_ids[:, None, None], starts[:, None, None], 0), axis=0) + rank
    n_used = ends[n_e - 1] // tg
    tile_start = jnp.arange(n_tiles, dtype=I32) * tg
    te = jnp.sum((ends[None, :] <= tile_start[:, None]).astype(I32), axis=1)
    te_last = jnp.sum((ends <= (n_used - 1) * tg).astype(I32))
    te = jnp.where(jnp.arange(n_tiles) < n_used, te, te_last)
    later = (e_ids[None, :] > e_ids[:, None]) & (counts[None, :] > 0)
    nxt = jnp.min(jnp.where(later, e_ids[None, :], n_e), axis=1)
    nxt = jnp.where(nxt == n_e, -1, nxt)
    i32 = lambda a: a.astype(I32)
    return (i32(pos.reshape(-1)), i32(starts + counts), i32(padded - counts), i32(te), i32(nxt),
            i32(n_used.reshape(1)))


def kernel(x_prompt, x_sample, cache_conv, state_gla, c_prompt, c_sample, w_mod, b_mod, ln_g, ln_b, ab_w_in, ab_conv_w, ab_v_ln_g, ab_v_ln_b, ab_w_s, ab_b_s, ab_w_out, gla_w_in, gla_w_gk, gla_b_gk, gla_norm_g, gla_w_out, moe_w_grp, moe_b_grp, moe_w_rt, moe_b_rt, moe_w1, moe_w3, moe_w2):
    bp, seq, d = x_prompt.shape
    bs, s_len, _ = x_sample.shape
    assert s_len == SUBLANES_V7X and seq % SUBLANES_V7X == 0
    depth = w_mod.shape[0]
    alpha = float((2 * depth) ** 0.25)
    t_p, t_s = bp * seq, bs * s_len
    t = t_p + t_s
    n_groups, n_exp = moe_w_rt.shape[1], moe_w_rt.shape[3]
    n_e = n_groups * n_exp
    d_ff = moe_w1.shape[-1]
    tg = 256
    n_tiles = (TOP_K_INNER * t) // tg + n_e

    x = (x_prompt.reshape(t_p // SUBLANES_V7X, SUBLANES_V7X, d), x_sample)
    table = _mod_table(c_prompt, c_sample, w_mod, b_mod, seq)

    w1 = moe_w1.reshape(depth * n_e, d, d_ff)
    w3 = moe_w3.reshape(depth * n_e, d, d_ff)
    w2 = moe_w2.reshape(depth * n_e, d_ff, d)

    conv_p, conv_s, chunk_v, gla_p, gla_s = [], [], [], [], []
    h_bf = None
    for layer in range(depth):
        li = layer // 2
        if layer % 2 == 0:
            n_heads, chunk = ab_w_s.shape[1], ab_w_s.shape[2]
            dc = ab_conv_w.shape[-1]
            if h_bf is None:
                p = _mm(_modulate(x, table, layer, 0, 1), ab_w_in, li, ab_w_in.shape[-1], tm=1024)
            else:
                p = _mm(h_bf, ab_w_in, li, ab_w_in.shape[-1])
            w_s = ab_w_s[li]
            wm_p = jnp.tril(w_s)
            reps = chunk // s_len
            blk = jnp.tril(w_s[:, :s_len, :s_len])
            wm_s = jnp.einsum("ab,hts->hatbs", jnp.eye(reps, dtype=F32), blk).reshape(n_heads, chunk, chunk)
            wm = jnp.stack([wm_p, wm_s]).astype(BF16)
            b_s = ab_b_s[li]
            hd = dc // n_heads
            bias_p = jnp.repeat(b_s.T, hd, axis=1)
            bias_s = jnp.repeat(jnp.tile(b_s[:, :s_len].T, (reps, 1)), hd, axis=1)
            bias = jnp.stack([bias_p, bias_s])
            y, cp_new, cs_new, vn_s = _mix0(p, cache_conv[li], ab_conv_w[li], ab_v_ln_g[li], ab_v_ln_b[li],
                                            wm, bias, t_prompt=t_p, seq=seq, n_heads=n_heads)
            conv_p.append(cp_new)
            conv_s.append(cs_new)
            chunk_v.append(vn_s)
            w_out = ab_w_out[li].astype(BF16)
        else:
            n_heads, dk, dv = state_gla.shape[2], state_gla.shape[3], state_gla.shape[4]
            dkt, dvt = n_heads * dk, n_heads * dv
            rank = gla_w_gk.shape[1]
            n_main = 2 * dkt + 2 * dvt
            w_in_t = jnp.swapaxes(gla_w_in, 1, 2)
            p = _mm(h_bf, w_in_t, li, n_main, tm=1024, w_transposed=True)
            w_lo = jnp.pad(w_in_t[li, n_main:, :], ((0, LANES_V7X - rank), (0, 0)))[None]
            gl = _mm(h_bf, w_lo, 0, LANES_V7X, w_transposed=True)
            wgk = jnp.pad(gla_w_gk[li], ((0, LANES_V7X - rank), (0, 0))).astype(BF16)
            y, sp_new, ss_new = _gla(p, gl, wgk, gla_b_gk[li].reshape(1, dkt),
                                     gla_norm_g[li].reshape(1, dv), state_gla[li],
                                     t_prompt=t_p, seq=seq, n_heads=n_heads, dk=dk, dv=dv)
            gla_p.append(sp_new)
            gla_s.append(ss_new)
            w_out = gla_w_out[li].astype(BF16)

        wr = jnp.concatenate([moe_w_grp[layer].T,
                              jnp.transpose(moe_w_rt[layer], (0, 2, 1)).reshape(n_e, d)], axis=0)
        wr = jnp.pad(wr, ((0, LANES_V7X - wr.shape[0]), (0, 0)))
        br = jnp.concatenate([moe_b_grp[layer], moe_b_rt[layer].reshape(n_e)])
        br = jnp.pad(br, (0, LANES_V7X - br.shape[0])).reshape(LANES_V7X, 1)
        x1, h2, eid, wt, rank_, cnt = _outln(y, w_out, x, table, layer, ln_g[layer, 0], ln_b[layer, 0], wr, br,
                                             alpha=alpha, n_groups=n_groups, n_exp=n_exp)
        pos, zero_start, zero_count, te, nxt, n_used = _moe_schedule(eid, rank_, cnt[:, 0], tg=tg, n_tiles=n_tiles)
        xs = _dispatch(pos, zero_start, zero_count, n_used, h2, n_tiles=n_tiles, tg=tg)
        ys = _moe(xs, w1, w3, w2, layer * n_e, te, nxt, n_used, tg=tg)
        has_next = layer + 1 < depth
        outs = _comb(pos, ys, wt.T, x1, table, layer, ln_g[layer, 1], ln_b[layer, 1],
                     alpha=alpha, has_next=has_next, t_prompt=t_p)
        if has_next:
            x, h_bf = outs

    y_prompt = outs[0].reshape(bp, seq, d)
    y_sample = outs[1].reshape(bs, s_len, d)
    return (y_prompt, y_sample, jnp.stack(conv_p), jnp.stack(conv_s), jnp.stack(chunk_v),
            jnp.stack(gla_p), jnp.stack(gla_s))
```

```python
import functools

import jax
import numpy as np
import jax.numpy as jnp
from jax import lax
from jax.experimental import pallas as pl
from jax.experimental.pallas import tpu as pltpu

F32 = jnp.float32
BF16 = jnp.bfloat16
I32 = jnp.int32

LN_EPS = 1e-5
RMS_EPS = 1e-6
GLA_GATE_NORMALIZER = 16.0
TOP_K_INNER = 2

SUBLANES_V7X = 8
LANES_V7X = 128
VMEM_LIMIT_V7X = 56 * 1024 * 1024

NT_DIMS = (((1,), (1,)), ((), ()))
TN_DIMS = (((0,), (0,)), ((), ()))


def _params(*sem):
    return pltpu.CompilerParams(dimension_semantics=sem, vmem_limit_bytes=VMEM_LIMIT_V7X)


def _silu(x):
    return x * (1.0 / (1.0 + jnp.exp(-x)))


def _log_sigmoid(z):
    return jnp.minimum(z, 0.0) - jnp.log(1.0 + jnp.exp(-jnp.abs(z)))


def _layer_norm(x, g, b):
    mu = jnp.mean(x, axis=-1, keepdims=True)
    xc = x - mu
    var = jnp.mean(xc * xc, axis=-1, keepdims=True)
    return xc * lax.rsqrt(var + LN_EPS) * g + b


def _dot(a, b):
    return jnp.dot(a, b, preferred_element_type=F32)


def _split_bf16(x):
    hi = x.astype(BF16)
    lo = (x - hi.astype(F32)).astype(BF16)
    return hi, lo


def _mod_kernel(c_ref, w_ref, b_ref, op_ref, os_ref):
    r = _dot(_silu(c_ref[...]).astype(BF16), w_ref[...].astype(BF16)) + b_ref[...]
    n_p = op_ref.shape[0]
    op_ref[...] = r[:n_p, :]
    os_ref[...] = r[n_p:, :]


def _mod_vectors(c_prompt, c_sample, w_mod, b_mod):
    depth, d, n = w_mod.shape
    bp, bs = c_prompt.shape[0], c_sample.shape[0]
    bp_rows = bp + (-bp) % SUBLANES_V7X
    c_all = jnp.concatenate([jnp.pad(c_prompt, ((0, bp_rows - bp), (0, 0))), c_sample], axis=0)
    tn = 1024
    return pl.pallas_call(
        _mod_kernel,
        grid=(depth, n // tn),
        in_specs=[
            pl.BlockSpec(c_all.shape, lambda l, j: (0, 0)),
            pl.BlockSpec((None, d, tn), lambda l, j: (l, 0, j)),
            pl.BlockSpec((None, 1, tn), lambda l, j: (l, 0, j)),
        ],
        out_specs=[pl.BlockSpec((None, bp_rows, tn), lambda l, j: (l, 0, j)),
                   pl.BlockSpec((None, bs, tn), lambda l, j: (l, 0, j))],
        out_shape=[jax.ShapeDtypeStruct((depth, bp_rows, n), F32), jax.ShapeDtypeStruct((depth, bs, n), F32)],
        compiler_params=_params("arbitrary", "arbitrary"),
        name="mod",
    )(c_all, w_mod, b_mod.reshape(depth, 1, n))


def _mod_specs(mods, layer, col, d, gt, n_p):
    mp, _ = mods
    return [pl.BlockSpec((None, mp.shape[1], d), lambda i, *_: (layer, 0, col)),
            pl.BlockSpec((None, gt, d), lambda i, *_: (layer, jnp.maximum(i - n_p, 0), col))]


def _mod_value(p_ref, s_ref, prompt, seq_idx):
    if prompt:
        return p_ref[pl.ds(seq_idx, 1), :][None]
    return s_ref[...][:, None, :]


def _by_group(i, n_prompt_tiles, fn):
    pl.when(i < n_prompt_tiles)(lambda: fn(True))
    pl.when(i >= n_prompt_tiles)(lambda: fn(False))


def _mm_kernel(a_ref, w_ref, o_ref, wb_ref, *, w_transposed):
    @pl.when(pl.program_id(1) == 0)
    def _():
        w = w_ref[...]
        wb_ref[...] = (w.T if w_transposed else w).astype(BF16)

    o_ref[...] = _dot(a_ref[...], wb_ref[...]).astype(o_ref.dtype)


def _mm(a, w3, w_idx, n_out, *, tm=1024, tn=1024, out_dtype=BF16, w_transposed=False):
    t = a.shape[0]
    k = w3.shape[2 if w_transposed else 1]
    tn = min(tn, n_out)
    return pl.pallas_call(
        functools.partial(_mm_kernel, w_transposed=w_transposed),
        grid=(n_out // tn, t // tm),
        in_specs=[pl.BlockSpec((tm, k), lambda j, i: (i, 0)),
                  pl.BlockSpec((None, tn, k), lambda j, i: (w_idx, j, 0)) if w_transposed
                  else pl.BlockSpec((None, k, tn), lambda j, i: (w_idx, 0, j))],
        out_specs=pl.BlockSpec((tm, tn), lambda j, i: (i, j)),
        out_shape=jax.ShapeDtypeStruct((t, n_out), out_dtype),
        scratch_shapes=[pltpu.VMEM((k, tn), BF16)],
        compiler_params=_params("arbitrary", "arbitrary"),
        name="mm",
    )(a, w3)


def _modulate_kernel(xp_ref, xs_ref, shp_ref, shs_ref, scp_ref, scs_ref, o_ref, *, n_prompt_tiles, tiles_per_seq):
    i = pl.program_id(0)

    def run(prompt):
        x = (xp_ref if prompt else xs_ref)[...]
        g, s, k = x.shape
        seq_idx = i // tiles_per_seq
        h = x * (1.0 + _mod_value(scp_ref, scs_ref, prompt, seq_idx)) + _mod_value(shp_ref, shs_ref, prompt, seq_idx)
        o_ref[...] = h.reshape(g * s, k).astype(BF16)

    _by_group(i, n_prompt_tiles, run)


def _modulate(x_pair, mods, layer, sh_col, sc_col, *, seq, tm=512):
    xp, xs = x_pair
    k = xp.shape[-1]
    gt = tm // SUBLANES_V7X
    n_p = xp.shape[0] // gt
    t = (xp.shape[0] + xs.shape[0]) * SUBLANES_V7X
    return pl.pallas_call(
        functools.partial(_modulate_kernel, n_prompt_tiles=n_p, tiles_per_seq=seq // tm),
        grid=(t // tm,),
        in_specs=[pl.BlockSpec((gt, SUBLANES_V7X, k), lambda i: (jnp.minimum(i, n_p - 1), 0, 0)),
                  pl.BlockSpec((gt, SUBLANES_V7X, k), lambda i: (jnp.maximum(i - n_p, 0), 0, 0))]
                 + _mod_specs(mods, layer, sh_col, k, gt, n_p) + _mod_specs(mods, layer, sc_col, k, gt, n_p),
        out_specs=pl.BlockSpec((tm, k), lambda i: (i, 0)),
        out_shape=jax.ShapeDtypeStruct((t, k), BF16),
        compiler_params=_params("arbitrary"),
        name="modulate",
    )(xp, xs, *mods, *mods)


def _mix0_kernel(bg_ref, cg_ref, hx_ref, u_ref, v_ref, cache_ref, cw_ref, vg_ref, vb_ref, wm_ref, bias_ref,
                 y_ref, convp_ref, convs_ref, vns_ref, zprev_ref, *, n_prompt_tiles, tiles_per_seq, n_heads):
    i = pl.program_id(0)
    tm, dc = bg_ref.shape
    ns = tm // SUBLANES_V7X
    z = cg_ref[...].astype(F32) * hx_ref[...].astype(F32)
    row = lax.broadcasted_iota(I32, (tm, dc), 0)
    r1 = pltpu.roll(z, 1, 0)
    r2 = pltpu.roll(z, 2, 0)
    cw = cw_ref[...]
    bg = bg_ref[...].astype(F32)

    vn = _layer_norm(v_ref[...].astype(F32), vg_ref[...], vb_ref[...])
    vnb = vn.astype(BF16)
    hd = dc // n_heads
    mixed = jnp.concatenate(
        [_dot(wm_ref[h], vnb[:, h * hd:(h + 1) * hd]) for h in range(n_heads)], axis=-1) + bias_ref[...]
    y_ref[:, dc:] = (u_ref[...].astype(F32) * mixed).astype(BF16)

    def conv_out(zm1, zm2):
        conv = cw[0:1, :] * zm2 + cw[1:2, :] * zm1 + cw[2:3, :] * z
        y_ref[:, :dc] = (bg * conv).astype(BF16)

    @pl.when(i < n_prompt_tiles)
    def _prompt():
        @pl.when(i % tiles_per_seq == 0)
        def _():
            zprev_ref[...] = jnp.zeros_like(zprev_ref)

        zp = zprev_ref[...]
        p1 = zp[SUBLANES_V7X - 1:SUBLANES_V7X, :]
        p2 = zp[SUBLANES_V7X - 2:SUBLANES_V7X - 1, :]
        conv_out(jnp.where(row == 0, p1, r1),
                 jnp.where(row == 0, p2, jnp.where(row == 1, p1, r2)))
        zprev_ref[...] = z[tm - SUBLANES_V7X:, :]
        convp_ref[...] = z[tm - 2:, :].reshape(1, 2, dc)

    @pl.when(i >= n_prompt_tiles)
    def _sample():
        c = cache_ref[...]
        c0 = jnp.broadcast_to(c[:, 0:1, :], (ns, SUBLANES_V7X, dc)).reshape(tm, dc)
        c1 = jnp.broadcast_to(c[:, 1:2, :], (ns, SUBLANES_V7X, dc)).reshape(tm, dc)
        rr = row % SUBLANES_V7X
        conv_out(jnp.where(rr == 0, c1, r1),
                 jnp.where(rr == 0, c0, jnp.where(rr == 1, c1, r2)))
        z3 = z.reshape(ns, SUBLANES_V7X, dc)
        convs_ref[...] = z3[:, SUBLANES_V7X - 2:, :]
        vns_ref[...] = vn.reshape(ns, SUBLANES_V7X, dc)


def _mix0(p, cache, conv_w, v_g, v_b, wm, bias, *, t_prompt, seq, n_heads):
    t, n = p.shape
    bs, cwm1, dc = cache.shape
    tm = wm.shape[-1]
    assert cwm1 == 2 and conv_w.shape[0] == 3 and n == 5 * dc and seq % tm == 0
    n_p = t_prompt // tm
    n_s = (t - t_prompt) // tm
    tps = seq // tm
    bp = t_prompt // seq
    ns = tm // SUBLANES_V7X

    def col(c):
        return pl.BlockSpec((tm, dc), lambda i: (i, c))

    def s_idx(i):
        return jnp.maximum(i - n_p, 0)

    const2 = lambda i: (0, 0)
    mode = lambda i: ((i >= n_p).astype(I32), 0, 0, 0)
    return pl.pallas_call(
        functools.partial(_mix0_kernel, n_prompt_tiles=n_p, tiles_per_seq=tps, n_heads=n_heads),
        grid=(n_p + n_s,),
        in_specs=[col(0), col(1), col(2), col(3), col(4),
                  pl.BlockSpec((ns, 2, dc), lambda i: (s_idx(i), 0, 0)),
                  pl.BlockSpec((3, dc), const2),
                  pl.BlockSpec((1, dc), const2),
                  pl.BlockSpec((1, dc), const2),
                  pl.BlockSpec((None, n_heads, tm, tm), mode),
                  pl.BlockSpec((None, tm, dc), lambda i: ((i >= n_p).astype(I32), 0, 0))],
        out_specs=[pl.BlockSpec((tm, 2 * dc), lambda i: (i, 0)),
                   pl.BlockSpec((1, 2, dc), lambda i: (jnp.minimum(i // tps, bp - 1), 0, 0)),
                   pl.BlockSpec((ns, 2, dc), lambda i: (s_idx(i), 0, 0)),
                   pl.BlockSpec((ns, SUBLANES_V7X, dc), lambda i: (s_idx(i), 0, 0))],
        out_shape=[jax.ShapeDtypeStruct((t, 2 * dc), BF16),
                   jax.ShapeDtypeStruct((bp, 2, dc), F32),
                   jax.ShapeDtypeStruct((bs, 2, dc), F32),
                   jax.ShapeDtypeStruct((bs, SUBLANES_V7X, dc), F32)],
        scratch_shapes=[pltpu.VMEM((SUBLANES_V7X, dc), F32)],
        compiler_params=_params("arbitrary"),
        name="mix0",
    )(p, p, p, p, p, cache, conv_w, v_g.reshape(1, dc), v_b.reshape(1, dc), wm, bias)


def _first_index_of(vals, target):
    idx = jnp.full(target.shape, len(vals) - 1, I32)
    for j in reversed(range(len(vals))):
        idx = jnp.where(vals[j] == target, j, idx)
    return idx


def _softmax_rows(rows):
    m = functools.reduce(jnp.maximum, rows)
    e = [jnp.exp(r - m) for r in rows]
    s = functools.reduce(lambda a, b: a + b, e)
    return [x / s for x in e]


def _route(h, wr_ref, br_ref, eid_ref, wt_ref, rank_ref, cnt_ref, carry_ref, n_groups, n_exp):
    tm = h.shape[0]
    hh, hl = _split_bf16(h)
    wh, wl = _split_bf16(wr_ref[...])
    dg = lambda a, b: lax.dot_general(a, b, NT_DIMS, preferred_element_type=F32)
    logits = dg(wh, hh) + dg(wh, hl) + dg(wl, hh) + br_ref[...]

    g_prob = _softmax_rows([logits[g:g + 1, :] for g in range(n_groups)])
    g_top = functools.reduce(jnp.maximum, g_prob)
    g_idx = _first_index_of(g_prob, g_top)

    e_sel = []
    for e in range(n_exp):
        sel = logits[n_groups + e:n_groups + e + 1, :]
        for g in range(1, n_groups):
            r = n_groups + g * n_exp + e
            sel = jnp.where(g_idx == g, logits[r:r + 1, :], sel)
        e_sel.append(sel)
    e_prob = _softmax_rows(e_sel)
    p1 = functools.reduce(jnp.maximum, e_prob)
    i1 = _first_index_of(e_prob, p1)
    rest = [jnp.where(i1 == e, -1.0, e_prob[e]) for e in range(n_exp)]
    p2 = functools.reduce(jnp.maximum, rest)
    i2 = _first_index_of(rest, p2)
    den = p1 + p2
    wt_ref[0:1, :] = g_top * (p1 / den)
    wt_ref[1:2, :] = g_top * (p2 / den)
    eid0 = g_idx * n_exp + i1
    eid1 = g_idx * n_exp + i2
    eid_ref[0:1, :] = eid0
    eid_ref[1:2, :] = eid1

    n_e = n_groups * n_exp
    eio = lax.broadcasted_iota(I32, (n_e, tm), 0)
    oh0 = (eio == eid0).astype(F32)
    oh1 = (eio == eid1).astype(F32)
    oh = oh0 + oh1
    before = (lax.broadcasted_iota(I32, (tm, tm), 0) < lax.broadcasted_iota(I32, (tm, tm), 1)).astype(BF16)
    base = _dot(oh.astype(BF16), before) + carry_ref[...]
    rank_ref[0:1, :] = jnp.sum(oh0 * base, axis=0, keepdims=True).astype(I32)
    rank_ref[1:2, :] = jnp.sum(oh1 * base, axis=0, keepdims=True).astype(I32)
    total = carry_ref[...] + jnp.sum(oh, axis=1, keepdims=True)
    carry_ref[...] = total
    cnt_ref[...] = jnp.broadcast_to(total, cnt_ref.shape).astype(I32)


def _outln_kernel(*refs, alpha, n_groups, n_exp, y_split, x_split, n_prompt_tiles, tiles_per_seq):
    refs = list(refs)
    i = pl.program_id(0)
    take = lambda n: [refs.pop(0) for _ in range(n)]
    y_refs, (w_ref,), x_refs = take(2 if y_split else 1), take(1), take(2 if x_split else 1)
    gt_refs, sh_refs, sc_refs = take(2), take(2), take(2)
    (lng_ref, lnb_ref, wr_ref, br_ref, x1_ref, h_ref, eid_ref, wt_ref, rank_ref, cnt_ref, carry_ref) = refs

    @pl.when(i == 0)
    def _():
        carry_ref[...] = jnp.zeros_like(carry_ref)

    def run(prompt):
        pick = lambda pair: pair[0 if prompt or len(pair) == 1 else 1]
        mod = lambda pair: _mod_value(*pair, prompt, i // tiles_per_seq)
        x = pick(x_refs)[...]
        g, s, d = x.shape
        m = _dot(pick(y_refs)[...], w_ref[...]).reshape(g, s, d)
        x1 = _layer_norm(alpha * x + mod(gt_refs) * m, lng_ref[...], lnb_ref[...])
        x1_ref[...] = x1
        h = (x1 * (1.0 + mod(sc_refs)) + mod(sh_refs)).reshape(g * s, d)
        h_ref[...] = h
        _route(h, wr_ref, br_ref, eid_ref, wt_ref, rank_ref, cnt_ref, carry_ref, n_groups, n_exp)

    _by_group(i, n_prompt_tiles, run)


def _outln(y, w_bf, x, mods, layer, ln_g, ln_b, wr, br, *, alpha, n_groups, n_exp, t_prompt, seq, tm=512):
    y_split, x_split = isinstance(y, tuple), isinstance(x, tuple)
    ys, xs = (y if y_split else (y,)), (x if x_split else (x,))
    k, d = w_bf.shape
    gt = tm // SUBLANES_V7X
    n_e = n_groups * n_exp
    t = sum(a.shape[0] for a in ys)
    n_p = t_prompt // tm

    def split_specs(block, n_arrays):
        if n_arrays == 1:
            return [pl.BlockSpec(block, lambda i: (i,) + (0,) * (len(block) - 1))]
        return [pl.BlockSpec(block, lambda i: (jnp.minimum(i, n_p - 1),) + (0,) * (len(block) - 1)),
                pl.BlockSpec(block, lambda i: (jnp.maximum(i - n_p, 0),) + (0,) * (len(block) - 1))]

    mod = lambda c: _mod_specs(mods, layer, c, d, gt, n_p)
    const = lambda i: (0, 0)
    pair = lambda dt: jax.ShapeDtypeStruct((TOP_K_INNER, t), dt)
    pair_spec = pl.BlockSpec((TOP_K_INNER, tm), lambda i: (0, i))
    return pl.pallas_call(
        functools.partial(_outln_kernel, alpha=alpha, n_groups=n_groups, n_exp=n_exp,
                          y_split=y_split, x_split=x_split, n_prompt_tiles=n_p, tiles_per_seq=seq // tm),
        grid=(t // tm,),
        in_specs=split_specs((tm, k), len(ys))
                 + [pl.BlockSpec((k, d), const, pipeline_mode=pl.Buffered(1))]
                 + split_specs((gt, SUBLANES_V7X, d), len(xs))
                 + mod(2) + mod(3) + mod(4)
                 + [pl.BlockSpec((1, d), const),
                    pl.BlockSpec((1, d), const),
                    pl.BlockSpec(wr.shape, const),
                    pl.BlockSpec(br.shape, const)],
        out_specs=[pl.BlockSpec((gt, SUBLANES_V7X, d), lambda i: (i, 0, 0)),
                   pl.BlockSpec((tm, d), lambda i: (i, 0)),
                   pair_spec, pair_spec, pair_spec,
                   pl.BlockSpec((n_e, LANES_V7X), const)],
        out_shape=[jax.ShapeDtypeStruct((t // SUBLANES_V7X, SUBLANES_V7X, d), F32),
                   jax.ShapeDtypeStruct((t, d), F32),
                   pair(I32), pair(F32), pair(I32),
                   jax.ShapeDtypeStruct((n_e, LANES_V7X), I32)],
        scratch_shapes=[pltpu.VMEM((n_e, 1), F32)],
        compiler_params=_params("arbitrary"),
        name="outln",
    )(*ys, w_bf, *xs, *mods, *mods, *mods, ln_g.reshape(1, d), ln_b.reshape(1, d), wr, br)


ROW_DMA_UNROLL = 8


def _row_gather_start(src_hbm, dst, sem, idx_ref, base, n):
    def body(q, c):
        for u in range(ROW_DMA_UNROLL):
            r = q * ROW_DMA_UNROLL + u
            row = idx_ref[base + r]
            pltpu.make_async_copy(src_hbm.at[pl.ds(row, 1)], dst.at[pl.ds(r, 1)], sem).start(priority=u % 2)
        return c
    lax.fori_loop(0, n // ROW_DMA_UNROLL, body, 0)


def _row_gather_wait(src_hbm, dst, sem, n):
    pltpu.make_async_copy(src_hbm.at[pl.ds(0, n)], dst, sem).wait()


def _dispatch_kernel(pos_ref, zs_ref, zn_ref, nu_ref, h_ref, xs_hbm, zrow_ref, ztile_ref, sem, zsem,
                     *, t_total, n_e, first_free_tile):
    i = pl.program_id(0)
    tm = h_ref.shape[0]
    tg = ztile_ref.shape[0]
    n_tiles = xs_hbm.shape[0] // tg
    n_used = nu_ref[0]

    @pl.when(i == 0)
    def _():
        zrow_ref[...] = jnp.zeros_like(zrow_ref)
        ztile_ref[...] = jnp.zeros_like(ztile_ref)
        pad_row = lambda p: pltpu.make_async_copy(zrow_ref.at[pl.ds(0, 1)], xs_hbm.at[pl.ds(p, 1)], zsem)
        tail_tile = lambda j: pltpu.make_async_copy(ztile_ref, xs_hbm.at[pl.ds(j * tg, tg)], zsem)

        def pads(do):
            for e in range(n_e):
                base = zs_ref[e]

                def body(r, c):
                    do(pad_row(base + r))
                    return c
                lax.fori_loop(0, zn_ref[e], body, 0)
            for j in range(first_free_tile, n_tiles):
                @pl.when(j >= n_used)
                def _():
                    do(tail_tile(j))

        pads(lambda cp: cp.start())
        pads(lambda cp: cp.wait())

    for k in range(TOP_K_INNER):
        def body(q, c):
            for u in range(ROW_DMA_UNROLL):
                r = q * ROW_DMA_UNROLL + u
                p = pos_ref[k * t_total + i * tm + r]
                pltpu.make_async_copy(h_ref.at[pl.ds(r, 1)], xs_hbm.at[pl.ds(p, 1)], sem).start(priority=u % 2)
            return c
        lax.fori_loop(0, tm // ROW_DMA_UNROLL, body, 0)
    for k in range(TOP_K_INNER):
        pltpu.make_async_copy(h_ref, xs_hbm.at[pl.ds(0, tm)], sem).wait()


def _dispatch(pos, zero_start, zero_count, n_used, h, *, n_tiles, tg, tm=1024):
    t, d = h.shape
    n_e = zero_start.shape[0]
    grid_spec = pltpu.PrefetchScalarGridSpec(
        num_scalar_prefetch=4,
        grid=(t // tm,),
        in_specs=[pl.BlockSpec((tm, d), lambda i, *_: (i, 0))],
        out_specs=pl.BlockSpec(memory_space=pl.ANY),
        scratch_shapes=[pltpu.VMEM((SUBLANES_V7X, d), h.dtype), pltpu.VMEM((tg, d), h.dtype),
                        pltpu.SemaphoreType.DMA(()), pltpu.SemaphoreType.DMA(())],
    )
    return pl.pallas_call(
        functools.partial(_dispatch_kernel, t_total=t, n_e=n_e, first_free_tile=(TOP_K_INNER * t) // tg),
        grid_spec=grid_spec,
        out_shape=jax.ShapeDtypeStruct((n_tiles * tg, d), h.dtype),
        compiler_params=_params("arbitrary"),
        name="dispatch",
    )(pos, zero_start, zero_count, n_used, h)


def _moe_kernel(te_ref, nxt_ref, nu_ref, xs_ref, w1_hbm, w3_hbm, w2_hbm, o_ref,
                w1s, w3s, w2s, w1b, w3b, w2b, wslot_ref, wsem, *, w_base):
    i = pl.program_id(0)
    n_used = nu_ref[0]

    def weight_copies(e, slot):
        return [pltpu.make_async_copy(hbm.at[w_base + e], stage.at[slot], wsem.at[slot])
                for hbm, stage in ((w1_hbm, w1s), (w3_hbm, w3s), (w2_hbm, w2s))]

    @pl.when(i == 0)
    def _():
        wslot_ref[0] = 1
        for cp in weight_copies(te_ref[0], 0):
            cp.start()

    @pl.when(i < n_used)
    def _():
        e = te_ref[i]

        @pl.when((i == 0) | (e != te_ref[jnp.maximum(i - 1, 0)]))
        def _():
            ws = 1 - wslot_ref[0]
            wslot_ref[0] = ws
            for cp in weight_copies(e, ws):
                cp.wait()
            w1b[...] = w1s[ws].astype(BF16)
            w3b[...] = w3s[ws].astype(BF16)
            w2b[...] = w2s[ws].astype(BF16)
            ne = nxt_ref[e]

            @pl.when(ne >= 0)
            def _():
                for cp in weight_copies(ne, 1 - ws):
                    cp.start()

        x = xs_ref[...].astype(BF16)
        a = _dot(x, w1b[...])
        b = _dot(x, w3b[...])
        o_ref[...] = _dot((_silu(a) * b).astype(BF16), w2b[...])

    @pl.when(i >= n_used)
    def _():
        o_ref[...] = jnp.zeros_like(o_ref)


def _moe(xs, w1, w3, w2, w_base, te, nxt, n_used, *, tg):
    p_tot, d = xs.shape
    f = w1.shape[-1]
    any_spec = pl.BlockSpec(memory_space=pl.ANY)
    grid_spec = pltpu.PrefetchScalarGridSpec(
        num_scalar_prefetch=3,
        grid=(p_tot // tg,),
        in_specs=[pl.BlockSpec((tg, d), lambda i, te, nxt, nu: (jnp.minimum(i, nu[0] - 1), 0)),
                  any_spec, any_spec, any_spec],
        out_specs=pl.BlockSpec((tg, d), lambda i, *_: (i, 0)),
        scratch_shapes=[pltpu.VMEM((2, d, f), F32), pltpu.VMEM((2, d, f), F32), pltpu.VMEM((2, f, d), F32),
                        pltpu.VMEM((d, f), BF16), pltpu.VMEM((d, f), BF16), pltpu.VMEM((f, d), BF16),
                        pltpu.SMEM((1,), I32), pltpu.SemaphoreType.DMA((2,))],
    )
    return pl.pallas_call(
        functools.partial(_moe_kernel, w_base=w_base),
        grid_spec=grid_spec,
        out_shape=jax.ShapeDtypeStruct((p_tot, d), F32),
        compiler_params=_params("arbitrary"),
        name="moe",
    )(te, nxt, n_used, xs, w1, w3, w2)


def _comb_kernel(pos_ref, y_hbm, wt_ref, x_ref, gtp_ref, gts_ref, lng_ref, lnb_ref, *rest,
                 alpha, tm, t_total, has_next, n_prompt_tiles, tiles_per_seq):
    if has_next:
        shp_ref, shs_ref, scp_ref, scs_ref, x2_ref, hn_ref, ybuf, sem = rest
    else:
        x2p_ref, x2s_ref, ybuf, sem = rest
    i = pl.program_id(0)
    n = pl.num_programs(0)

    def start(tile, slot):
        for k in range(TOP_K_INNER):
            _row_gather_start(y_hbm, ybuf.at[slot, k], sem.at[slot], pos_ref, k * t_total + tile * tm, tm)

    @pl.when(i == 0)
    def _():
        start(0, 0)

    slot = i % 2

    @pl.when(i + 1 < n)
    def _():
        start(i + 1, 1 - slot)

    for k in range(TOP_K_INNER):
        _row_gather_wait(y_hbm, ybuf.at[slot, k], sem.at[slot], tm)

    def run(prompt):
        seq_idx = i // tiles_per_seq
        w = wt_ref[...]
        f = w[:, 0:1] * ybuf[slot, 0] + w[:, 1:2] * ybuf[slot, 1]
        g, s, d = x_ref.shape
        x2 = _layer_norm(alpha * x_ref[...] + _mod_value(gtp_ref, gts_ref, prompt, seq_idx) * f.reshape(g, s, d),
                         lng_ref[...], lnb_ref[...])
        if has_next:
            x2_ref[...] = x2
            hn = (x2 * (1.0 + _mod_value(scp_ref, scs_ref, prompt, seq_idx))
                  + _mod_value(shp_ref, shs_ref, prompt, seq_idx))
            hn_ref[...] = hn.reshape(g * s, d).astype(BF16)
        else:
            (x2p_ref if prompt else x2s_ref)[...] = x2

    _by_group(i, n_prompt_tiles, run)


def _comb(pos_flat, y_sorted, wt_t, x1, mods, layer, ln_g, ln_b, *, alpha, has_next, t_prompt, seq, tm=256):
    g_total, s, d = x1.shape
    t = g_total * s
    gt = tm // SUBLANES_V7X
    n_p = t_prompt // tm
    mod = lambda l, c: _mod_specs(mods, l, c, d, gt, n_p)
    xspec = pl.BlockSpec((gt, s, d), lambda i, pos: (i, 0, 0))
    vec = pl.BlockSpec((1, d), lambda i, pos: (0, 0))
    in_specs = [pl.BlockSpec(memory_space=pl.ANY),
                pl.BlockSpec((tm, TOP_K_INNER), lambda i, pos: (i, 0)),
                xspec] + mod(layer, 5) + [vec, vec]
    ins = [y_sorted, wt_t, x1, *mods, ln_g.reshape(1, d), ln_b.reshape(1, d)]
    if has_next:
        in_specs += mod(layer + 1, 0) + mod(layer + 1, 1)
        ins += [*mods, *mods]
        out_specs = [xspec, pl.BlockSpec((tm, d), lambda i, pos: (i, 0))]
        out_shape = [jax.ShapeDtypeStruct(x1.shape, F32), jax.ShapeDtypeStruct((t, d), BF16)]
    else:
        out_specs = [pl.BlockSpec((gt, s, d), lambda i, pos: (jnp.minimum(i, n_p - 1), 0, 0)),
                     pl.BlockSpec((gt, s, d), lambda i, pos: (jnp.maximum(i - n_p, 0), 0, 0))]
        out_shape = [jax.ShapeDtypeStruct((t_prompt // s, s, d), F32),
                     jax.ShapeDtypeStruct(((t - t_prompt) // s, s, d), F32)]
    grid_spec = pltpu.PrefetchScalarGridSpec(
        num_scalar_prefetch=1,
        grid=(t // tm,),
        in_specs=in_specs,
        out_specs=out_specs,
        scratch_shapes=[pltpu.VMEM((2, TOP_K_INNER, tm, d), F32), pltpu.SemaphoreType.DMA((2,))],
    )
    return pl.pallas_call(
        functools.partial(_comb_kernel, alpha=alpha, tm=tm, t_total=t, has_next=has_next, n_prompt_tiles=n_p,
                          tiles_per_seq=seq // tm),
        grid_spec=grid_spec,
        out_shape=out_shape,
        compiler_params=_params("arbitrary"),
        name="comb",
    )(pos_flat, *ins)


def _gla_tables(rows, span):
    t = np.arange(rows)[:, None]
    u = np.arange(rows)[None, :]
    same = (t // span) == (u // span)
    mats = [same & (u <= t), same]
    n_levels, m = 0, 1
    while m < span:
        mid = (t // (2 * m)) * (2 * m) + m - 1
        mats.append(same & np.where(t > mid, (u > mid) & (u <= t), (u > t) & (u <= mid)))
        m *= 2
        n_levels += 1
    x = t ^ u
    level = np.where(x > 0, np.floor(np.log2(np.maximum(x, 1))), n_levels).astype(np.int32)
    lid = np.where(same & (u <= t), level, -1).astype(np.int32)
    grp = (t // span) == np.arange(LANES_V7X)[None, :]
    stack = np.concatenate(mats, axis=0).astype(np.float32)
    return jnp.asarray(stack, BF16), jnp.asarray(lid), jnp.asarray(grp.astype(np.float32), BF16), n_levels


def _gla_block(q, k, la, stack, lid, grp, n_levels):
    rows = q.shape[0]
    hi, lo = _split_bf16(la)
    dall = _dot(stack, hi)
    d_lo = _dot(stack[0:2 * rows, :], lo)
    cum, total = dall[0:rows] + d_lo[0:rows], dall[rows:2 * rows] + d_lo[rows:2 * rows]
    group_total = (lax.dot_general(hi, grp, TN_DIMS, preferred_element_type=F32)
                   + lax.dot_general(lo, grp, TN_DIMS, preferred_element_type=F32))
    nt = lambda a, b: lax.dot_general(a.astype(BF16), b.astype(BF16), NT_DIMS, preferred_element_type=F32)
    sc = jnp.where(lid == n_levels, nt(q, k), 0.0)
    for l in range(n_levels):
        e = jnp.exp(dall[(2 + l) * rows:(3 + l) * rows])
        sc = jnp.where(lid == l, nt(q * e, k * e), sc)
    return sc.astype(BF16), cum, total, group_total


def _gla_log_decay(gl, wgk, bg_row):
    return _log_sigmoid(_dot(gl, wgk) + bg_row) * (1.0 / GLA_GATE_NORMALIZER)


def _rms_gate(o, ng, gate):
    on = o * lax.rsqrt(jnp.mean(o * o, axis=-1, keepdims=True) + RMS_EPS) * ng
    return (on * _silu(gate)).astype(BF16)


def _gla_prompt_kernel(q_ref, k_ref, v_ref, g_ref, gl_ref, wgk_ref, bgr_ref, ng_ref, stack_ref, lid_ref, grp_ref,
                       y_ref, sout_ref, st_ref, *, n_heads, scale, n_levels):
    j = pl.program_id(1)

    @pl.when(j == 0)
    def _():
        st_ref[...] = jnp.zeros_like(st_ref)

    dk = q_ref.shape[1] // n_heads
    dv = v_ref.shape[1] // n_heads
    gl = gl_ref[...]
    for h in range(n_heads):
        ks, vs = slice(h * dk, (h + 1) * dk), slice(h * dv, (h + 1) * dv)
        q = q_ref[:, ks].astype(F32) * scale
        k = k_ref[:, ks].astype(F32)
        v = v_ref[:, vs]
        la = _gla_log_decay(gl, wgk_ref[:, ks], bgr_ref[:, ks])
        sc, cum, total, group_total = _gla_block(q, k, la, stack_ref[...], lid_ref[...], grp_ref[...], n_levels)
        s_old = st_ref[h]
        o = _dot(sc, v) + _dot((q * jnp.exp(cum)).astype(BF16), s_old.astype(BF16))
        kd = (k * jnp.exp(total - cum)).astype(BF16)
        st_ref[h] = s_old * jnp.exp(group_total[:, 0:1]) + lax.dot_general(
            kd, v, TN_DIMS, preferred_element_type=F32)
        y_ref[:, vs] = _rms_gate(o, ng_ref[...], g_ref[:, vs].astype(F32))

    @pl.when(j == pl.num_programs(1) - 1)
    def _():
        sout_ref[...] = st_ref[...]


def _gla_sample_kernel(q_ref, k_ref, v_ref, g_ref, gl_ref, wgk_ref, bgr_ref, ng_ref, stack_ref, lid_ref, grp_ref,
                       sin_ref, y_ref, sout_ref, *, n_heads, scale, seq, n_levels):
    rows = q_ref.shape[0]
    nb = rows // seq
    dk = q_ref.shape[1] // n_heads
    dv = v_ref.shape[1] // n_heads
    gl = gl_ref[...]
    for h in range(n_heads):
        ks, vs = slice(h * dk, (h + 1) * dk), slice(h * dv, (h + 1) * dv)
        q = q_ref[:, ks].astype(F32) * scale
        k = k_ref[:, ks].astype(F32)
        v = v_ref[:, vs]
        la = _gla_log_decay(gl, wgk_ref[:, ks], bgr_ref[:, ks])
        sc, cum, total, group_total = _gla_block(q, k, la, stack_ref[...], lid_ref[...], grp_ref[...], n_levels)
        q_dec = q * jnp.exp(cum)
        kd = k * jnp.exp(total - cum)
        v32 = v.astype(F32)
        o_state = []
        for s in range(nb):
            rs = slice(s * seq, (s + 1) * seq)
            s_old = sin_ref[s, h]
            o_state.append(_dot(q_dec[rs, :].astype(BF16), s_old.astype(BF16)))
            sout_ref[s, h] = s_old * jnp.exp(group_total[:, s:s + 1]) + lax.dot_general(
                kd[rs, :].astype(BF16), v32[rs, :].astype(BF16), TN_DIMS, preferred_element_type=F32)
        o = _dot(sc, v) + jnp.concatenate(o_state, axis=0)
        y_ref[:, vs] = _rms_gate(o, ng_ref[...], g_ref[:, vs].astype(F32))


def _gla(p, gl, wgk, bg_row, ng, state_s, *, t_prompt, seq, n_heads, dk, dv, bk=256, nb=4):
    t = p.shape[0]
    bs, _, _, _ = state_s.shape
    s_len = (t - t_prompt) // bs
    bp = t_prompt // seq
    dkt, dvt = n_heads * dk, n_heads * dv
    assert dvt == 2 * dkt
    scale = dk ** -0.5
    r = gl.shape[1]
    const = lambda *_: (0, 0)

    def tables(rows, span):
        stack, lid, grp, n_levels = _gla_tables(rows, span)
        specs = [pl.BlockSpec((r, dkt), const), pl.BlockSpec((1, dkt), const), pl.BlockSpec((1, dv), const),
                 pl.BlockSpec(stack.shape, const), pl.BlockSpec(lid.shape, const), pl.BlockSpec(grp.shape, const)]
        return specs, [wgk, bg_row, ng, stack, lid, grp], n_levels

    nblk = seq // bk
    rowp = lambda b, j: b * nblk + j
    w_specs, w_ins, n_levels = tables(bk, bk)
    y_p, s_p = pl.pallas_call(
        functools.partial(_gla_prompt_kernel, n_heads=n_heads, scale=scale, n_levels=n_levels),
        grid=(bp, nblk),
        in_specs=[pl.BlockSpec((bk, dkt), lambda b, j: (rowp(b, j), 0)),
                  pl.BlockSpec((bk, dkt), lambda b, j: (rowp(b, j), 1)),
                  pl.BlockSpec((bk, dvt), lambda b, j: (rowp(b, j), 1)),
                  pl.BlockSpec((bk, dvt), lambda b, j: (rowp(b, j), 2)),
                  pl.BlockSpec((bk, r), lambda b, j: (rowp(b, j), 0))] + w_specs,
        out_specs=[pl.BlockSpec((bk, dvt), lambda b, j: (rowp(b, j), 0)),
                   pl.BlockSpec((None, n_heads, dk, dv), lambda b, j: (b, 0, 0, 0))],
        out_shape=[jax.ShapeDtypeStruct((t_prompt, dvt), BF16),
                   jax.ShapeDtypeStruct((bp, n_heads, dk, dv), F32)],
        scratch_shapes=[pltpu.VMEM((n_heads, dk, dv), F32)],
        compiler_params=_params("arbitrary", "arbitrary"),
        name="gla_prompt",
    )(p, p, p, p, gl, *w_ins)

    rows = nb * s_len
    off = t_prompt // rows
    w_specs, w_ins, n_levels = tables(rows, s_len)
    y_s, s_s = pl.pallas_call(
        functools.partial(_gla_sample_kernel, n_heads=n_heads, scale=scale, seq=s_len, n_levels=n_levels),
        grid=(bs // nb,),
        in_specs=[pl.BlockSpec((rows, dkt), lambda i: (off + i, 0)),
                  pl.BlockSpec((rows, dkt), lambda i: (off + i, 1)),
                  pl.BlockSpec((rows, dvt), lambda i: (off + i, 1)),
                  pl.BlockSpec((rows, dvt), lambda i: (off + i, 2)),
                  pl.BlockSpec((rows, r), lambda i: (off + i, 0))] + w_specs + [
                  pl.BlockSpec((nb, n_heads, dk, dv), lambda i: (i, 0, 0, 0))],
        out_specs=[pl.BlockSpec((rows, dvt), lambda i: (i, 0)),
                   pl.BlockSpec((nb, n_heads, dk, dv), lambda i: (i, 0, 0, 0))],
        out_shape=[jax.ShapeDtypeStruct((t - t_prompt, dvt), BF16),
                   jax.ShapeDtypeStruct(state_s.shape, F32)],
        compiler_params=_params("arbitrary"),
        name="gla_sample",
    )(p, p, p, p, gl, *w_ins, state_s)
    return (y_p, y_s), s_p, s_s


def _moe_schedule(eid, rank, counts, *, tg, n_tiles):
    n_e = counts.shape[0]
    e_ids = jnp.arange(n_e, dtype=I32)
    padded = ((counts + tg - 1) // tg) * tg
    ends = jnp.sum(jnp.where(e_ids[None, :] <= e_ids[:, None], padded[None, :], 0), axis=1)
    starts = ends - padded
    pos = jnp.sum(jnp.where(eid[None] == e_ids[:, None, None], starts[:, None, None], 0), axis=0) + rank
    n_used = ends[n_e - 1] // tg
    tile_start = jnp.arange(n_tiles, dtype=I32) * tg
    te = jnp.sum((ends[None, :] <= tile_start[:, None]).astype(I32), axis=1)
    te_last = jnp.sum((ends <= (n_used - 1) * tg).astype(I32))
    te = jnp.where(jnp.arange(n_tiles) < n_used, te, te_last)
    later = (e_ids[None, :] > e_ids[:, None]) & (counts[None, :] > 0)
    nxt = jnp.min(jnp.where(later, e_ids[None, :], n_e), axis=1)
    nxt = jnp.where(nxt == n_e, -1, nxt)
    i32 = lambda a: a.astype(I32)
    return (i32(pos.reshape(-1)), i32(starts + counts), i32(padded - counts), i32(te), i32(nxt),
            i32(n_used.reshape(1)))


def kernel(x_prompt, x_sample, cache_conv, state_gla, c_prompt, c_sample, w_mod, b_mod, ln_g, ln_b, ab_w_in, ab_conv_w, ab_v_ln_g, ab_v_ln_b, ab_w_s, ab_b_s, ab_w_out, gla_w_in, gla_w_gk, gla_b_gk, gla_norm_g, gla_w_out, moe_w_grp, moe_b_grp, moe_w_rt, moe_b_rt, moe_w1, moe_w3, moe_w2):
    bp, seq, d = x_prompt.shape
    bs, s_len, _ = x_sample.shape
    assert s_len == SUBLANES_V7X and seq % SUBLANES_V7X == 0
    depth = w_mod.shape[0]
    alpha = float((2 * depth) ** 0.25)
    t_p, t_s = bp * seq, bs * s_len
    t = t_p + t_s
    n_groups, n_exp = moe_w_rt.shape[1], moe_w_rt.shape[3]
    n_e = n_groups * n_exp
    d_ff = moe_w1.shape[-1]
    tg = 256
    n_tiles = (TOP_K_INNER * t) // tg + n_e

    x = (x_prompt.reshape(t_p // SUBLANES_V7X, SUBLANES_V7X, d), x_sample)
    mods = _mod_vectors(c_prompt, c_sample, w_mod, b_mod)

    w1 = moe_w1.reshape(depth * n_e, d, d_ff)
    w3 = moe_w3.reshape(depth * n_e, d, d_ff)
    w2 = moe_w2.reshape(depth * n_e, d_ff, d)

    conv_p, conv_s, chunk_v, gla_p, gla_s = [], [], [], [], []
    h_bf = None
    for layer in range(depth):
        li = layer // 2
        if layer % 2 == 0:
            n_heads, chunk = ab_w_s.shape[1], ab_w_s.shape[2]
            dc = ab_conv_w.shape[-1]
            if h_bf is None:
                p = _mm(_modulate(x, mods, layer, 0, 1, seq=seq), ab_w_in, li, ab_w_in.shape[-1])
            else:
                p = _mm(h_bf, ab_w_in, li, ab_w_in.shape[-1])
            w_s = ab_w_s[li]
            wm_p = jnp.tril(w_s)
            reps = chunk // s_len
            blk = jnp.tril(w_s[:, :s_len, :s_len])
            wm_s = jnp.einsum("ab,hts->hatbs", jnp.eye(reps, dtype=F32), blk).reshape(n_heads, chunk, chunk)
            wm = jnp.stack([wm_p, wm_s]).astype(BF16)
            b_s = ab_b_s[li]
            hd = dc // n_heads
            bias_p = jnp.repeat(b_s.T, hd, axis=1)
            bias_s = jnp.repeat(jnp.tile(b_s[:, :s_len].T, (reps, 1)), hd, axis=1)
            bias = jnp.stack([bias_p, bias_s])
            y, cp_new, cs_new, vn_s = _mix0(p, cache_conv[li], ab_conv_w[li], ab_v_ln_g[li], ab_v_ln_b[li],
                                            wm, bias, t_prompt=t_p, seq=seq, n_heads=n_heads)
            conv_p.append(cp_new)
            conv_s.append(cs_new)
            chunk_v.append(vn_s)
            w_out = ab_w_out[li].astype(BF16)
        else:
            n_heads, dk, dv = state_gla.shape[2], state_gla.shape[3], state_gla.shape[4]
            dkt, dvt = n_heads * dk, n_heads * dv
            rank = gla_w_gk.shape[1]
            n_main = 2 * dkt + 2 * dvt
            w_in_t = jnp.swapaxes(gla_w_in, 1, 2)
            p = _mm(h_bf, w_in_t, li, n_main, w_transposed=True)
            w_lo = jnp.pad(w_in_t[li, n_main:, :], ((0, LANES_V7X - rank), (0, 0)))[None]
            gl = _mm(h_bf, w_lo, 0, LANES_V7X, w_transposed=True)
            wgk = jnp.pad(gla_w_gk[li], ((0, LANES_V7X - rank), (0, 0))).astype(BF16)
            y, sp_new, ss_new = _gla(p, gl, wgk, gla_b_gk[li].reshape(1, dkt),
                                     gla_norm_g[li].reshape(1, dv), state_gla[li],
                                     t_prompt=t_p, seq=seq, n_heads=n_heads, dk=dk, dv=dv)
            gla_p.append(sp_new)
            gla_s.append(ss_new)
            w_out = gla_w_out[li].astype(BF16)

        wr = jnp.concatenate([moe_w_grp[layer].T,
                              jnp.transpose(moe_w_rt[layer], (0, 2, 1)).reshape(n_e, d)], axis=0)
        wr = jnp.pad(wr, ((0, LANES_V7X - wr.shape[0]), (0, 0)))
        br = jnp.concatenate([moe_b_grp[layer], moe_b_rt[layer].reshape(n_e)])
        br = jnp.pad(br, (0, LANES_V7X - br.shape[0])).reshape(LANES_V7X, 1)
        x1, h2, eid, wt, rank_, cnt = _outln(y, w_out, x, mods, layer, ln_g[layer, 0], ln_b[layer, 0], wr, br,
                                             alpha=alpha, n_groups=n_groups, n_exp=n_exp, t_prompt=t_p, seq=seq)
        pos, zero_start, zero_count, te, nxt, n_used = _moe_schedule(eid, rank_, cnt[:, 0], tg=tg, n_tiles=n_tiles)
        xs = _dispatch(pos, zero_start, zero_count, n_used, h2, n_tiles=n_tiles, tg=tg)
        ys = _moe(xs, w1, w3, w2, layer * n_e, te, nxt, n_used, tg=tg)
        has_next = layer + 1 < depth
        outs = _comb(pos, ys, wt.T, x1, mods, layer, ln_g[layer, 1], ln_b[layer, 1],
                     alpha=alpha, has_next=has_next, t_prompt=t_p, seq=seq)
        if has_next:
            x, h_bf = outs

    y_prompt = outs[0].reshape(bp, seq, d)
    y_sample = outs[1].reshape(bs, s_len, d)
    return (y_prompt, y_sample, jnp.stack(conv_p), jnp.stack(conv_s), jnp.stack(chunk_v),
            jnp.stack(gla_p), jnp.stack(gla_s))
```

```python
import functools

import jax
import numpy as np
import jax.numpy as jnp
from jax import lax
from jax.experimental import pallas as pl
from jax.experimental.pallas import tpu as pltpu

F32 = jnp.float32
BF16 = jnp.bfloat16
I32 = jnp.int32

LN_EPS = 1e-5
RMS_EPS = 1e-6
GLA_GATE_NORMALIZER = 16.0
TOP_K_INNER = 2

SUBLANES_V7X = 8
LANES_V7X = 128
VMEM_LIMIT_V7X = 56 * 1024 * 1024

NT_DIMS = (((1,), (1,)), ((), ()))
TN_DIMS = (((0,), (0,)), ((), ()))


def _params(*sem):
    return pltpu.CompilerParams(dimension_semantics=sem, vmem_limit_bytes=VMEM_LIMIT_V7X)


def _silu(x):
    return x * (1.0 / (1.0 + jnp.exp(-x)))


def _log_sigmoid(z):
    return jnp.minimum(z, 0.0) - jnp.log(1.0 + jnp.exp(-jnp.abs(z)))


def _layer_norm(x, g, b):
    mu = jnp.mean(x, axis=-1, keepdims=True)
    xc = x - mu
    var = jnp.mean(xc * xc, axis=-1, keepdims=True)
    return xc * lax.rsqrt(var + LN_EPS) * g + b


def _dot(a, b):
    return jnp.dot(a, b, preferred_element_type=F32)


def _split_bf16(x):
    hi = x.astype(BF16)
    lo = (x - hi.astype(F32)).astype(BF16)
    return hi, lo


def _mod_kernel(c_ref, w_ref, b_ref, op_ref, os_ref):
    r = _dot(_silu(c_ref[...]).astype(BF16), w_ref[...].astype(BF16)) + b_ref[...]
    n_p = op_ref.shape[0]
    op_ref[...] = r[:n_p, :]
    os_ref[...] = r[n_p:, :]


def _mod_vectors(c_prompt, c_sample, w_mod, b_mod):
    depth, d, n = w_mod.shape
    bp, bs = c_prompt.shape[0], c_sample.shape[0]
    bp_rows = bp + (-bp) % SUBLANES_V7X
    c_all = jnp.concatenate([jnp.pad(c_prompt, ((0, bp_rows - bp), (0, 0))), c_sample], axis=0)
    tn = 1024
    return pl.pallas_call(
        _mod_kernel,
        grid=(depth, n // tn),
        in_specs=[
            pl.BlockSpec(c_all.shape, lambda l, j: (0, 0)),
            pl.BlockSpec((None, d, tn), lambda l, j: (l, 0, j)),
            pl.BlockSpec((None, 1, tn), lambda l, j: (l, 0, j)),
        ],
        out_specs=[pl.BlockSpec((None, bp_rows, tn), lambda l, j: (l, 0, j)),
                   pl.BlockSpec((None, bs, tn), lambda l, j: (l, 0, j))],
        out_shape=[jax.ShapeDtypeStruct((depth, bp_rows, n), F32), jax.ShapeDtypeStruct((depth, bs, n), F32)],
        compiler_params=_params("arbitrary", "arbitrary"),
        name="mod",
    )(c_all, w_mod, b_mod.reshape(depth, 1, n))


def _mod_specs(mods, layer, col, d, gt, n_p):
    mp, _ = mods
    return [pl.BlockSpec((None, mp.shape[1], d), lambda i, *_: (layer, 0, col)),
            pl.BlockSpec((None, gt, d), lambda i, *_: (layer, jnp.maximum(i - n_p, 0), col))]


def _mod_value(p_ref, s_ref, prompt, seq_idx):
    if prompt:
        return p_ref[pl.ds(seq_idx, 1), :][None]
    return s_ref[...][:, None, :]


def _by_group(i, n_prompt_tiles, fn):
    pl.when(i < n_prompt_tiles)(lambda: fn(True))
    pl.when(i >= n_prompt_tiles)(lambda: fn(False))


def _mm_kernel(a_ref, w_ref, o_ref, wb_ref, *, w_transposed):
    @pl.when(pl.program_id(1) == 0)
    def _():
        w = w_ref[...]
        wb_ref[...] = (w.T if w_transposed else w).astype(BF16)

    o_ref[...] = _dot(a_ref[...], wb_ref[...]).astype(o_ref.dtype)


def _mm(a, w3, w_idx, n_out, *, tm=1024, tn=1024, out_dtype=BF16, w_transposed=False):
    t = a.shape[0]
    k = w3.shape[2 if w_transposed else 1]
    tn = min(tn, n_out)
    return pl.pallas_call(
        functools.partial(_mm_kernel, w_transposed=w_transposed),
        grid=(n_out // tn, t // tm),
        in_specs=[pl.BlockSpec((tm, k), lambda j, i: (i, 0)),
                  pl.BlockSpec((None, tn, k), lambda j, i: (w_idx, j, 0)) if w_transposed
                  else pl.BlockSpec((None, k, tn), lambda j, i: (w_idx, 0, j))],
        out_specs=pl.BlockSpec((tm, tn), lambda j, i: (i, j)),
        out_shape=jax.ShapeDtypeStruct((t, n_out), out_dtype),
        scratch_shapes=[pltpu.VMEM((k, tn), BF16)],
        compiler_params=_params("arbitrary", "arbitrary"),
        name="mm",
    )(a, w3)


def _modulate_kernel(xp_ref, xs_ref, shp_ref, shs_ref, scp_ref, scs_ref, o_ref, *, n_prompt_tiles, tiles_per_seq):
    i = pl.program_id(0)

    def run(prompt):
        x = (xp_ref if prompt else xs_ref)[...]
        g, s, k = x.shape
        seq_idx = i // tiles_per_seq
        h = x * (1.0 + _mod_value(scp_ref, scs_ref, prompt, seq_idx)) + _mod_value(shp_ref, shs_ref, prompt, seq_idx)
        o_ref[...] = h.reshape(g * s, k).astype(BF16)

    _by_group(i, n_prompt_tiles, run)


def _modulate(x_pair, mods, layer, sh_col, sc_col, *, seq, tm=512):
    xp, xs = x_pair
    k = xp.shape[-1]
    gt = tm // SUBLANES_V7X
    n_p = xp.shape[0] // gt
    t = (xp.shape[0] + xs.shape[0]) * SUBLANES_V7X
    return pl.pallas_call(
        functools.partial(_modulate_kernel, n_prompt_tiles=n_p, tiles_per_seq=seq // tm),
        grid=(t // tm,),
        in_specs=[pl.BlockSpec((gt, SUBLANES_V7X, k), lambda i: (jnp.minimum(i, n_p - 1), 0, 0)),
                  pl.BlockSpec((gt, SUBLANES_V7X, k), lambda i: (jnp.maximum(i - n_p, 0), 0, 0))]
                 + _mod_specs(mods, layer, sh_col, k, gt, n_p) + _mod_specs(mods, layer, sc_col, k, gt, n_p),
        out_specs=pl.BlockSpec((tm, k), lambda i: (i, 0)),
        out_shape=jax.ShapeDtypeStruct((t, k), BF16),
        compiler_params=_params("arbitrary"),
        name="modulate",
    )(xp, xs, *mods, *mods)


def _mix0_kernel(bg_ref, cg_ref, hx_ref, u_ref, v_ref, cache_ref, cw_ref, vg_ref, vb_ref, wm_ref, bias_ref,
                 y_ref, convp_ref, convs_ref, vns_ref, zprev_ref, *, n_prompt_tiles, tiles_per_seq, n_heads):
    i = pl.program_id(0)
    tm, dc = bg_ref.shape
    ns = tm // SUBLANES_V7X
    z = cg_ref[...].astype(F32) * hx_ref[...].astype(F32)
    row = lax.broadcasted_iota(I32, (tm, dc), 0)
    r1 = pltpu.roll(z, 1, 0)
    r2 = pltpu.roll(z, 2, 0)
    cw = cw_ref[...]
    bg = bg_ref[...].astype(F32)

    vn = _layer_norm(v_ref[...].astype(F32), vg_ref[...], vb_ref[...])
    vnb = vn.astype(BF16)
    hd = dc // n_heads
    mixed = jnp.concatenate(
        [_dot(wm_ref[h], vnb[:, h * hd:(h + 1) * hd]) for h in range(n_heads)], axis=-1) + bias_ref[...]
    y_ref[:, dc:] = (u_ref[...].astype(F32) * mixed).astype(BF16)

    def conv_out(zm1, zm2):
        conv = cw[0:1, :] * zm2 + cw[1:2, :] * zm1 + cw[2:3, :] * z
        y_ref[:, :dc] = (bg * conv).astype(BF16)

    @pl.when(i < n_prompt_tiles)
    def _prompt():
        @pl.when(i % tiles_per_seq == 0)
        def _():
            zprev_ref[...] = jnp.zeros_like(zprev_ref)

        zp = zprev_ref[...]
        p1 = zp[SUBLANES_V7X - 1:SUBLANES_V7X, :]
        p2 = zp[SUBLANES_V7X - 2:SUBLANES_V7X - 1, :]
        conv_out(jnp.where(row == 0, p1, r1),
                 jnp.where(row == 0, p2, jnp.where(row == 1, p1, r2)))
        zprev_ref[...] = z[tm - SUBLANES_V7X:, :]
        convp_ref[...] = z[tm - 2:, :].reshape(1, 2, dc)

    @pl.when(i >= n_prompt_tiles)
    def _sample():
        c = cache_ref[...]
        c0 = jnp.broadcast_to(c[:, 0:1, :], (ns, SUBLANES_V7X, dc)).reshape(tm, dc)
        c1 = jnp.broadcast_to(c[:, 1:2, :], (ns, SUBLANES_V7X, dc)).reshape(tm, dc)
        rr = row % SUBLANES_V7X
        conv_out(jnp.where(rr == 0, c1, r1),
                 jnp.where(rr == 0, c0, jnp.where(rr == 1, c1, r2)))
        z3 = z.reshape(ns, SUBLANES_V7X, dc)
        convs_ref[...] = z3[:, SUBLANES_V7X - 2:, :]
        vns_ref[...] = vn.reshape(ns, SUBLANES_V7X, dc)


def _mix0(p, cache, conv_w, v_g, v_b, wm, bias, *, t_prompt, seq, n_heads):
    t, n = p.shape
    bs, cwm1, dc = cache.shape
    tm = wm.shape[-1]
    assert cwm1 == 2 and conv_w.shape[0] == 3 and n == 5 * dc and seq % tm == 0
    n_p = t_prompt // tm
    n_s = (t - t_prompt) // tm
    tps = seq // tm
    bp = t_prompt // seq
    ns = tm // SUBLANES_V7X

    def col(c):
        return pl.BlockSpec((tm, dc), lambda i: (i, c))

    def s_idx(i):
        return jnp.maximum(i - n_p, 0)

    const2 = lambda i: (0, 0)
    mode = lambda i: ((i >= n_p).astype(I32), 0, 0, 0)
    return pl.pallas_call(
        functools.partial(_mix0_kernel, n_prompt_tiles=n_p, tiles_per_seq=tps, n_heads=n_heads),
        grid=(n_p + n_s,),
        in_specs=[col(0), col(1), col(2), col(3), col(4),
                  pl.BlockSpec((ns, 2, dc), lambda i: (s_idx(i), 0, 0)),
                  pl.BlockSpec((3, dc), const2),
                  pl.BlockSpec((1, dc), const2),
                  pl.BlockSpec((1, dc), const2),
                  pl.BlockSpec((None, n_heads, tm, tm), mode),
                  pl.BlockSpec((None, tm, dc), lambda i: ((i >= n_p).astype(I32), 0, 0))],
        out_specs=[pl.BlockSpec((tm, 2 * dc), lambda i: (i, 0)),
                   pl.BlockSpec((1, 2, dc), lambda i: (jnp.minimum(i // tps, bp - 1), 0, 0)),
                   pl.BlockSpec((ns, 2, dc), lambda i: (s_idx(i), 0, 0)),
                   pl.BlockSpec((ns, SUBLANES_V7X, dc), lambda i: (s_idx(i), 0, 0))],
        out_shape=[jax.ShapeDtypeStruct((t, 2 * dc), BF16),
                   jax.ShapeDtypeStruct((bp, 2, dc), F32),
                   jax.ShapeDtypeStruct((bs, 2, dc), F32),
                   jax.ShapeDtypeStruct((bs, SUBLANES_V7X, dc), F32)],
        scratch_shapes=[pltpu.VMEM((SUBLANES_V7X, dc), F32)],
        compiler_params=_params("arbitrary"),
        name="mix0",
    )(p, p, p, p, p, cache, conv_w, v_g.reshape(1, dc), v_b.reshape(1, dc), wm, bias)


def _first_index_of(vals, target):
    idx = jnp.full(target.shape, len(vals) - 1, I32)
    for j in reversed(range(len(vals))):
        idx = jnp.where(vals[j] == target, j, idx)
    return idx


def _softmax_rows(rows):
    m = functools.reduce(jnp.maximum, rows)
    e = [jnp.exp(r - m) for r in rows]
    s = functools.reduce(lambda a, b: a + b, e)
    return [x / s for x in e]


def _route(h, wr_ref, br_ref, eid_ref, wt_ref, rank_ref, cnt_ref, carry_ref, n_groups, n_exp):
    tm = h.shape[0]
    hh, hl = _split_bf16(h)
    wh, wl = _split_bf16(wr_ref[...])
    dg = lambda a, b: lax.dot_general(a, b, NT_DIMS, preferred_element_type=F32)
    logits = dg(wh, hh) + dg(wh, hl) + dg(wl, hh) + br_ref[...]

    g_prob = _softmax_rows([logits[g:g + 1, :] for g in range(n_groups)])
    g_top = functools.reduce(jnp.maximum, g_prob)
    g_idx = _first_index_of(g_prob, g_top)

    e_sel = []
    for e in range(n_exp):
        sel = logits[n_groups + e:n_groups + e + 1, :]
        for g in range(1, n_groups):
            r = n_groups + g * n_exp + e
            sel = jnp.where(g_idx == g, logits[r:r + 1, :], sel)
        e_sel.append(sel)
    e_prob = _softmax_rows(e_sel)
    p1 = functools.reduce(jnp.maximum, e_prob)
    i1 = _first_index_of(e_prob, p1)
    rest = [jnp.where(i1 == e, -1.0, e_prob[e]) for e in range(n_exp)]
    p2 = functools.reduce(jnp.maximum, rest)
    i2 = _first_index_of(rest, p2)
    den = p1 + p2
    wt_ref[0:1, :] = g_top * (p1 / den)
    wt_ref[1:2, :] = g_top * (p2 / den)
    eid0 = g_idx * n_exp + i1
    eid1 = g_idx * n_exp + i2
    eid_ref[0:1, :] = eid0
    eid_ref[1:2, :] = eid1

    n_e = n_groups * n_exp
    eio = lax.broadcasted_iota(I32, (n_e, tm), 0)
    oh0 = (eio == eid0).astype(F32)
    oh1 = (eio == eid1).astype(F32)
    oh = oh0 + oh1
    before = (lax.broadcasted_iota(I32, (tm, tm), 0) < lax.broadcasted_iota(I32, (tm, tm), 1)).astype(BF16)
    base = _dot(oh.astype(BF16), before) + carry_ref[...]
    rank_ref[0:1, :] = jnp.sum(oh0 * base, axis=0, keepdims=True).astype(I32)
    rank_ref[1:2, :] = jnp.sum(oh1 * base, axis=0, keepdims=True).astype(I32)
    total = carry_ref[...] + jnp.sum(oh, axis=1, keepdims=True)
    carry_ref[...] = total
    cnt_ref[...] = jnp.broadcast_to(total, cnt_ref.shape).astype(I32)


def _outln_kernel(*refs, alpha, n_groups, n_exp, y_split, x_split, n_prompt_tiles, tiles_per_seq):
    refs = list(refs)
    i = pl.program_id(0)
    take = lambda n: [refs.pop(0) for _ in range(n)]
    y_refs, (w_ref,), x_refs = take(2 if y_split else 1), take(1), take(2 if x_split else 1)
    gt_refs, sh_refs, sc_refs = take(2), take(2), take(2)
    (lng_ref, lnb_ref, wr_ref, br_ref, x1_ref, h_ref, eid_ref, wt_ref, rank_ref, cnt_ref, carry_ref) = refs

    @pl.when(i == 0)
    def _():
        carry_ref[...] = jnp.zeros_like(carry_ref)

    def run(prompt):
        pick = lambda pair: pair[0 if prompt or len(pair) == 1 else 1]
        mod = lambda pair: _mod_value(*pair, prompt, i // tiles_per_seq)
        x = pick(x_refs)[...]
        g, s, d = x.shape
        m = _dot(pick(y_refs)[...], w_ref[...]).reshape(g, s, d)
        x1 = _layer_norm(alpha * x + mod(gt_refs) * m, lng_ref[...], lnb_ref[...])
        x1_ref[...] = x1
        h = (x1 * (1.0 + mod(sc_refs)) + mod(sh_refs)).reshape(g * s, d)
        h_ref[...] = h
        _route(h, wr_ref, br_ref, eid_ref, wt_ref, rank_ref, cnt_ref, carry_ref, n_groups, n_exp)

    _by_group(i, n_prompt_tiles, run)


def _outln(y, w_bf, x, mods, layer, ln_g, ln_b, wr, br, *, alpha, n_groups, n_exp, t_prompt, seq, tm=512):
    y_split, x_split = isinstance(y, tuple), isinstance(x, tuple)
    ys, xs = (y if y_split else (y,)), (x if x_split else (x,))
    k, d = w_bf.shape
    gt = tm // SUBLANES_V7X
    n_e = n_groups * n_exp
    t = sum(a.shape[0] for a in ys)
    n_p = t_prompt // tm

    def split_specs(block, n_arrays):
        if n_arrays == 1:
            return [pl.BlockSpec(block, lambda i: (i,) + (0,) * (len(block) - 1))]
        return [pl.BlockSpec(block, lambda i: (jnp.minimum(i, n_p - 1),) + (0,) * (len(block) - 1)),
                pl.BlockSpec(block, lambda i: (jnp.maximum(i - n_p, 0),) + (0,) * (len(block) - 1))]

    mod = lambda c: _mod_specs(mods, layer, c, d, gt, n_p)
    const = lambda i: (0, 0)
    pair = lambda dt: jax.ShapeDtypeStruct((TOP_K_INNER, t), dt)
    pair_spec = pl.BlockSpec((TOP_K_INNER, tm), lambda i: (0, i))
    return pl.pallas_call(
        functools.partial(_outln_kernel, alpha=alpha, n_groups=n_groups, n_exp=n_exp,
                          y_split=y_split, x_split=x_split, n_prompt_tiles=n_p, tiles_per_seq=seq // tm),
        grid=(t // tm,),
        in_specs=split_specs((tm, k), len(ys))
                 + [pl.BlockSpec((k, d), const, pipeline_mode=pl.Buffered(1))]
                 + split_specs((gt, SUBLANES_V7X, d), len(xs))
                 + mod(2) + mod(3) + mod(4)
                 + [pl.BlockSpec((1, d), const),
                    pl.BlockSpec((1, d), const),
                    pl.BlockSpec(wr.shape, const),
                    pl.BlockSpec(br.shape, const)],
        out_specs=[pl.BlockSpec((gt, SUBLANES_V7X, d), lambda i: (i, 0, 0)),
                   pl.BlockSpec((tm, d), lambda i: (i, 0)),
                   pair_spec, pair_spec, pair_spec,
                   pl.BlockSpec((n_e, LANES_V7X), const)],
        out_shape=[jax.ShapeDtypeStruct((t // SUBLANES_V7X, SUBLANES_V7X, d), F32),
                   jax.ShapeDtypeStruct((t, d), F32),
                   pair(I32), pair(F32), pair(I32),
                   jax.ShapeDtypeStruct((n_e, LANES_V7X), I32)],
        scratch_shapes=[pltpu.VMEM((n_e, 1), F32)],
        compiler_params=_params("arbitrary"),
        name="outln",
    )(*ys, w_bf, *xs, *mods, *mods, *mods, ln_g.reshape(1, d), ln_b.reshape(1, d), wr, br)


ROW_DMA_UNROLL = 8


def _row_gather_start(src_hbm, dst, sem, idx_ref, base, n):
    def body(q, c):
        for u in range(ROW_DMA_UNROLL):
            r = q * ROW_DMA_UNROLL + u
            row = idx_ref[base + r]
            pltpu.make_async_copy(src_hbm.at[pl.ds(row, 1)], dst.at[pl.ds(r, 1)], sem).start(priority=u % 2)
        return c
    lax.fori_loop(0, n // ROW_DMA_UNROLL, body, 0)


def _row_gather_wait(src_hbm, dst, sem, n):
    pltpu.make_async_copy(src_hbm.at[pl.ds(0, n)], dst, sem).wait()


def _dispatch_kernel(pos_ref, zs_ref, zn_ref, nu_ref, h_ref, xs_hbm, zrow_ref, ztile_ref, sem, zsem,
                     *, t_total, n_e, first_free_tile):
    i = pl.program_id(0)
    tm = h_ref.shape[0]
    tg = ztile_ref.shape[0]
    n_tiles = xs_hbm.shape[0] // tg
    n_used = nu_ref[0]

    @pl.when(i == 0)
    def _():
        zrow_ref[...] = jnp.zeros_like(zrow_ref)
        ztile_ref[...] = jnp.zeros_like(ztile_ref)
        pad_row = lambda p: pltpu.make_async_copy(zrow_ref.at[pl.ds(0, 1)], xs_hbm.at[pl.ds(p, 1)], zsem)
        tail_tile = lambda j: pltpu.make_async_copy(ztile_ref, xs_hbm.at[pl.ds(j * tg, tg)], zsem)

        def pads(do):
            for e in range(n_e):
                base = zs_ref[e]

                def body(r, c):
                    do(pad_row(base + r))
                    return c
                lax.fori_loop(0, zn_ref[e], body, 0)
            for j in range(first_free_tile, n_tiles):
                @pl.when(j >= n_used)
                def _():
                    do(tail_tile(j))

        pads(lambda cp: cp.start())
        pads(lambda cp: cp.wait())

    for k in range(TOP_K_INNER):
        def body(q, c):
            for u in range(ROW_DMA_UNROLL):
                r = q * ROW_DMA_UNROLL + u
                p = pos_ref[k * t_total + i * tm + r]
                pltpu.make_async_copy(h_ref.at[pl.ds(r, 1)], xs_hbm.at[pl.ds(p, 1)], sem).start(priority=u % 2)
            return c
        lax.fori_loop(0, tm // ROW_DMA_UNROLL, body, 0)
    for k in range(TOP_K_INNER):
        pltpu.make_async_copy(h_ref, xs_hbm.at[pl.ds(0, tm)], sem).wait()


def _dispatch(pos, zero_start, zero_count, n_used, h, *, n_tiles, tg, tm=1024):
    t, d = h.shape
    n_e = zero_start.shape[0]
    grid_spec = pltpu.PrefetchScalarGridSpec(
        num_scalar_prefetch=4,
        grid=(t // tm,),
        in_specs=[pl.BlockSpec((tm, d), lambda i, *_: (i, 0))],
        out_specs=pl.BlockSpec(memory_space=pl.ANY),
        scratch_shapes=[pltpu.VMEM((SUBLANES_V7X, d), h.dtype), pltpu.VMEM((tg, d), h.dtype),
                        pltpu.SemaphoreType.DMA(()), pltpu.SemaphoreType.DMA(())],
    )
    return pl.pallas_call(
        functools.partial(_dispatch_kernel, t_total=t, n_e=n_e, first_free_tile=(TOP_K_INNER * t) // tg),
        grid_spec=grid_spec,
        out_shape=jax.ShapeDtypeStruct((n_tiles * tg, d), h.dtype),
        compiler_params=_params("arbitrary"),
        name="dispatch",
    )(pos, zero_start, zero_count, n_used, h)


def _moe_kernel(te_ref, nxt_ref, nu_ref, xs_ref, w1_hbm, w3_hbm, w2_hbm, o_ref,
                w1s, w3s, w2s, w1b, w3b, w2b, wslot_ref, wsem, *, w_base):
    i = pl.program_id(0)
    n_used = nu_ref[0]

    def weight_copies(e, slot):
        return [pltpu.make_async_copy(hbm.at[w_base + e], stage.at[slot], wsem.at[slot])
                for hbm, stage in ((w1_hbm, w1s), (w3_hbm, w3s), (w2_hbm, w2s))]

    @pl.when(i == 0)
    def _():
        wslot_ref[0] = 1
        for cp in weight_copies(te_ref[0], 0):
            cp.start()

    @pl.when(i < n_used)
    def _():
        e = te_ref[i]

        @pl.when((i == 0) | (e != te_ref[jnp.maximum(i - 1, 0)]))
        def _():
            ws = 1 - wslot_ref[0]
            wslot_ref[0] = ws
            for cp in weight_copies(e, ws):
                cp.wait()
            w1b[...] = w1s[ws].astype(BF16)
            w3b[...] = w3s[ws].astype(BF16)
            w2b[...] = w2s[ws].astype(BF16)
            ne = nxt_ref[e]

            @pl.when(ne >= 0)
            def _():
                for cp in weight_copies(ne, 1 - ws):
                    cp.start()

        x = xs_ref[...].astype(BF16)
        a = _dot(x, w1b[...])
        b = _dot(x, w3b[...])
        o_ref[...] = _dot((_silu(a) * b).astype(BF16), w2b[...])

    @pl.when(i >= n_used)
    def _():
        o_ref[...] = jnp.zeros_like(o_ref)


def _moe(xs, w1, w3, w2, w_base, te, nxt, n_used, *, tg):
    p_tot, d = xs.shape
    f = w1.shape[-1]
    any_spec = pl.BlockSpec(memory_space=pl.ANY)
    grid_spec = pltpu.PrefetchScalarGridSpec(
        num_scalar_prefetch=3,
        grid=(p_tot // tg,),
        in_specs=[pl.BlockSpec((tg, d), lambda i, te, nxt, nu: (jnp.minimum(i, nu[0] - 1), 0)),
                  any_spec, any_spec, any_spec],
        out_specs=pl.BlockSpec((tg, d), lambda i, *_: (i, 0)),
        scratch_shapes=[pltpu.VMEM((2, d, f), F32), pltpu.VMEM((2, d, f), F32), pltpu.VMEM((2, f, d), F32),
                        pltpu.VMEM((d, f), BF16), pltpu.VMEM((d, f), BF16), pltpu.VMEM((f, d), BF16),
                        pltpu.SMEM((1,), I32), pltpu.SemaphoreType.DMA((2,))],
    )
    return pl.pallas_call(
        functools.partial(_moe_kernel, w_base=w_base),
        grid_spec=grid_spec,
        out_shape=jax.ShapeDtypeStruct((p_tot, d), F32),
        compiler_params=_params("arbitrary"),
        name="moe",
    )(te, nxt, n_used, xs, w1, w3, w2)


def _comb_kernel(pos_ref, y_hbm, wt_ref, x_ref, gtp_ref, gts_ref, lng_ref, lnb_ref, *rest,
                 alpha, tm, t_total, has_next, n_prompt_tiles, tiles_per_seq):
    if has_next:
        shp_ref, shs_ref, scp_ref, scs_ref, x2_ref, hn_ref, ybuf, sem = rest
    else:
        x2p_ref, x2s_ref, ybuf, sem = rest
    i = pl.program_id(0)
    n = pl.num_programs(0)

    def start(tile, slot):
        for k in range(TOP_K_INNER):
            _row_gather_start(y_hbm, ybuf.at[slot, k], sem.at[slot], pos_ref, k * t_total + tile * tm, tm)

    @pl.when(i == 0)
    def _():
        start(0, 0)

    slot = i % 2

    @pl.when(i + 1 < n)
    def _():
        start(i + 1, 1 - slot)

    for k in range(TOP_K_INNER):
        _row_gather_wait(y_hbm, ybuf.at[slot, k], sem.at[slot], tm)

    def run(prompt):
        seq_idx = i // tiles_per_seq
        w = wt_ref[...]
        f = w[:, 0:1] * ybuf[slot, 0] + w[:, 1:2] * ybuf[slot, 1]
        g, s, d = x_ref.shape
        x2 = _layer_norm(alpha * x_ref[...] + _mod_value(gtp_ref, gts_ref, prompt, seq_idx) * f.reshape(g, s, d),
                         lng_ref[...], lnb_ref[...])
        if has_next:
            x2_ref[...] = x2
            hn = (x2 * (1.0 + _mod_value(scp_ref, scs_ref, prompt, seq_idx))
                  + _mod_value(shp_ref, shs_ref, prompt, seq_idx))
            hn_ref[...] = hn.reshape(g * s, d).astype(BF16)
        else:
            (x2p_ref if prompt else x2s_ref)[...] = x2

    _by_group(i, n_prompt_tiles, run)


def _comb(pos_flat, y_sorted, wt_t, x1, mods, layer, ln_g, ln_b, *, alpha, has_next, t_prompt, seq, tm=256):
    g_total, s, d = x1.shape
    t = g_total * s
    gt = tm // SUBLANES_V7X
    n_p = t_prompt // tm
    mod = lambda l, c: _mod_specs(mods, l, c, d, gt, n_p)
    xspec = pl.BlockSpec((gt, s, d), lambda i, pos: (i, 0, 0))
    vec = pl.BlockSpec((1, d), lambda i, pos: (0, 0))
    in_specs = [pl.BlockSpec(memory_space=pl.ANY),
                pl.BlockSpec((tm, TOP_K_INNER), lambda i, pos: (i, 0)),
                xspec] + mod(layer, 5) + [vec, vec]
    ins = [y_sorted, wt_t, x1, *mods, ln_g.reshape(1, d), ln_b.reshape(1, d)]
    if has_next:
        in_specs += mod(layer + 1, 0) + mod(layer + 1, 1)
        ins += [*mods, *mods]
        out_specs = [xspec, pl.BlockSpec((tm, d), lambda i, pos: (i, 0))]
        out_shape = [jax.ShapeDtypeStruct(x1.shape, F32), jax.ShapeDtypeStruct((t, d), BF16)]
    else:
        out_specs = [pl.BlockSpec((gt, s, d), lambda i, pos: (jnp.minimum(i, n_p - 1), 0, 0)),
                     pl.BlockSpec((gt, s, d), lambda i, pos: (jnp.maximum(i - n_p, 0), 0, 0))]
        out_shape = [jax.ShapeDtypeStruct((t_prompt // s, s, d), F32),
                     jax.ShapeDtypeStruct(((t - t_prompt) // s, s, d), F32)]
    grid_spec = pltpu.PrefetchScalarGridSpec(
        num_scalar_prefetch=1,
        grid=(t // tm,),
        in_specs=in_specs,
        out_specs=out_specs,
        scratch_shapes=[pltpu.VMEM((2, TOP_K_INNER, tm, d), F32), pltpu.SemaphoreType.DMA((2,))],
    )
    return pl.pallas_call(
        functools.partial(_comb_kernel, alpha=alpha, tm=tm, t_total=t, has_next=has_next, n_prompt_tiles=n_p,
                          tiles_per_seq=seq // tm),
        grid_spec=grid_spec,
        out_shape=out_shape,
        compiler_params=_params("arbitrary"),
        name="comb",
    )(pos_flat, *ins)


def _gla_tables(rows, span):
    t = np.arange(rows)[:, None]
    u = np.arange(rows)[None, :]
    same = (t // span) == (u // span)
    mats = [same & (u <= t), same]
    n_levels, m = 0, 1
    while m < span:
        mid = (t // (2 * m)) * (2 * m) + m - 1
        mats.append(same & np.where(t > mid, (u > mid) & (u <= t), (u > t) & (u <= mid)))
        m *= 2
        n_levels += 1
    x = t ^ u
    level = np.where(x > 0, np.floor(np.log2(np.maximum(x, 1))), n_levels).astype(np.int32)
    lid = np.where(same & (u <= t), level, -1).astype(np.int32)
    grp = (t // span) == np.arange(LANES_V7X)[None, :]
    stack = np.concatenate(mats, axis=0).astype(np.float32)
    return jnp.asarray(stack, BF16), jnp.asarray(lid), jnp.asarray(grp.astype(np.float32), BF16), n_levels


def _gla_block(q, k, la, stack, lid, grp, n_levels):
    rows = q.shape[0]
    hi, lo = _split_bf16(la)
    dall = _dot(stack, hi)
    d_lo = _dot(stack[0:2 * rows, :], lo)
    cum, total = dall[0:rows] + d_lo[0:rows], dall[rows:2 * rows] + d_lo[rows:2 * rows]
    group_total = (lax.dot_general(hi, grp, TN_DIMS, preferred_element_type=F32)
                   + lax.dot_general(lo, grp, TN_DIMS, preferred_element_type=F32))
    nt = lambda a, b: lax.dot_general(a.astype(BF16), b.astype(BF16), NT_DIMS, preferred_element_type=F32)
    sc = jnp.where(lid == n_levels, nt(q, k), 0.0)
    for l in range(n_levels):
        e = jnp.exp(dall[(2 + l) * rows:(3 + l) * rows])
        sc = jnp.where(lid == l, nt(q * e, k * e), sc)
    return sc.astype(BF16), cum, total, group_total


def _gla_log_decay(gl, wgk, bg_row):
    return _log_sigmoid(_dot(gl, wgk) + bg_row) * (1.0 / GLA_GATE_NORMALIZER)


def _rms_gate(o, ng, gate):
    on = o * lax.rsqrt(jnp.mean(o * o, axis=-1, keepdims=True) + RMS_EPS) * ng
    return (on * _silu(gate)).astype(BF16)


def _gla_prompt_kernel(q_ref, k_ref, v_ref, g_ref, gl_ref, wgk_ref, bgr_ref, ng_ref, stack_ref, lid_ref, grp_ref,
                       y_ref, sout_ref, st_ref, *, n_heads, scale, n_levels):
    j = pl.program_id(1)

    @pl.when(j == 0)
    def _():
        st_ref[...] = jnp.zeros_like(st_ref)

    dk = q_ref.shape[1] // n_heads
    dv = v_ref.shape[1] // n_heads
    gl = gl_ref[...]
    for h in range(n_heads):
        ks, vs = slice(h * dk, (h + 1) * dk), slice(h * dv, (h + 1) * dv)
        q = q_ref[:, ks].astype(F32) * scale
        k = k_ref[:, ks].astype(F32)
        v = v_ref[:, vs]
        la = _gla_log_decay(gl, wgk_ref[:, ks], bgr_ref[:, ks])
        sc, cum, total, group_total = _gla_block(q, k, la, stack_ref[...], lid_ref[...], grp_ref[...], n_levels)
        s_old = st_ref[h]
        o = _dot(sc, v) + _dot((q * jnp.exp(cum)).astype(BF16), s_old.astype(BF16))
        kd = (k * jnp.exp(total - cum)).astype(BF16)
        st_ref[h] = s_old * jnp.exp(group_total[:, 0:1]) + lax.dot_general(
            kd, v, TN_DIMS, preferred_element_type=F32)
        y_ref[:, vs] = _rms_gate(o, ng_ref[...], g_ref[:, vs].astype(F32))

    @pl.when(j == pl.num_programs(1) - 1)
    def _():
        sout_ref[...] = st_ref[...]


def _gla_sample_kernel(q_ref, k_ref, v_ref, g_ref, gl_ref, wgk_ref, bgr_ref, ng_ref, stack_ref, lid_ref, grp_ref,
                       sin_ref, y_ref, sout_ref, *, n_heads, scale, seq, n_levels):
    rows = q_ref.shape[0]
    nb = rows // seq
    dk = q_ref.shape[1] // n_heads
    dv = v_ref.shape[1] // n_heads
    gl = gl_ref[...]
    for h in range(n_heads):
        ks, vs = slice(h * dk, (h + 1) * dk), slice(h * dv, (h + 1) * dv)
        q = q_ref[:, ks].astype(F32) * scale
        k = k_ref[:, ks].astype(F32)
        v = v_ref[:, vs]
        la = _gla_log_decay(gl, wgk_ref[:, ks], bgr_ref[:, ks])
        sc, cum, total, group_total = _gla_block(q, k, la, stack_ref[...], lid_ref[...], grp_ref[...], n_levels)
        q_dec = q * jnp.exp(cum)
        kd = k * jnp.exp(total - cum)
        v32 = v.astype(F32)
        o_state = []
        for s in range(nb):
            rs = slice(s * seq, (s + 1) * seq)
            s_old = sin_ref[s, h]
            o_state.append(_dot(q_dec[rs, :].astype(BF16), s_old.astype(BF16)))
            sout_ref[s, h] = s_old * jnp.exp(group_total[:, s:s + 1]) + lax.dot_general(
                kd[rs, :].astype(BF16), v32[rs, :].astype(BF16), TN_DIMS, preferred_element_type=F32)
        o = _dot(sc, v) + jnp.concatenate(o_state, axis=0)
        y_ref[:, vs] = _rms_gate(o, ng_ref[...], g_ref[:, vs].astype(F32))


def _gla_kernel(*refs, n_prompt_in, n_sample_in, prompt_args, sample_args):
    p_in, s_in = refs[:n_prompt_in], refs[n_prompt_in:n_prompt_in + n_sample_in]
    yp_ref, sp_ref, ys_ref, ss_ref, st_ref = refs[n_prompt_in + n_sample_in:]
    _gla_prompt_kernel(*p_in, yp_ref, sp_ref, st_ref, **prompt_args)
    _gla_sample_kernel(*s_in, ys_ref, ss_ref, **sample_args)


def _gla(p, gl, wgk, bg_row, ng, state_s, *, t_prompt, seq, n_heads, dk, dv, bk=256):
    t = p.shape[0]
    bs, _, _, _ = state_s.shape
    s_len = (t - t_prompt) // bs
    bp = t_prompt // seq
    dkt, dvt = n_heads * dk, n_heads * dv
    assert dvt == 2 * dkt
    scale = dk ** -0.5
    r = gl.shape[1]
    const = lambda *_: (0, 0)

    def tables(rows, span):
        stack, lid, grp, n_levels = _gla_tables(rows, span)
        specs = [pl.BlockSpec((r, dkt), const), pl.BlockSpec((1, dkt), const), pl.BlockSpec((1, dv), const),
                 pl.BlockSpec(stack.shape, const), pl.BlockSpec(lid.shape, const), pl.BlockSpec(grp.shape, const)]
        return specs, [wgk, bg_row, ng, stack, lid, grp], n_levels

    nblk = seq // bk
    n_steps = bp * nblk
    assert bs % n_steps == 0
    nb = bs // n_steps
    rows = nb * s_len
    off = t_prompt // rows
    rowp = lambda b, j: b * nblk + j
    rows_s = lambda b, j: off + rowp(b, j)
    p_specs, p_ins, levels_p = tables(bk, bk)
    s_specs, s_ins, levels_s = tables(rows, s_len)
    prompt_in = [pl.BlockSpec((bk, dkt), lambda b, j: (rowp(b, j), 0)),
                 pl.BlockSpec((bk, dkt), lambda b, j: (rowp(b, j), 1)),
                 pl.BlockSpec((bk, dvt), lambda b, j: (rowp(b, j), 1)),
                 pl.BlockSpec((bk, dvt), lambda b, j: (rowp(b, j), 2)),
                 pl.BlockSpec((bk, r), lambda b, j: (rowp(b, j), 0))] + p_specs
    sample_in = [pl.BlockSpec((rows, dkt), lambda b, j: (rows_s(b, j), 0)),
                 pl.BlockSpec((rows, dkt), lambda b, j: (rows_s(b, j), 1)),
                 pl.BlockSpec((rows, dvt), lambda b, j: (rows_s(b, j), 1)),
                 pl.BlockSpec((rows, dvt), lambda b, j: (rows_s(b, j), 2)),
                 pl.BlockSpec((rows, r), lambda b, j: (rows_s(b, j), 0))] + s_specs + [
                 pl.BlockSpec((nb, n_heads, dk, dv), lambda b, j: (rowp(b, j), 0, 0, 0))]
    y_p, s_p, y_s, s_s = pl.pallas_call(
        functools.partial(
            _gla_kernel, n_prompt_in=len(prompt_in), n_sample_in=len(sample_in),
            prompt_args=dict(n_heads=n_heads, scale=scale, n_levels=levels_p),
            sample_args=dict(n_heads=n_heads, scale=scale, seq=s_len, n_levels=levels_s)),
        grid=(bp, nblk),
        in_specs=prompt_in + sample_in,
        out_specs=[pl.BlockSpec((bk, dvt), lambda b, j: (rowp(b, j), 0)),
                   pl.BlockSpec((None, n_heads, dk, dv), lambda b, j: (b, 0, 0, 0)),
                   pl.BlockSpec((rows, dvt), lambda b, j: (rowp(b, j), 0)),
                   pl.BlockSpec((nb, n_heads, dk, dv), lambda b, j: (rowp(b, j), 0, 0, 0))],
        out_shape=[jax.ShapeDtypeStruct((t_prompt, dvt), BF16),
                   jax.ShapeDtypeStruct((bp, n_heads, dk, dv), F32),
                   jax.ShapeDtypeStruct((t - t_prompt, dvt), BF16),
                   jax.ShapeDtypeStruct(state_s.shape, F32)],
        scratch_shapes=[pltpu.VMEM((n_heads, dk, dv), F32)],
        compiler_params=_params("arbitrary", "arbitrary"),
        name="gla",
    )(p, p, p, p, gl, *p_ins, p, p, p, p, gl, *s_ins, state_s)
    return (y_p, y_s), s_p, s_s


def _moe_schedule(eid, rank, counts, *, tg, n_tiles):
    n_e = counts.shape[0]
    e_ids = jnp.arange(n_e, dtype=I32)
    padded = ((counts + tg - 1) // tg) * tg
    ends = jnp.sum(jnp.where(e_ids[None, :] <= e_ids[:, None], padded[None, :], 0), axis=1)
    starts = ends - padded
    pos = jnp.sum(jnp.where(eid[None] == e_ids[:, None, None], starts[:, None, None], 0), axis=0) + rank
    n_used = ends[n_e - 1] // tg
    tile_start = jnp.arange(n_tiles, dtype=I32) * tg
    te = jnp.sum((ends[None, :] <= tile_start[:, None]).astype(I32), axis=1)
    te_last = jnp.sum((ends <= (n_used - 1) * tg).astype(I32))
    te = jnp.where(jnp.arange(n_tiles) < n_used, te, te_last)
    later = (e_ids[None, :] > e_ids[:, None]) & (counts[None, :] > 0)
    nxt = jnp.min(jnp.where(later, e_ids[None, :], n_e), axis=1)
    nxt = jnp.where(nxt == n_e, -1, nxt)
    i32 = lambda a: a.astype(I32)
    return (i32(pos.reshape(-1)), i32(starts + counts), i32(padded - counts), i32(te), i32(nxt),
            i32(n_used.reshape(1)))


def kernel(x_prompt, x_sample, cache_conv, state_gla, c_prompt, c_sample, w_mod, b_mod, ln_g, ln_b, ab_w_in, ab_conv_w, ab_v_ln_g, ab_v_ln_b, ab_w_s, ab_b_s, ab_w_out, gla_w_in, gla_w_gk, gla_b_gk, gla_norm_g, gla_w_out, moe_w_grp, moe_b_grp, moe_w_rt, moe_b_rt, moe_w1, moe_w3, moe_w2):
    bp, seq, d = x_prompt.shape
    bs, s_len, _ = x_sample.shape
    assert s_len == SUBLANES_V7X and seq % SUBLANES_V7X == 0
    depth = w_mod.shape[0]
    alpha = float((2 * depth) ** 0.25)
    t_p, t_s = bp * seq, bs * s_len
    t = t_p + t_s
    n_groups, n_exp = moe_w_rt.shape[1], moe_w_rt.shape[3]
    n_e = n_groups * n_exp
    d_ff = moe_w1.shape[-1]
    tg = 256
    n_tiles = (TOP_K_INNER * t) // tg + n_e

    x = (x_prompt.reshape(t_p // SUBLANES_V7X, SUBLANES_V7X, d), x_sample)
    mods = _mod_vectors(c_prompt, c_sample, w_mod, b_mod)

    w1 = moe_w1.reshape(depth * n_e, d, d_ff)
    w3 = moe_w3.reshape(depth * n_e, d, d_ff)
    w2 = moe_w2.reshape(depth * n_e, d_ff, d)

    conv_p, conv_s, chunk_v, gla_p, gla_s = [], [], [], [], []
    h_bf = None
    for layer in range(depth):
        li = layer // 2
        if layer % 2 == 0:
            n_heads, chunk = ab_w_s.shape[1], ab_w_s.shape[2]
            dc = ab_conv_w.shape[-1]
            if h_bf is None:
                p = _mm(_modulate(x, mods, layer, 0, 1, seq=seq), ab_w_in, li, ab_w_in.shape[-1])
            else:
                p = _mm(h_bf, ab_w_in, li, ab_w_in.shape[-1])
            w_s = ab_w_s[li]
            wm_p = jnp.tril(w_s)
            reps = chunk // s_len
            blk = jnp.tril(w_s[:, :s_len, :s_len])
            wm_s = jnp.einsum("ab,hts->hatbs", jnp.eye(reps, dtype=F32), blk).reshape(n_heads, chunk, chunk)
            wm = jnp.stack([wm_p, wm_s]).astype(BF16)
            b_s = ab_b_s[li]
            hd = dc // n_heads
            bias_p = jnp.repeat(b_s.T, hd, axis=1)
            bias_s = jnp.repeat(jnp.tile(b_s[:, :s_len].T, (reps, 1)), hd, axis=1)
            bias = jnp.stack([bias_p, bias_s])
            y, cp_new, cs_new, vn_s = _mix0(p, cache_conv[li], ab_conv_w[li], ab_v_ln_g[li], ab_v_ln_b[li],
                                            wm, bias, t_prompt=t_p, seq=seq, n_heads=n_heads)
            conv_p.append(cp_new)
            conv_s.append(cs_new)
            chunk_v.append(vn_s)
            w_out = ab_w_out[li].astype(BF16)
        else:
            n_heads, dk, dv = state_gla.shape[2], state_gla.shape[3], state_gla.shape[4]
            dkt, dvt = n_heads * dk, n_heads * dv
            rank = gla_w_gk.shape[1]
            n_main = 2 * dkt + 2 * dvt
            w_in_t = jnp.swapaxes(gla_w_in, 1, 2)
            p = _mm(h_bf, w_in_t, li, n_main, w_transposed=True)
            w_lo = jnp.pad(w_in_t[li, n_main:, :], ((0, LANES_V7X - rank), (0, 0)))[None]
            gl = _mm(h_bf, w_lo, 0, LANES_V7X, w_transposed=True)
            wgk = jnp.pad(gla_w_gk[li], ((0, LANES_V7X - rank), (0, 0))).astype(BF16)
            y, sp_new, ss_new = _gla(p, gl, wgk, gla_b_gk[li].reshape(1, dkt),
                                     gla_norm_g[li].reshape(1, dv), state_gla[li],
                                     t_prompt=t_p, seq=seq, n_heads=n_heads, dk=dk, dv=dv)
            gla_p.append(sp_new)
            gla_s.append(ss_new)
            w_out = gla_w_out[li].astype(BF16)

        wr = jnp.concatenate([moe_w_grp[layer].T,
                              jnp.transpose(moe_w_rt[layer], (0, 2, 1)).reshape(n_e, d)], axis=0)
        wr = jnp.pad(wr, ((0, LANES_V7X - wr.shape[0]), (0, 0)))
        br = jnp.concatenate([moe_b_grp[layer], moe_b_rt[layer].reshape(n_e)])
        br = jnp.pad(br, (0, LANES_V7X - br.shape[0])).reshape(LANES_V7X, 1)
        x1, h2, eid, wt, rank_, cnt = _outln(y, w_out, x, mods, layer, ln_g[layer, 0], ln_b[layer, 0], wr, br,
                                             alpha=alpha, n_groups=n_groups, n_exp=n_exp, t_prompt=t_p, seq=seq)
        pos, zero_start, zero_count, te, nxt, n_used = _moe_schedule(eid, rank_, cnt[:, 0], tg=tg, n_tiles=n_tiles)
        xs = _dispatch(pos, zero_start, zero_count, n_used, h2, n_tiles=n_tiles, tg=tg)
        ys = _moe(xs, w1, w3, w2, layer * n_e, te, nxt, n_used, tg=tg)
        has_next = layer + 1 < depth
        outs = _comb(pos, ys, wt.T, x1, mods, layer, ln_g[layer, 1], ln_b[layer, 1],
                     alpha=alpha, has_next=has_next, t_prompt=t_p, seq=seq)
        if has_next:
            x, h_bf = outs

    y_prompt = outs[0].reshape(bp, seq, d)
    y_sample = outs[1].reshape(bs, s_len, d)
    return (y_prompt, y_sample, jnp.stack(conv_p), jnp.stack(conv_s), jnp.stack(chunk_v),
            jnp.stack(gla_p), jnp.stack(gla_s))
```

```python
import functools

import jax
import numpy as np
import jax.numpy as jnp
from jax import lax
from jax.experimental import pallas as pl
from jax.experimental.pallas import tpu as pltpu

F32 = jnp.float32
BF16 = jnp.bfloat16
I32 = jnp.int32

LN_EPS = 1e-5
RMS_EPS = 1e-6
GLA_GATE_NORMALIZER = 16.0
TOP_K_INNER = 2

SUBLANES_V7X = 8
LANES_V7X = 128
VMEM_LIMIT_V7X = 56 * 1024 * 1024

NT_DIMS = (((1,), (1,)), ((), ()))
TN_DIMS = (((0,), (0,)), ((), ()))


def _params(*sem):
    return pltpu.CompilerParams(dimension_semantics=sem, vmem_limit_bytes=VMEM_LIMIT_V7X)


def _silu(x):
    return x * (1.0 / (1.0 + jnp.exp(-x)))


def _log_sigmoid(z):
    return jnp.minimum(z, 0.0) - jnp.log(1.0 + jnp.exp(-jnp.abs(z)))


def _layer_norm(x, g, b):
    mu = jnp.mean(x, axis=-1, keepdims=True)
    xc = x - mu
    var = jnp.mean(xc * xc, axis=-1, keepdims=True)
    return xc * lax.rsqrt(var + LN_EPS) * g + b


def _dot(a, b):
    return jnp.dot(a, b, preferred_element_type=F32)


def _split_bf16(x):
    hi = x.astype(BF16)
    lo = (x - hi.astype(F32)).astype(BF16)
    return hi, lo


def _mod_kernel(c_ref, w_ref, b_ref, op_ref, os_ref):
    r = _dot(_silu(c_ref[...]).astype(BF16), w_ref[...].astype(BF16)) + b_ref[...]
    n_p = op_ref.shape[0]
    op_ref[...] = r[:n_p, :]
    os_ref[...] = r[n_p:, :]


def _mod_vectors(c_prompt, c_sample, w_mod, b_mod):
    depth, d, n = w_mod.shape
    bp, bs = c_prompt.shape[0], c_sample.shape[0]
    bp_rows = bp + (-bp) % SUBLANES_V7X
    c_all = jnp.concatenate([jnp.pad(c_prompt, ((0, bp_rows - bp), (0, 0))), c_sample], axis=0)
    tn = 1024
    return pl.pallas_call(
        _mod_kernel,
        grid=(depth, n // tn),
        in_specs=[
            pl.BlockSpec(c_all.shape, lambda l, j: (0, 0)),
            pl.BlockSpec((None, d, tn), lambda l, j: (l, 0, j)),
            pl.BlockSpec((None, 1, tn), lambda l, j: (l, 0, j)),
        ],
        out_specs=[pl.BlockSpec((None, bp_rows, tn), lambda l, j: (l, 0, j)),
                   pl.BlockSpec((None, bs, tn), lambda l, j: (l, 0, j))],
        out_shape=[jax.ShapeDtypeStruct((depth, bp_rows, n), F32), jax.ShapeDtypeStruct((depth, bs, n), F32)],
        compiler_params=_params("arbitrary", "arbitrary"),
        name="mod",
    )(c_all, w_mod, b_mod.reshape(depth, 1, n))


def _mod_specs(mods, layer, col, d, gt, n_p):
    mp, _ = mods
    return [pl.BlockSpec((None, mp.shape[1], d), lambda i, *_: (layer, 0, col)),
            pl.BlockSpec((None, gt, d), lambda i, *_: (layer, jnp.maximum(i - n_p, 0), col))]


def _mod_value(p_ref, s_ref, prompt, seq_idx):
    if prompt:
        return p_ref[pl.ds(seq_idx, 1), :][None]
    return s_ref[...][:, None, :]


def _by_group(i, n_prompt_tiles, fn):
    pl.when(i < n_prompt_tiles)(lambda: fn(True))
    pl.when(i >= n_prompt_tiles)(lambda: fn(False))


def _mm_kernel(a_ref, w_ref, o_ref, wb_ref, *, w_transposed):
    @pl.when(pl.program_id(1) == 0)
    def _():
        w = w_ref[...]
        wb_ref[...] = (w.T if w_transposed else w).astype(BF16)

    o_ref[...] = _dot(a_ref[...], wb_ref[...]).astype(o_ref.dtype)


def _mm(a, w3, w_idx, n_out, *, tm=1024, tn=1024, out_dtype=BF16, w_transposed=False):
    t = a.shape[0]
    k = w3.shape[2 if w_transposed else 1]
    tn = min(tn, n_out)
    return pl.pallas_call(
        functools.partial(_mm_kernel, w_transposed=w_transposed),
        grid=(n_out // tn, t // tm),
        in_specs=[pl.BlockSpec((tm, k), lambda j, i: (i, 0)),
                  pl.BlockSpec((None, tn, k), lambda j, i: (w_idx, j, 0)) if w_transposed
                  else pl.BlockSpec((None, k, tn), lambda j, i: (w_idx, 0, j))],
        out_specs=pl.BlockSpec((tm, tn), lambda j, i: (i, j)),
        out_shape=jax.ShapeDtypeStruct((t, n_out), out_dtype),
        scratch_shapes=[pltpu.VMEM((k, tn), BF16)],
        compiler_params=_params("arbitrary", "arbitrary"),
        name="mm",
    )(a, w3)


def _modulate_kernel(xp_ref, xs_ref, shp_ref, shs_ref, scp_ref, scs_ref, o_ref, *, n_prompt_tiles, tiles_per_seq):
    i = pl.program_id(0)

    def run(prompt):
        x = (xp_ref if prompt else xs_ref)[...]
        g, s, k = x.shape
        seq_idx = i // tiles_per_seq
        h = x * (1.0 + _mod_value(scp_ref, scs_ref, prompt, seq_idx)) + _mod_value(shp_ref, shs_ref, prompt, seq_idx)
        o_ref[...] = h.reshape(g * s, k).astype(BF16)

    _by_group(i, n_prompt_tiles, run)


def _modulate(x_pair, mods, layer, sh_col, sc_col, *, seq, tm=512):
    xp, xs = x_pair
    k = xp.shape[-1]
    gt = tm // SUBLANES_V7X
    n_p = xp.shape[0] // gt
    t = (xp.shape[0] + xs.shape[0]) * SUBLANES_V7X
    return pl.pallas_call(
        functools.partial(_modulate_kernel, n_prompt_tiles=n_p, tiles_per_seq=seq // tm),
        grid=(t // tm,),
        in_specs=[pl.BlockSpec((gt, SUBLANES_V7X, k), lambda i: (jnp.minimum(i, n_p - 1), 0, 0)),
                  pl.BlockSpec((gt, SUBLANES_V7X, k), lambda i: (jnp.maximum(i - n_p, 0), 0, 0))]
                 + _mod_specs(mods, layer, sh_col, k, gt, n_p) + _mod_specs(mods, layer, sc_col, k, gt, n_p),
        out_specs=pl.BlockSpec((tm, k), lambda i: (i, 0)),
        out_shape=jax.ShapeDtypeStruct((t, k), BF16),
        compiler_params=_params("arbitrary"),
        name="modulate",
    )(xp, xs, *mods, *mods)


def _mix0_kernel(bg_ref, cg_ref, hx_ref, u_ref, v_ref, cache_ref, cw_ref, vg_ref, vb_ref, wm_ref, bias_ref,
                 y_ref, convp_ref, convs_ref, vns_ref, zprev_ref, *, n_prompt_tiles, tiles_per_seq, n_heads):
    i = pl.program_id(0)
    tm, dc = bg_ref.shape
    ns = tm // SUBLANES_V7X
    z = cg_ref[...].astype(F32) * hx_ref[...].astype(F32)
    row = lax.broadcasted_iota(I32, (tm, dc), 0)
    r1 = pltpu.roll(z, 1, 0)
    r2 = pltpu.roll(z, 2, 0)
    cw = cw_ref[...]
    bg = bg_ref[...].astype(F32)

    vn = _layer_norm(v_ref[...].astype(F32), vg_ref[...], vb_ref[...])
    vnb = vn.astype(BF16)
    hd = dc // n_heads
    mixed = jnp.concatenate(
        [_dot(wm_ref[h], vnb[:, h * hd:(h + 1) * hd]) for h in range(n_heads)], axis=-1) + bias_ref[...]
    y_ref[:, dc:] = (u_ref[...].astype(F32) * mixed).astype(BF16)

    def conv_out(zm1, zm2):
        conv = cw[0:1, :] * zm2 + cw[1:2, :] * zm1 + cw[2:3, :] * z
        y_ref[:, :dc] = (bg * conv).astype(BF16)

    @pl.when(i < n_prompt_tiles)
    def _prompt():
        @pl.when(i % tiles_per_seq == 0)
        def _():
            zprev_ref[...] = jnp.zeros_like(zprev_ref)

        zp = zprev_ref[...]
        p1 = zp[SUBLANES_V7X - 1:SUBLANES_V7X, :]
        p2 = zp[SUBLANES_V7X - 2:SUBLANES_V7X - 1, :]
        conv_out(jnp.where(row == 0, p1, r1),
                 jnp.where(row == 0, p2, jnp.where(row == 1, p1, r2)))
        zprev_ref[...] = z[tm - SUBLANES_V7X:, :]
        convp_ref[...] = z[tm - 2:, :].reshape(1, 2, dc)

    @pl.when(i >= n_prompt_tiles)
    def _sample():
        c = cache_ref[...]
        c0 = jnp.broadcast_to(c[:, 0:1, :], (ns, SUBLANES_V7X, dc)).reshape(tm, dc)
        c1 = jnp.broadcast_to(c[:, 1:2, :], (ns, SUBLANES_V7X, dc)).reshape(tm, dc)
        rr = row % SUBLANES_V7X
        conv_out(jnp.where(rr == 0, c1, r1),
                 jnp.where(rr == 0, c0, jnp.where(rr == 1, c1, r2)))
        z3 = z.reshape(ns, SUBLANES_V7X, dc)
        convs_ref[...] = z3[:, SUBLANES_V7X - 2:, :]
        vns_ref[...] = vn.reshape(ns, SUBLANES_V7X, dc)


def _mix0(p, cache, conv_w, v_g, v_b, wm, bias, *, t_prompt, seq, n_heads):
    t, n = p.shape
    bs, cwm1, dc = cache.shape
    tm = wm.shape[-1]
    assert cwm1 == 2 and conv_w.shape[0] == 3 and n == 5 * dc and seq % tm == 0
    n_p = t_prompt // tm
    n_s = (t - t_prompt) // tm
    tps = seq // tm
    bp = t_prompt // seq
    ns = tm // SUBLANES_V7X

    def col(c):
        return pl.BlockSpec((tm, dc), lambda i: (i, c))

    def s_idx(i):
        return jnp.maximum(i - n_p, 0)

    const2 = lambda i: (0, 0)
    mode = lambda i: ((i >= n_p).astype(I32), 0, 0, 0)
    return pl.pallas_call(
        functools.partial(_mix0_kernel, n_prompt_tiles=n_p, tiles_per_seq=tps, n_heads=n_heads),
        grid=(n_p + n_s,),
        in_specs=[col(0), col(1), col(2), col(3), col(4),
                  pl.BlockSpec((ns, 2, dc), lambda i: (s_idx(i), 0, 0)),
                  pl.BlockSpec((3, dc), const2),
                  pl.BlockSpec((1, dc), const2),
                  pl.BlockSpec((1, dc), const2),
                  pl.BlockSpec((None, n_heads, tm, tm), mode),
                  pl.BlockSpec((None, tm, dc), lambda i: ((i >= n_p).astype(I32), 0, 0))],
        out_specs=[pl.BlockSpec((tm, 2 * dc), lambda i: (i, 0)),
                   pl.BlockSpec((1, 2, dc), lambda i: (jnp.minimum(i // tps, bp - 1), 0, 0)),
                   pl.BlockSpec((ns, 2, dc), lambda i: (s_idx(i), 0, 0)),
                   pl.BlockSpec((ns, SUBLANES_V7X, dc), lambda i: (s_idx(i), 0, 0))],
        out_shape=[jax.ShapeDtypeStruct((t, 2 * dc), BF16),
                   jax.ShapeDtypeStruct((bp, 2, dc), F32),
                   jax.ShapeDtypeStruct((bs, 2, dc), F32),
                   jax.ShapeDtypeStruct((bs, SUBLANES_V7X, dc), F32)],
        scratch_shapes=[pltpu.VMEM((SUBLANES_V7X, dc), F32)],
        compiler_params=_params("arbitrary"),
        name="mix0",
    )(p, p, p, p, p, cache, conv_w, v_g.reshape(1, dc), v_b.reshape(1, dc), wm, bias)


def _first_index_of(vals, target):
    idx = jnp.full(target.shape, len(vals) - 1, I32)
    for j in reversed(range(len(vals))):
        idx = jnp.where(vals[j] == target, j, idx)
    return idx


def _softmax_rows(rows):
    m = functools.reduce(jnp.maximum, rows)
    e = [jnp.exp(r - m) for r in rows]
    s = functools.reduce(lambda a, b: a + b, e)
    return [x / s for x in e]


def _route(h, wr_ref, br_ref, eid_ref, wt_ref, rank_ref, cnt_ref, carry_ref, n_groups, n_exp):
    tm = h.shape[0]
    hh, hl = _split_bf16(h)
    wh, wl = _split_bf16(wr_ref[...])
    dg = lambda a, b: lax.dot_general(a, b, NT_DIMS, preferred_element_type=F32)
    logits = dg(wh, hh) + dg(wh, hl) + dg(wl, hh) + br_ref[...]

    g_prob = _softmax_rows([logits[g:g + 1, :] for g in range(n_groups)])
    g_top = functools.reduce(jnp.maximum, g_prob)
    g_idx = _first_index_of(g_prob, g_top)

    e_sel = []
    for e in range(n_exp):
        sel = logits[n_groups + e:n_groups + e + 1, :]
        for g in range(1, n_groups):
            r = n_groups + g * n_exp + e
            sel = jnp.where(g_idx == g, logits[r:r + 1, :], sel)
        e_sel.append(sel)
    e_prob = _softmax_rows(e_sel)
    p1 = functools.reduce(jnp.maximum, e_prob)
    i1 = _first_index_of(e_prob, p1)
    rest = [jnp.where(i1 == e, -1.0, e_prob[e]) for e in range(n_exp)]
    p2 = functools.reduce(jnp.maximum, rest)
    i2 = _first_index_of(rest, p2)
    den = p1 + p2
    wt_ref[0:1, :] = g_top * (p1 / den)
    wt_ref[1:2, :] = g_top * (p2 / den)
    eid0 = g_idx * n_exp + i1
    eid1 = g_idx * n_exp + i2
    eid_ref[0:1, :] = eid0
    eid_ref[1:2, :] = eid1

    n_e = n_groups * n_exp
    eio = lax.broadcasted_iota(I32, (n_e, tm), 0)
    oh0 = (eio == eid0).astype(F32)
    oh1 = (eio == eid1).astype(F32)
    oh = oh0 + oh1
    before = (lax.broadcasted_iota(I32, (tm, tm), 0) < lax.broadcasted_iota(I32, (tm, tm), 1)).astype(BF16)
    base = _dot(oh.astype(BF16), before) + carry_ref[...]
    rank_ref[0:1, :] = jnp.sum(oh0 * base, axis=0, keepdims=True).astype(I32)
    rank_ref[1:2, :] = jnp.sum(oh1 * base, axis=0, keepdims=True).astype(I32)
    total = carry_ref[...] + jnp.sum(oh, axis=1, keepdims=True)
    carry_ref[...] = total
    cnt_ref[...] = jnp.broadcast_to(total, cnt_ref.shape).astype(I32)


def _outln_kernel(*refs, alpha, n_groups, n_exp, y_split, x_split, n_prompt_tiles, tiles_per_seq):
    refs = list(refs)
    i = pl.program_id(0)
    take = lambda n: [refs.pop(0) for _ in range(n)]
    y_refs, (w_ref,), x_refs = take(2 if y_split else 1), take(1), take(2 if x_split else 1)
    gt_refs, sh_refs, sc_refs = take(2), take(2), take(2)
    (lng_ref, lnb_ref, wr_ref, br_ref, x1_ref, h_ref, eid_ref, wt_ref, rank_ref, cnt_ref, carry_ref) = refs

    @pl.when(i == 0)
    def _():
        carry_ref[...] = jnp.zeros_like(carry_ref)

    def run(prompt):
        pick = lambda pair: pair[0 if prompt or len(pair) == 1 else 1]
        mod = lambda pair: _mod_value(*pair, prompt, i // tiles_per_seq)
        x = pick(x_refs)[...]
        g, s, d = x.shape
        m = _dot(pick(y_refs)[...], w_ref[...]).reshape(g, s, d)
        x1 = _layer_norm(alpha * x + mod(gt_refs) * m, lng_ref[...], lnb_ref[...])
        x1_ref[...] = x1
        h = (x1 * (1.0 + mod(sc_refs)) + mod(sh_refs)).reshape(g * s, d)
        h_ref[...] = h
        _route(h, wr_ref, br_ref, eid_ref, wt_ref, rank_ref, cnt_ref, carry_ref, n_groups, n_exp)

    _by_group(i, n_prompt_tiles, run)


def _outln(y, w_bf, x, mods, layer, ln_g, ln_b, wr, br, *, alpha, n_groups, n_exp, t_prompt, seq, tm=512):
    y_split, x_split = isinstance(y, tuple), isinstance(x, tuple)
    ys, xs = (y if y_split else (y,)), (x if x_split else (x,))
    k, d = w_bf.shape
    gt = tm // SUBLANES_V7X
    n_e = n_groups * n_exp
    t = sum(a.shape[0] for a in ys)
    n_p = t_prompt // tm

    def split_specs(block, n_arrays):
        if n_arrays == 1:
            return [pl.BlockSpec(block, lambda i: (i,) + (0,) * (len(block) - 1))]
        return [pl.BlockSpec(block, lambda i: (jnp.minimum(i, n_p - 1),) + (0,) * (len(block) - 1)),
                pl.BlockSpec(block, lambda i: (jnp.maximum(i - n_p, 0),) + (0,) * (len(block) - 1))]

    mod = lambda c: _mod_specs(mods, layer, c, d, gt, n_p)
    const = lambda i: (0, 0)
    pair = lambda dt: jax.ShapeDtypeStruct((TOP_K_INNER, t), dt)
    pair_spec = pl.BlockSpec((TOP_K_INNER, tm), lambda i: (0, i))
    return pl.pallas_call(
        functools.partial(_outln_kernel, alpha=alpha, n_groups=n_groups, n_exp=n_exp,
                          y_split=y_split, x_split=x_split, n_prompt_tiles=n_p, tiles_per_seq=seq // tm),
        grid=(t // tm,),
        in_specs=split_specs((tm, k), len(ys))
                 + [pl.BlockSpec((k, d), const, pipeline_mode=pl.Buffered(1))]
                 + split_specs((gt, SUBLANES_V7X, d), len(xs))
                 + mod(2) + mod(3) + mod(4)
                 + [pl.BlockSpec((1, d), const),
                    pl.BlockSpec((1, d), const),
                    pl.BlockSpec(wr.shape, const),
                    pl.BlockSpec(br.shape, const)],
        out_specs=[pl.BlockSpec((gt, SUBLANES_V7X, d), lambda i: (i, 0, 0)),
                   pl.BlockSpec((tm, d), lambda i: (i, 0)),
                   pair_spec, pair_spec, pair_spec,
                   pl.BlockSpec((n_e, LANES_V7X), const)],
        out_shape=[jax.ShapeDtypeStruct((t // SUBLANES_V7X, SUBLANES_V7X, d), F32),
                   jax.ShapeDtypeStruct((t, d), F32),
                   pair(I32), pair(F32), pair(I32),
                   jax.ShapeDtypeStruct((n_e, LANES_V7X), I32)],
        scratch_shapes=[pltpu.VMEM((n_e, 1), F32)],
        compiler_params=_params("arbitrary"),
        name="outln",
    )(*ys, w_bf, *xs, *mods, *mods, *mods, ln_g.reshape(1, d), ln_b.reshape(1, d), wr, br)


ROW_DMA_UNROLL = 8


def _row_gather_start(src_hbm, dst, sem, idx_ref, base, n):
    def body(q, c):
        for u in range(ROW_DMA_UNROLL):
            r = q * ROW_DMA_UNROLL + u
            row = idx_ref[base + r]
            pltpu.make_async_copy(src_hbm.at[pl.ds(row, 1)], dst.at[pl.ds(r, 1)], sem).start(priority=u % 2)
        return c
    lax.fori_loop(0, n // ROW_DMA_UNROLL, body, 0)


def _row_gather_wait(src_hbm, dst, sem, n):
    pltpu.make_async_copy(src_hbm.at[pl.ds(0, n)], dst, sem).wait()


def _dispatch_kernel(pos_ref, zs_ref, zn_ref, nu_ref, h_ref, xs_hbm, zrow_ref, ztile_ref, sem, zsem,
                     *, t_total, n_e, first_free_tile):
    i = pl.program_id(0)
    tm = h_ref.shape[0]
    tg = ztile_ref.shape[0]
    n_tiles = xs_hbm.shape[0] // tg
    n_used = nu_ref[0]

    @pl.when(i == 0)
    def _():
        zrow_ref[...] = jnp.zeros_like(zrow_ref)
        ztile_ref[...] = jnp.zeros_like(ztile_ref)
        pad_row = lambda p: pltpu.make_async_copy(zrow_ref.at[pl.ds(0, 1)], xs_hbm.at[pl.ds(p, 1)], zsem)
        tail_tile = lambda j: pltpu.make_async_copy(ztile_ref, xs_hbm.at[pl.ds(j * tg, tg)], zsem)

        def pads(do):
            for e in range(n_e):
                base = zs_ref[e]

                def body(r, c):
                    do(pad_row(base + r))
                    return c
                lax.fori_loop(0, zn_ref[e], body, 0)
            for j in range(first_free_tile, n_tiles):
                @pl.when(j >= n_used)
                def _():
                    do(tail_tile(j))

        pads(lambda cp: cp.start())
        pads(lambda cp: cp.wait())

    for k in range(TOP_K_INNER):
        def body(q, c):
            for u in range(ROW_DMA_UNROLL):
                r = q * ROW_DMA_UNROLL + u
                p = pos_ref[k * t_total + i * tm + r]
                pltpu.make_async_copy(h_ref.at[pl.ds(r, 1)], xs_hbm.at[pl.ds(p, 1)], sem).start(priority=u % 2)
            return c
        lax.fori_loop(0, tm // ROW_DMA_UNROLL, body, 0)
    for k in range(TOP_K_INNER):
        pltpu.make_async_copy(h_ref, xs_hbm.at[pl.ds(0, tm)], sem).wait()


def _dispatch(pos, zero_start, zero_count, n_used, h, *, n_tiles, tg, tm=1024):
    t, d = h.shape
    n_e = zero_start.shape[0]
    grid_spec = pltpu.PrefetchScalarGridSpec(
        num_scalar_prefetch=4,
        grid=(t // tm,),
        in_specs=[pl.BlockSpec((tm, d), lambda i, *_: (i, 0))],
        out_specs=pl.BlockSpec(memory_space=pl.ANY),
        scratch_shapes=[pltpu.VMEM((SUBLANES_V7X, d), h.dtype), pltpu.VMEM((tg, d), h.dtype),
                        pltpu.SemaphoreType.DMA(()), pltpu.SemaphoreType.DMA(())],
    )
    return pl.pallas_call(
        functools.partial(_dispatch_kernel, t_total=t, n_e=n_e, first_free_tile=(TOP_K_INNER * t) // tg),
        grid_spec=grid_spec,
        out_shape=jax.ShapeDtypeStruct((n_tiles * tg, d), h.dtype),
        compiler_params=_params("arbitrary"),
        name="dispatch",
    )(pos, zero_start, zero_count, n_used, h)


def _moe_kernel(te_ref, nxt_ref, nu_ref, xs_ref, w1_hbm, w3_hbm, w2_hbm, o_ref,
                w1s, w3s, w2s, w1b, w3b, w2b, wslot_ref, wsem, *, w_base):
    i = pl.program_id(0)
    n_used = nu_ref[0]

    def weight_copies(e, slot):
        return [pltpu.make_async_copy(hbm.at[w_base + e], stage.at[slot], wsem.at[slot])
                for hbm, stage in ((w1_hbm, w1s), (w3_hbm, w3s), (w2_hbm, w2s))]

    @pl.when(i == 0)
    def _():
        wslot_ref[0] = 1
        for cp in weight_copies(te_ref[0], 0):
            cp.start()

    @pl.when(i < n_used)
    def _():
        e = te_ref[i]

        @pl.when((i == 0) | (e != te_ref[jnp.maximum(i - 1, 0)]))
        def _():
            ws = 1 - wslot_ref[0]
            wslot_ref[0] = ws
            for cp in weight_copies(e, ws):
                cp.wait()
            w1b[...] = w1s[ws].astype(BF16)
            w3b[...] = w3s[ws].astype(BF16)
            w2b[...] = w2s[ws].astype(BF16)
            ne = nxt_ref[e]

            @pl.when(ne >= 0)
            def _():
                for cp in weight_copies(ne, 1 - ws):
                    cp.start()

        x = xs_ref[...].astype(BF16)
        a = _dot(x, w1b[...])
        b = _dot(x, w3b[...])
        o_ref[...] = _dot((_silu(a) * b).astype(BF16), w2b[...])

    @pl.when(i >= n_used)
    def _():
        o_ref[...] = jnp.zeros_like(o_ref)


def _moe(xs, w1, w3, w2, w_base, te, nxt, n_used, *, tg):
    p_tot, d = xs.shape
    f = w1.shape[-1]
    any_spec = pl.BlockSpec(memory_space=pl.ANY)
    grid_spec = pltpu.PrefetchScalarGridSpec(
        num_scalar_prefetch=3,
        grid=(p_tot // tg,),
        in_specs=[pl.BlockSpec((tg, d), lambda i, te, nxt, nu: (jnp.minimum(i, nu[0] - 1), 0)),
                  any_spec, any_spec, any_spec],
        out_specs=pl.BlockSpec((tg, d), lambda i, *_: (i, 0)),
        scratch_shapes=[pltpu.VMEM((2, d, f), F32), pltpu.VMEM((2, d, f), F32), pltpu.VMEM((2, f, d), F32),
                        pltpu.VMEM((d, f), BF16), pltpu.VMEM((d, f), BF16), pltpu.VMEM((f, d), BF16),
                        pltpu.SMEM((1,), I32), pltpu.SemaphoreType.DMA((2,))],
    )
    return pl.pallas_call(
        functools.partial(_moe_kernel, w_base=w_base),
        grid_spec=grid_spec,
        out_shape=jax.ShapeDtypeStruct((p_tot, d), F32),
        compiler_params=_params("arbitrary"),
        name="moe",
    )(te, nxt, n_used, xs, w1, w3, w2)


def _comb_kernel(pos_ref, y_hbm, wt_ref, x_ref, gtp_ref, gts_ref, lng_ref, lnb_ref, *rest,
                 alpha, tm, t_total, has_next, n_prompt_tiles, tiles_per_seq):
    if has_next:
        shp_ref, shs_ref, scp_ref, scs_ref, x2_ref, hn_ref, ybuf, sem = rest
    else:
        x2p_ref, x2s_ref, ybuf, sem = rest
    i = pl.program_id(0)
    n = pl.num_programs(0)

    n_slots = ybuf.shape[0]

    def wait(slot):
        for k in range(TOP_K_INNER):
            _row_gather_wait(y_hbm, ybuf.at[slot, k], sem.at[slot], tm)

    @pl.when(i == 0)
    def _():
        for tile in range(n_slots - 1):
            for k in range(TOP_K_INNER):
                _row_gather_start(y_hbm, ybuf.at[tile, k], sem.at[tile], pos_ref, k * t_total + tile * tm, tm)

    slot = i % n_slots
    wait(slot)

    def start_ahead():
        tile = jnp.minimum(i + n_slots - 1, n - 1)
        nslot = (i + n_slots - 1) % n_slots
        for k in range(TOP_K_INNER):
            for r in range(tm):
                row = pos_ref[k * t_total + tile * tm + r]
                pltpu.make_async_copy(y_hbm.at[pl.ds(row, 1)], ybuf.at[nslot, k, pl.ds(r, 1)],
                                      sem.at[nslot]).start(priority=r % 2)

    def run(prompt):
        start_ahead()
        seq_idx = i // tiles_per_seq
        w = wt_ref[...]
        f = w[:, 0:1] * ybuf[slot, 0] + w[:, 1:2] * ybuf[slot, 1]
        g, s, d = x_ref.shape
        x2 = _layer_norm(alpha * x_ref[...] + _mod_value(gtp_ref, gts_ref, prompt, seq_idx) * f.reshape(g, s, d),
                         lng_ref[...], lnb_ref[...])
        if has_next:
            x2_ref[...] = x2
            hn = (x2 * (1.0 + _mod_value(scp_ref, scs_ref, prompt, seq_idx))
                  + _mod_value(shp_ref, shs_ref, prompt, seq_idx))
            hn_ref[...] = hn.reshape(g * s, d).astype(BF16)
        else:
            (x2p_ref if prompt else x2s_ref)[...] = x2

    _by_group(i, n_prompt_tiles, run)

    @pl.when(i == n - 1)
    def _():
        for ahead in range(1, n_slots):
            wait((i + ahead) % n_slots)


def _comb(pos_flat, y_sorted, wt_t, x1, mods, layer, ln_g, ln_b, *, alpha, has_next, t_prompt, seq, tm=256):
    g_total, s, d = x1.shape
    t = g_total * s
    gt = tm // SUBLANES_V7X
    n_p = t_prompt // tm
    mod = lambda l, c: _mod_specs(mods, l, c, d, gt, n_p)
    xspec = pl.BlockSpec((gt, s, d), lambda i, pos: (i, 0, 0))
    vec = pl.BlockSpec((1, d), lambda i, pos: (0, 0))
    in_specs = [pl.BlockSpec(memory_space=pl.ANY),
                pl.BlockSpec((tm, TOP_K_INNER), lambda i, pos: (i, 0)),
                xspec] + mod(layer, 5) + [vec, vec]
    ins = [y_sorted, wt_t, x1, *mods, ln_g.reshape(1, d), ln_b.reshape(1, d)]
    if has_next:
        in_specs += mod(layer + 1, 0) + mod(layer + 1, 1)
        ins += [*mods, *mods]
        out_specs = [xspec, pl.BlockSpec((tm, d), lambda i, pos: (i, 0))]
        out_shape = [jax.ShapeDtypeStruct(x1.shape, F32), jax.ShapeDtypeStruct((t, d), BF16)]
    else:
        out_specs = [pl.BlockSpec((gt, s, d), lambda i, pos: (jnp.minimum(i, n_p - 1), 0, 0)),
                     pl.BlockSpec((gt, s, d), lambda i, pos: (jnp.maximum(i - n_p, 0), 0, 0))]
        out_shape = [jax.ShapeDtypeStruct((t_prompt // s, s, d), F32),
                     jax.ShapeDtypeStruct(((t - t_prompt) // s, s, d), F32)]
    grid_spec = pltpu.PrefetchScalarGridSpec(
        num_scalar_prefetch=1,
        grid=(t // tm,),
        in_specs=in_specs,
        out_specs=out_specs,
        scratch_shapes=[pltpu.VMEM((3, TOP_K_INNER, tm, d), F32), pltpu.SemaphoreType.DMA((3,))],
    )
    return pl.pallas_call(
        functools.partial(_comb_kernel, alpha=alpha, tm=tm, t_total=t, has_next=has_next, n_prompt_tiles=n_p,
                          tiles_per_seq=seq // tm),
        grid_spec=grid_spec,
        out_shape=out_shape,
        compiler_params=_params("arbitrary"),
        name="comb",
    )(pos_flat, *ins)


def _gla_tables(rows, span):
    t = np.arange(rows)[:, None]
    u = np.arange(rows)[None, :]
    same = (t // span) == (u // span)
    mats = [same & (u <= t), same]
    n_levels, m = 0, 1
    while m < span:
        mid = (t // (2 * m)) * (2 * m) + m - 1
        mats.append(same & np.where(t > mid, (u > mid) & (u <= t), (u > t) & (u <= mid)))
        m *= 2
        n_levels += 1
    x = t ^ u
    level = np.where(x > 0, np.floor(np.log2(np.maximum(x, 1))), n_levels).astype(np.int32)
    lid = np.where(same & (u <= t), level, -1).astype(np.int32)
    grp = (t // span) == np.arange(LANES_V7X)[None, :]
    stack = np.concatenate(mats, axis=0).astype(np.float32)
    return jnp.asarray(stack, BF16), jnp.asarray(lid), jnp.asarray(grp.astype(np.float32), BF16), n_levels


def _gla_block(q, k, la, stack, lid, grp, n_levels):
    rows = q.shape[0]
    hi, lo = _split_bf16(la)
    dall = _dot(stack, hi)
    d_lo = _dot(stack[0:2 * rows, :], lo)
    cum, total = dall[0:rows] + d_lo[0:rows], dall[rows:2 * rows] + d_lo[rows:2 * rows]
    group_total = (lax.dot_general(hi, grp, TN_DIMS, preferred_element_type=F32)
                   + lax.dot_general(lo, grp, TN_DIMS, preferred_element_type=F32))
    nt = lambda a, b: lax.dot_general(a.astype(BF16), b.astype(BF16), NT_DIMS, preferred_element_type=F32)
    sc = jnp.where(lid == n_levels, nt(q, k), 0.0)
    for l in range(n_levels):
        e = jnp.exp(dall[(2 + l) * rows:(3 + l) * rows])
        sc = jnp.where(lid == l, nt(q * e, k * e), sc)
    return sc.astype(BF16), cum, total, group_total


def _gla_log_decay(gl, wgk, bg_row):
    return _log_sigmoid(_dot(gl, wgk) + bg_row) * (1.0 / GLA_GATE_NORMALIZER)


def _rms_gate(o, ng, gate):
    on = o * lax.rsqrt(jnp.mean(o * o, axis=-1, keepdims=True) + RMS_EPS) * ng
    return (on * _silu(gate)).astype(BF16)


def _gla_prompt_kernel(q_ref, k_ref, v_ref, g_ref, gl_ref, wgk_ref, bgr_ref, ng_ref, stack_ref, lid_ref, grp_ref,
                       y_ref, sout_ref, st_ref, *, n_heads, scale, n_levels):
    j = pl.program_id(1)

    @pl.when(j == 0)
    def _():
        st_ref[...] = jnp.zeros_like(st_ref)

    dk = q_ref.shape[1] // n_heads
    dv = v_ref.shape[1] // n_heads
    gl = gl_ref[...]
    for h in range(n_heads):
        ks, vs = slice(h * dk, (h + 1) * dk), slice(h * dv, (h + 1) * dv)
        q = q_ref[:, ks].astype(F32) * scale
        k = k_ref[:, ks].astype(F32)
        v = v_ref[:, vs]
        la = _gla_log_decay(gl, wgk_ref[:, ks], bgr_ref[:, ks])
        sc, cum, total, group_total = _gla_block(q, k, la, stack_ref[...], lid_ref[...], grp_ref[...], n_levels)
        s_old = st_ref[h]
        o = _dot(sc, v) + _dot((q * jnp.exp(cum)).astype(BF16), s_old.astype(BF16))
        kd = (k * jnp.exp(total - cum)).astype(BF16)
        st_ref[h] = s_old * jnp.exp(group_total[:, 0:1]) + lax.dot_general(
            kd, v, TN_DIMS, preferred_element_type=F32)
        y_ref[:, vs] = _rms_gate(o, ng_ref[...], g_ref[:, vs].astype(F32))

    @pl.when(j == pl.num_programs(1) - 1)
    def _():
        sout_ref[...] = st_ref[...]


def _gla_sample_kernel(q_ref, k_ref, v_ref, g_ref, gl_ref, wgk_ref, bgr_ref, ng_ref, stack_ref, lid_ref, grp_ref,
                       sin_ref, y_ref, sout_ref, *, n_heads, scale, seq, n_levels):
    rows = q_ref.shape[0]
    nb = rows // seq
    dk = q_ref.shape[1] // n_heads
    dv = v_ref.shape[1] // n_heads
    gl = gl_ref[...]
    for h in range(n_heads):
        ks, vs = slice(h * dk, (h + 1) * dk), slice(h * dv, (h + 1) * dv)
        q = q_ref[:, ks].astype(F32) * scale
        k = k_ref[:, ks].astype(F32)
        v = v_ref[:, vs]
        la = _gla_log_decay(gl, wgk_ref[:, ks], bgr_ref[:, ks])
        sc, cum, total, group_total = _gla_block(q, k, la, stack_ref[...], lid_ref[...], grp_ref[...], n_levels)
        q_dec = q * jnp.exp(cum)
        kd = k * jnp.exp(total - cum)
        v32 = v.astype(F32)
        o_state = []
        for s in range(nb):
            rs = slice(s * seq, (s + 1) * seq)
            s_old = sin_ref[s, h]
            o_state.append(_dot(q_dec[rs, :].astype(BF16), s_old.astype(BF16)))
            sout_ref[s, h] = s_old * jnp.exp(group_total[:, s:s + 1]) + lax.dot_general(
                kd[rs, :].astype(BF16), v32[rs, :].astype(BF16), TN_DIMS, preferred_element_type=F32)
        o = _dot(sc, v) + jnp.concatenate(o_state, axis=0)
        y_ref[:, vs] = _rms_gate(o, ng_ref[...], g_ref[:, vs].astype(F32))


def _gla_kernel(*refs, n_prompt_in, n_sample_in, prompt_args, sample_args):
    p_in, s_in = refs[:n_prompt_in], refs[n_prompt_in:n_prompt_in + n_sample_in]
    yp_ref, sp_ref, ys_ref, ss_ref, st_ref = refs[n_prompt_in + n_sample_in:]
    _gla_prompt_kernel(*p_in, yp_ref, sp_ref, st_ref, **prompt_args)
    _gla_sample_kernel(*s_in, ys_ref, ss_ref, **sample_args)


def _gla(p, gl, wgk, bg_row, ng, state_s, *, t_prompt, seq, n_heads, dk, dv, bk=256):
    t = p.shape[0]
    bs, _, _, _ = state_s.shape
    s_len = (t - t_prompt) // bs
    bp = t_prompt // seq
    dkt, dvt = n_heads * dk, n_heads * dv
    assert dvt == 2 * dkt
    scale = dk ** -0.5
    r = gl.shape[1]
    const = lambda *_: (0, 0)

    def tables(rows, span):
        stack, lid, grp, n_levels = _gla_tables(rows, span)
        specs = [pl.BlockSpec((r, dkt), const), pl.BlockSpec((1, dkt), const), pl.BlockSpec((1, dv), const),
                 pl.BlockSpec(stack.shape, const), pl.BlockSpec(lid.shape, const), pl.BlockSpec(grp.shape, const)]
        return specs, [wgk, bg_row, ng, stack, lid, grp], n_levels

    nblk = seq // bk
    n_steps = bp * nblk
    assert bs % n_steps == 0
    nb = bs // n_steps
    rows = nb * s_len
    off = t_prompt // rows
    rowp = lambda b, j: b * nblk + j
    rows_s = lambda b, j: off + rowp(b, j)
    p_specs, p_ins, levels_p = tables(bk, bk)
    s_specs, s_ins, levels_s = tables(rows, s_len)
    prompt_in = [pl.BlockSpec((bk, dkt), lambda b, j: (rowp(b, j), 0)),
                 pl.BlockSpec((bk, dkt), lambda b, j: (rowp(b, j), 1)),
                 pl.BlockSpec((bk, dvt), lambda b, j: (rowp(b, j), 1)),
                 pl.BlockSpec((bk, dvt), lambda b, j: (rowp(b, j), 2)),
                 pl.BlockSpec((bk, r), lambda b, j: (rowp(b, j), 0))] + p_specs
    sample_in = [pl.BlockSpec((rows, dkt), lambda b, j: (rows_s(b, j), 0)),
                 pl.BlockSpec((rows, dkt), lambda b, j: (rows_s(b, j), 1)),
                 pl.BlockSpec((rows, dvt), lambda b, j: (rows_s(b, j), 1)),
                 pl.BlockSpec((rows, dvt), lambda b, j: (rows_s(b, j), 2)),
                 pl.BlockSpec((rows, r), lambda b, j: (rows_s(b, j), 0))] + s_specs + [
                 pl.BlockSpec((nb, n_heads, dk, dv), lambda b, j: (rowp(b, j), 0, 0, 0))]
    y_p, s_p, y_s, s_s = pl.pallas_call(
        functools.partial(
            _gla_kernel, n_prompt_in=len(prompt_in), n_sample_in=len(sample_in),
            prompt_args=dict(n_heads=n_heads, scale=scale, n_levels=levels_p),
            sample_args=dict(n_heads=n_heads, scale=scale, seq=s_len, n_levels=levels_s)),
        grid=(bp, nblk),
        in_specs=prompt_in + sample_in,
        out_specs=[pl.BlockSpec((bk, dvt), lambda b, j: (rowp(b, j), 0)),
                   pl.BlockSpec((None, n_heads, dk, dv), lambda b, j: (b, 0, 0, 0)),
                   pl.BlockSpec((rows, dvt), lambda b, j: (rowp(b, j), 0)),
                   pl.BlockSpec((nb, n_heads, dk, dv), lambda b, j: (rowp(b, j), 0, 0, 0))],
        out_shape=[jax.ShapeDtypeStruct((t_prompt, dvt), BF16),
                   jax.ShapeDtypeStruct((bp, n_heads, dk, dv), F32),
                   jax.ShapeDtypeStruct((t - t_prompt, dvt), BF16),
                   jax.ShapeDtypeStruct(state_s.shape, F32)],
        scratch_shapes=[pltpu.VMEM((n_heads, dk, dv), F32)],
        compiler_params=_params("arbitrary", "arbitrary"),
        name="gla",
    )(p, p, p, p, gl, *p_ins, p, p, p, p, gl, *s_ins, state_s)
    return (y_p, y_s), s_p, s_s


def _moe_schedule(eid, rank, counts, *, tg, n_tiles):
    n_e = counts.shape[0]
    e_ids = jnp.arange(n_e, dtype=I32)
    padded = ((counts + tg - 1) // tg) * tg
    ends = jnp.sum(jnp.where(e_ids[None, :] <= e_ids[:, None], padded[None, :], 0), axis=1)
    starts = ends - padded
    pos = jnp.sum(jnp.where(eid[None] == e_ids[:, None, None], starts[:, None, None], 0), axis=0) + rank
    n_used = ends[n_e - 1] // tg
    tile_start = jnp.arange(n_tiles, dtype=I32) * tg
    te = jnp.sum((ends[None, :] <= tile_start[:, None]).astype(I32), axis=1)
    te_last = jnp.sum((ends <= (n_used - 1) * tg).astype(I32))
    te = jnp.where(jnp.arange(n_tiles) < n_used, te, te_last)
    later = (e_ids[None, :] > e_ids[:, None]) & (counts[None, :] > 0)
    nxt = jnp.min(jnp.where(later, e_ids[None, :], n_e), axis=1)
    nxt = jnp.where(nxt == n_e, -1, nxt)
    i32 = lambda a: a.astype(I32)
    return (i32(pos.reshape(-1)), i32(starts + counts), i32(padded - counts), i32(te), i32(nxt),
            i32(n_used.reshape(1)))


def kernel(x_prompt, x_sample, cache_conv, state_gla, c_prompt, c_sample, w_mod, b_mod, ln_g, ln_b, ab_w_in, ab_conv_w, ab_v_ln_g, ab_v_ln_b, ab_w_s, ab_b_s, ab_w_out, gla_w_in, gla_w_gk, gla_b_gk, gla_norm_g, gla_w_out, moe_w_grp, moe_b_grp, moe_w_rt, moe_b_rt, moe_w1, moe_w3, moe_w2):
    bp, seq, d = x_prompt.shape
    bs, s_len, _ = x_sample.shape
    assert s_len == SUBLANES_V7X and seq % SUBLANES_V7X == 0
    depth = w_mod.shape[0]
    alpha = float((2 * depth) ** 0.25)
    t_p, t_s = bp * seq, bs * s_len
    t = t_p + t_s
    n_groups, n_exp = moe_w_rt.shape[1], moe_w_rt.shape[3]
    n_e = n_groups * n_exp
    d_ff = moe_w1.shape[-1]
    tg = 256
    n_tiles = (TOP_K_INNER * t) // tg + n_e

    x = (x_prompt.reshape(t_p // SUBLANES_V7X, SUBLANES_V7X, d), x_sample)
    mods = _mod_vectors(c_prompt, c_sample, w_mod, b_mod)

    w1 = moe_w1.reshape(depth * n_e, d, d_ff)
    w3 = moe_w3.reshape(depth * n_e, d, d_ff)
    w2 = moe_w2.reshape(depth * n_e, d_ff, d)

    conv_p, conv_s, chunk_v, gla_p, gla_s = [], [], [], [], []
    h_bf = None
    for layer in range(depth):
        li = layer // 2
        if layer % 2 == 0:
            n_heads, chunk = ab_w_s.shape[1], ab_w_s.shape[2]
            dc = ab_conv_w.shape[-1]
            if h_bf is None:
                p = _mm(_modulate(x, mods, layer, 0, 1, seq=seq), ab_w_in, li, ab_w_in.shape[-1])
            else:
                p = _mm(h_bf, ab_w_in, li, ab_w_in.shape[-1])
            w_s = ab_w_s[li]
            wm_p = jnp.tril(w_s)
            reps = chunk // s_len
            blk = jnp.tril(w_s[:, :s_len, :s_len])
            wm_s = jnp.einsum("ab,hts->hatbs", jnp.eye(reps, dtype=F32), blk).reshape(n_heads, chunk, chunk)
            wm = jnp.stack([wm_p, wm_s]).astype(BF16)
            b_s = ab_b_s[li]
            hd = dc // n_heads
            bias_p = jnp.repeat(b_s.T, hd, axis=1)
            bias_s = jnp.repeat(jnp.tile(b_s[:, :s_len].T, (reps, 1)), hd, axis=1)
            bias = jnp.stack([bias_p, bias_s])
            y, cp_new, cs_new, vn_s = _mix0(p, cache_conv[li], ab_conv_w[li], ab_v_ln_g[li], ab_v_ln_b[li],
                                            wm, bias, t_prompt=t_p, seq=seq, n_heads=n_heads)
            conv_p.append(cp_new)
            conv_s.append(cs_new)
            chunk_v.append(vn_s)
            w_out = ab_w_out[li].astype(BF16)
        else:
            n_heads, dk, dv = state_gla.shape[2], state_gla.shape[3], state_gla.shape[4]
            dkt, dvt = n_heads * dk, n_heads * dv
            rank = gla_w_gk.shape[1]
            n_main = 2 * dkt + 2 * dvt
            w_in_t = jnp.swapaxes(gla_w_in, 1, 2)
            p = _mm(h_bf, w_in_t, li, n_main, w_transposed=True)
            w_lo = jnp.pad(w_in_t[li, n_main:, :], ((0, LANES_V7X - rank), (0, 0)))[None]
            gl = _mm(h_bf, w_lo, 0, LANES_V7X, w_transposed=True)
            wgk = jnp.pad(gla_w_gk[li], ((0, LANES_V7X - rank), (0, 0))).astype(BF16)
            y, sp_new, ss_new = _gla(p, gl, wgk, gla_b_gk[li].reshape(1, dkt),
                                     gla_norm_g[li].reshape(1, dv), state_gla[li],
                                     t_prompt=t_p, seq=seq, n_heads=n_heads, dk=dk, dv=dv)
            gla_p.append(sp_new)
            gla_s.append(ss_new)
            w_out = gla_w_out[li].astype(BF16)

        wr = jnp.concatenate([moe_w_grp[layer].T,
                              jnp.transpose(moe_w_rt[layer], (0, 2, 1)).reshape(n_e, d)], axis=0)
        wr = jnp.pad(wr, ((0, LANES_V7X - wr.shape[0]), (0, 0)))
        br = jnp.concatenate([moe_b_grp[layer], moe_b_rt[layer].reshape(n_e)])
        br = jnp.pad(br, (0, LANES_V7X - br.shape[0])).reshape(LANES_V7X, 1)
        x1, h2, eid, wt, rank_, cnt = _outln(y, w_out, x, mods, layer, ln_g[layer, 0], ln_b[layer, 0], wr, br,
                                             alpha=alpha, n_groups=n_groups, n_exp=n_exp, t_prompt=t_p, seq=seq)
        pos, zero_start, zero_count, te, nxt, n_used = _moe_schedule(eid, rank_, cnt[:, 0], tg=tg, n_tiles=n_tiles)
        xs = _dispatch(pos, zero_start, zero_count, n_used, h2, n_tiles=n_tiles, tg=tg)
        ys = _moe(xs, w1, w3, w2, layer * n_e, te, nxt, n_used, tg=tg)
        has_next = layer + 1 < depth
        outs = _comb(pos, ys, wt.T, x1, mods, layer, ln_g[layer, 1], ln_b[layer, 1],
                     alpha=alpha, has_next=has_next, t_prompt=t_p, seq=seq)
        if has_next:
            x, h_bf = outs

    y_prompt = outs[0].reshape(bp, seq, d)
    y_sample = outs[1].reshape(bs, s_len, d)
    return (y_prompt, y_sample, jnp.stack(conv_p), jnp.stack(conv_s), jnp.stack(chunk_v),
            jnp.stack(gla_p), jnp.stack(gla_s))
```

```python
import functools

import jax
import numpy as np
import jax.numpy as jnp
from jax import lax
from jax.experimental import pallas as pl
from jax.experimental.pallas import tpu as pltpu

F32 = jnp.float32
BF16 = jnp.bfloat16
I32 = jnp.int32

LN_EPS = 1e-5
RMS_EPS = 1e-6
GLA_GATE_NORMALIZER = 16.0
TOP_K_INNER = 2

SUBLANES_V7X = 8
LANES_V7X = 128
VMEM_LIMIT_V7X = 56 * 1024 * 1024

NT_DIMS = (((1,), (1,)), ((), ()))
TN_DIMS = (((0,), (0,)), ((), ()))


def _params(*sem):
    return pltpu.CompilerParams(dimension_semantics=sem, vmem_limit_bytes=VMEM_LIMIT_V7X)


def _silu(x):
    return x * (1.0 / (1.0 + jnp.exp(-x)))


def _log_sigmoid(z):
    return jnp.minimum(z, 0.0) - jnp.log(1.0 + jnp.exp(-jnp.abs(z)))


def _layer_norm(x, g, b):
    mu = jnp.mean(x, axis=-1, keepdims=True)
    xc = x - mu
    var = jnp.mean(xc * xc, axis=-1, keepdims=True)
    return xc * lax.rsqrt(var + LN_EPS) * g + b


def _dot(a, b):
    return jnp.dot(a, b, preferred_element_type=F32)


def _split_bf16(x):
    hi = x.astype(BF16)
    lo = (x - hi.astype(F32)).astype(BF16)
    return hi, lo


def _mod_kernel(c_ref, w_ref, b_ref, op_ref, os_ref):
    r = _dot(_silu(c_ref[...]).astype(BF16), w_ref[...].astype(BF16)) + b_ref[...]
    n_p = op_ref.shape[0]
    op_ref[...] = r[:n_p, :]
    os_ref[...] = r[n_p:, :]


def _mod_vectors(c_prompt, c_sample, w_mod, b_mod):
    depth, d, n = w_mod.shape
    bp, bs = c_prompt.shape[0], c_sample.shape[0]
    bp_rows = bp + (-bp) % SUBLANES_V7X
    c_all = jnp.concatenate([jnp.pad(c_prompt, ((0, bp_rows - bp), (0, 0))), c_sample], axis=0)
    tn = 1024
    return pl.pallas_call(
        _mod_kernel,
        grid=(depth, n // tn),
        in_specs=[
            pl.BlockSpec(c_all.shape, lambda l, j: (0, 0)),
            pl.BlockSpec((None, d, tn), lambda l, j: (l, 0, j)),
            pl.BlockSpec((None, 1, tn), lambda l, j: (l, 0, j)),
        ],
        out_specs=[pl.BlockSpec((None, bp_rows, tn), lambda l, j: (l, 0, j)),
                   pl.BlockSpec((None, bs, tn), lambda l, j: (l, 0, j))],
        out_shape=[jax.ShapeDtypeStruct((depth, bp_rows, n), F32), jax.ShapeDtypeStruct((depth, bs, n), F32)],
        compiler_params=_params("arbitrary", "arbitrary"),
        name="mod",
    )(c_all, w_mod, b_mod.reshape(depth, 1, n))


def _mod_specs(mods, layer, col, d, gt, n_p):
    mp, _ = mods
    return [pl.BlockSpec((None, mp.shape[1], d), lambda i, *_: (layer, 0, col)),
            pl.BlockSpec((None, gt, d), lambda i, *_: (layer, jnp.maximum(i - n_p, 0), col))]


def _mod_value(p_ref, s_ref, prompt, seq_idx):
    if prompt:
        return p_ref[pl.ds(seq_idx, 1), :][None]
    return s_ref[...][:, None, :]


def _by_group(i, n_prompt_tiles, fn):
    pl.when(i < n_prompt_tiles)(lambda: fn(True))
    pl.when(i >= n_prompt_tiles)(lambda: fn(False))


def _mm_kernel(a_ref, w_ref, o_ref, wb_ref, *, w_transposed):
    @pl.when(pl.program_id(1) == 0)
    def _():
        w = w_ref[...]
        wb_ref[...] = (w.T if w_transposed else w).astype(BF16)

    o_ref[...] = _dot(a_ref[...], wb_ref[...]).astype(o_ref.dtype)


def _mm(a, w3, w_idx, n_out, *, tm=1536, tn=1024, out_dtype=BF16, w_transposed=False):
    t = a.shape[0]
    k = w3.shape[2 if w_transposed else 1]
    tn = min(tn, n_out)
    return pl.pallas_call(
        functools.partial(_mm_kernel, w_transposed=w_transposed),
        grid=(n_out // tn, t // tm),
        in_specs=[pl.BlockSpec((tm, k), lambda j, i: (i, 0)),
                  pl.BlockSpec((None, tn, k), lambda j, i: (w_idx, j, 0)) if w_transposed
                  else pl.BlockSpec((None, k, tn), lambda j, i: (w_idx, 0, j))],
        out_specs=pl.BlockSpec((tm, tn), lambda j, i: (i, j)),
        out_shape=jax.ShapeDtypeStruct((t, n_out), out_dtype),
        scratch_shapes=[pltpu.VMEM((k, tn), BF16)],
        compiler_params=_params("arbitrary", "arbitrary"),
        name="mm",
    )(a, w3)


def _modulate_kernel(xp_ref, xs_ref, shp_ref, shs_ref, scp_ref, scs_ref, o_ref, *, n_prompt_tiles, tiles_per_seq):
    i = pl.program_id(0)

    def run(prompt):
        x = (xp_ref if prompt else xs_ref)[...]
        g, s, k = x.shape
        seq_idx = i // tiles_per_seq
        h = x * (1.0 + _mod_value(scp_ref, scs_ref, prompt, seq_idx)) + _mod_value(shp_ref, shs_ref, prompt, seq_idx)
        o_ref[...] = h.reshape(g * s, k).astype(BF16)

    _by_group(i, n_prompt_tiles, run)


def _modulate(x_pair, mods, layer, sh_col, sc_col, *, seq, tm=512):
    xp, xs = x_pair
    k = xp.shape[-1]
    gt = tm // SUBLANES_V7X
    n_p = xp.shape[0] // gt
    t = (xp.shape[0] + xs.shape[0]) * SUBLANES_V7X
    return pl.pallas_call(
        functools.partial(_modulate_kernel, n_prompt_tiles=n_p, tiles_per_seq=seq // tm),
        grid=(t // tm,),
        in_specs=[pl.BlockSpec((gt, SUBLANES_V7X, k), lambda i: (jnp.minimum(i, n_p - 1), 0, 0)),
                  pl.BlockSpec((gt, SUBLANES_V7X, k), lambda i: (jnp.maximum(i - n_p, 0), 0, 0))]
                 + _mod_specs(mods, layer, sh_col, k, gt, n_p) + _mod_specs(mods, layer, sc_col, k, gt, n_p),
        out_specs=pl.BlockSpec((tm, k), lambda i: (i, 0)),
        out_shape=jax.ShapeDtypeStruct((t, k), BF16),
        compiler_params=_params("arbitrary"),
        name="modulate",
    )(xp, xs, *mods, *mods)


def _mix0_kernel(bg_ref, cg_ref, hx_ref, u_ref, v_ref, cache_ref, cw_ref, vg_ref, vb_ref, wm_ref, bias_ref,
                 y_ref, convp_ref, convs_ref, vns_ref, zprev_ref, *, n_prompt_tiles, tiles_per_seq, n_heads):
    i = pl.program_id(0)
    tm, dc = bg_ref.shape
    ns = tm // SUBLANES_V7X
    z = cg_ref[...].astype(F32) * hx_ref[...].astype(F32)
    row = lax.broadcasted_iota(I32, (tm, dc), 0)
    r1 = pltpu.roll(z, 1, 0)
    r2 = pltpu.roll(z, 2, 0)
    cw = cw_ref[...]
    bg = bg_ref[...].astype(F32)

    vn = _layer_norm(v_ref[...].astype(F32), vg_ref[...], vb_ref[...])
    vnb = vn.astype(BF16)
    hd = dc // n_heads
    mixed = jnp.concatenate(
        [_dot(wm_ref[h], vnb[:, h * hd:(h + 1) * hd]) for h in range(n_heads)], axis=-1) + bias_ref[...]
    y_ref[:, dc:] = (u_ref[...].astype(F32) * mixed).astype(BF16)

    def conv_out(zm1, zm2):
        conv = cw[0:1, :] * zm2 + cw[1:2, :] * zm1 + cw[2:3, :] * z
        y_ref[:, :dc] = (bg * conv).astype(BF16)

    @pl.when(i < n_prompt_tiles)
    def _prompt():
        @pl.when(i % tiles_per_seq == 0)
        def _():
            zprev_ref[...] = jnp.zeros_like(zprev_ref)

        zp = zprev_ref[...]
        p1 = zp[SUBLANES_V7X - 1:SUBLANES_V7X, :]
        p2 = zp[SUBLANES_V7X - 2:SUBLANES_V7X - 1, :]
        conv_out(jnp.where(row == 0, p1, r1),
                 jnp.where(row == 0, p2, jnp.where(row == 1, p1, r2)))
        zprev_ref[...] = z[tm - SUBLANES_V7X:, :]
        convp_ref[...] = z[tm - 2:, :].reshape(1, 2, dc)

    @pl.when(i >= n_prompt_tiles)
    def _sample():
        c = cache_ref[...]
        c0 = jnp.broadcast_to(c[:, 0:1, :], (ns, SUBLANES_V7X, dc)).reshape(tm, dc)
        c1 = jnp.broadcast_to(c[:, 1:2, :], (ns, SUBLANES_V7X, dc)).reshape(tm, dc)
        rr = row % SUBLANES_V7X
        conv_out(jnp.where(rr == 0, c1, r1),
                 jnp.where(rr == 0, c0, jnp.where(rr == 1, c1, r2)))
        z3 = z.reshape(ns, SUBLANES_V7X, dc)
        convs_ref[...] = z3[:, SUBLANES_V7X - 2:, :]
        vns_ref[...] = vn.reshape(ns, SUBLANES_V7X, dc)


def _mix0(p, cache, conv_w, v_g, v_b, wm, bias, *, t_prompt, seq, n_heads):
    t, n = p.shape
    bs, cwm1, dc = cache.shape
    tm = wm.shape[-1]
    assert cwm1 == 2 and conv_w.shape[0] == 3 and n == 5 * dc and seq % tm == 0
    n_p = t_prompt // tm
    n_s = (t - t_prompt) // tm
    tps = seq // tm
    bp = t_prompt // seq
    ns = tm // SUBLANES_V7X

    def col(c):
        return pl.BlockSpec((tm, dc), lambda i: (i, c))

    def s_idx(i):
        return jnp.maximum(i - n_p, 0)

    const2 = lambda i: (0, 0)
    mode = lambda i: ((i >= n_p).astype(I32), 0, 0, 0)
    return pl.pallas_call(
        functools.partial(_mix0_kernel, n_prompt_tiles=n_p, tiles_per_seq=tps, n_heads=n_heads),
        grid=(n_p + n_s,),
        in_specs=[col(0), col(1), col(2), col(3), col(4),
                  pl.BlockSpec((ns, 2, dc), lambda i: (s_idx(i), 0, 0)),
                  pl.BlockSpec((3, dc), const2),
                  pl.BlockSpec((1, dc), const2),
                  pl.BlockSpec((1, dc), const2),
                  pl.BlockSpec((None, n_heads, tm, tm), mode),
                  pl.BlockSpec((None, tm, dc), lambda i: ((i >= n_p).astype(I32), 0, 0))],
        out_specs=[pl.BlockSpec((tm, 2 * dc), lambda i: (i, 0)),
                   pl.BlockSpec((1, 2, dc), lambda i: (jnp.minimum(i // tps, bp - 1), 0, 0)),
                   pl.BlockSpec((ns, 2, dc), lambda i: (s_idx(i), 0, 0)),
                   pl.BlockSpec((ns, SUBLANES_V7X, dc), lambda i: (s_idx(i), 0, 0))],
        out_shape=[jax.ShapeDtypeStruct((t, 2 * dc), BF16),
                   jax.ShapeDtypeStruct((bp, 2, dc), F32),
                   jax.ShapeDtypeStruct((bs, 2, dc), F32),
                   jax.ShapeDtypeStruct((bs, SUBLANES_V7X, dc), F32)],
        scratch_shapes=[pltpu.VMEM((SUBLANES_V7X, dc), F32)],
        compiler_params=_params("arbitrary"),
        name="mix0",
    )(p, p, p, p, p, cache, conv_w, v_g.reshape(1, dc), v_b.reshape(1, dc), wm, bias)


def _first_index_of(vals, target):
    idx = jnp.full(target.shape, len(vals) - 1, I32)
    for j in reversed(range(len(vals))):
        idx = jnp.where(vals[j] == target, j, idx)
    return idx


def _softmax_rows(rows):
    m = functools.reduce(jnp.maximum, rows)
    e = [jnp.exp(r - m) for r in rows]
    s = functools.reduce(lambda a, b: a + b, e)
    return [x / s for x in e]


def _route(h, wr_ref, br_ref, eid_ref, wt_ref, rank_ref, cnt_ref, carry_ref, n_groups, n_exp):
    tm = h.shape[0]
    hh, hl = _split_bf16(h)
    wh, wl = _split_bf16(wr_ref[...])
    dg = lambda a, b: lax.dot_general(a, b, NT_DIMS, preferred_element_type=F32)
    logits = dg(wh, hh) + dg(wh, hl) + dg(wl, hh) + br_ref[...]

    g_prob = _softmax_rows([logits[g:g + 1, :] for g in range(n_groups)])
    g_top = functools.reduce(jnp.maximum, g_prob)
    g_idx = _first_index_of(g_prob, g_top)

    e_sel = []
    for e in range(n_exp):
        sel = logits[n_groups + e:n_groups + e + 1, :]
        for g in range(1, n_groups):
            r = n_groups + g * n_exp + e
            sel = jnp.where(g_idx == g, logits[r:r + 1, :], sel)
        e_sel.append(sel)
    e_prob = _softmax_rows(e_sel)
    p1 = functools.reduce(jnp.maximum, e_prob)
    i1 = _first_index_of(e_prob, p1)
    rest = [jnp.where(i1 == e, -1.0, e_prob[e]) for e in range(n_exp)]
    p2 = functools.reduce(jnp.maximum, rest)
    i2 = _first_index_of(rest, p2)
    den = p1 + p2
    wt_ref[0:1, :] = g_top * (p1 / den)
    wt_ref[1:2, :] = g_top * (p2 / den)
    eid0 = g_idx * n_exp + i1
    eid1 = g_idx * n_exp + i2
    eid_ref[0:1, :] = eid0
    eid_ref[1:2, :] = eid1

    n_e = n_groups * n_exp
    eio = lax.broadcasted_iota(I32, (n_e, tm), 0)
    oh0 = (eio == eid0).astype(F32)
    oh1 = (eio == eid1).astype(F32)
    oh = oh0 + oh1
    before = (lax.broadcasted_iota(I32, (tm, tm), 0) < lax.broadcasted_iota(I32, (tm, tm), 1)).astype(BF16)
    base = _dot(oh.astype(BF16), before) + carry_ref[...]
    rank_ref[0:1, :] = jnp.sum(oh0 * base, axis=0, keepdims=True).astype(I32)
    rank_ref[1:2, :] = jnp.sum(oh1 * base, axis=0, keepdims=True).astype(I32)
    total = carry_ref[...] + jnp.sum(oh, axis=1, keepdims=True)
    carry_ref[...] = total
    cnt_ref[...] = jnp.broadcast_to(total, cnt_ref.shape).astype(I32)


def _outln_kernel(*refs, alpha, n_groups, n_exp, y_split, x_split, n_prompt_tiles, tiles_per_seq):
    refs = list(refs)
    i = pl.program_id(0)
    take = lambda n: [refs.pop(0) for _ in range(n)]
    y_refs, (w_ref,), x_refs = take(2 if y_split else 1), take(1), take(2 if x_split else 1)
    gt_refs, sh_refs, sc_refs = take(2), take(2), take(2)
    (lng_ref, lnb_ref, wr_ref, br_ref, x1_ref, h_ref, eid_ref, wt_ref, rank_ref, cnt_ref, carry_ref) = refs

    @pl.when(i == 0)
    def _():
        carry_ref[...] = jnp.zeros_like(carry_ref)

    def run(prompt):
        pick = lambda pair: pair[0 if prompt or len(pair) == 1 else 1]
        mod = lambda pair: _mod_value(*pair, prompt, i // tiles_per_seq)
        x = pick(x_refs)[...]
        g, s, d = x.shape
        m = _dot(pick(y_refs)[...], w_ref[...]).reshape(g, s, d)
        x1 = _layer_norm(alpha * x + mod(gt_refs) * m, lng_ref[...], lnb_ref[...])
        x1_ref[...] = x1
        h = (x1 * (1.0 + mod(sc_refs)) + mod(sh_refs)).reshape(g * s, d)
        h_ref[...] = h
        _route(h, wr_ref, br_ref, eid_ref, wt_ref, rank_ref, cnt_ref, carry_ref, n_groups, n_exp)

    _by_group(i, n_prompt_tiles, run)


def _outln(y, w_bf, x, mods, layer, ln_g, ln_b, wr, br, *, alpha, n_groups, n_exp, t_prompt, seq, tm=512):
    y_split, x_split = isinstance(y, tuple), isinstance(x, tuple)
    ys, xs = (y if y_split else (y,)), (x if x_split else (x,))
    k, d = w_bf.shape
    gt = tm // SUBLANES_V7X
    n_e = n_groups * n_exp
    t = sum(a.shape[0] for a in ys)
    n_p = t_prompt // tm

    def split_specs(block, n_arrays):
        if n_arrays == 1:
            return [pl.BlockSpec(block, lambda i: (i,) + (0,) * (len(block) - 1))]
        return [pl.BlockSpec(block, lambda i: (jnp.minimum(i, n_p - 1),) + (0,) * (len(block) - 1)),
                pl.BlockSpec(block, lambda i: (jnp.maximum(i - n_p, 0),) + (0,) * (len(block) - 1))]

    mod = lambda c: _mod_specs(mods, layer, c, d, gt, n_p)
    const = lambda i: (0, 0)
    pair = lambda dt: jax.ShapeDtypeStruct((TOP_K_INNER, t), dt)
    pair_spec = pl.BlockSpec((TOP_K_INNER, tm), lambda i: (0, i))
    return pl.pallas_call(
        functools.partial(_outln_kernel, alpha=alpha, n_groups=n_groups, n_exp=n_exp,
                          y_split=y_split, x_split=x_split, n_prompt_tiles=n_p, tiles_per_seq=seq // tm),
        grid=(t // tm,),
        in_specs=split_specs((tm, k), len(ys))
                 + [pl.BlockSpec((k, d), const, pipeline_mode=pl.Buffered(1))]
                 + split_specs((gt, SUBLANES_V7X, d), len(xs))
                 + mod(2) + mod(3) + mod(4)
                 + [pl.BlockSpec((1, d), const),
                    pl.BlockSpec((1, d), const),
                    pl.BlockSpec(wr.shape, const),
                    pl.BlockSpec(br.shape, const)],
        out_specs=[pl.BlockSpec((gt, SUBLANES_V7X, d), lambda i: (i, 0, 0)),
                   pl.BlockSpec((tm, d), lambda i: (i, 0)),
                   pair_spec, pair_spec, pair_spec,
                   pl.BlockSpec((n_e, LANES_V7X), const)],
        out_shape=[jax.ShapeDtypeStruct((t // SUBLANES_V7X, SUBLANES_V7X, d), F32),
                   jax.ShapeDtypeStruct((t, d), F32),
                   pair(I32), pair(F32), pair(I32),
                   jax.ShapeDtypeStruct((n_e, LANES_V7X), I32)],
        scratch_shapes=[pltpu.VMEM((n_e, 1), F32)],
        compiler_params=_params("arbitrary"),
        name="outln",
    )(*ys, w_bf, *xs, *mods, *mods, *mods, ln_g.reshape(1, d), ln_b.reshape(1, d), wr, br)


ROW_DMA_UNROLL = 8


def _row_gather_start(src_hbm, dst, sem, idx_ref, base, n):
    def body(q, c):
        for u in range(ROW_DMA_UNROLL):
            r = q * ROW_DMA_UNROLL + u
            row = idx_ref[base + r]
            pltpu.make_async_copy(src_hbm.at[pl.ds(row, 1)], dst.at[pl.ds(r, 1)], sem).start(priority=u % 2)
        return c
    lax.fori_loop(0, n // ROW_DMA_UNROLL, body, 0)


def _row_gather_wait(src_hbm, dst, sem, n):
    pltpu.make_async_copy(src_hbm.at[pl.ds(0, n)], dst, sem).wait()


def _dispatch_kernel(pos_ref, zs_ref, zn_ref, nu_ref, h_ref, xs_hbm, zrow_ref, ztile_ref, sem, zsem,
                     *, t_total, n_e, first_free_tile):
    i = pl.program_id(0)
    tm = h_ref.shape[0]
    tg = ztile_ref.shape[0]
    n_tiles = xs_hbm.shape[0] // tg
    n_used = nu_ref[0]

    @pl.when(i == 0)
    def _():
        zrow_ref[...] = jnp.zeros_like(zrow_ref)
        ztile_ref[...] = jnp.zeros_like(ztile_ref)
        pad_row = lambda p: pltpu.make_async_copy(zrow_ref.at[pl.ds(0, 1)], xs_hbm.at[pl.ds(p, 1)], zsem)
        tail_tile = lambda j: pltpu.make_async_copy(ztile_ref, xs_hbm.at[pl.ds(j * tg, tg)], zsem)

        def pads(do):
            for e in range(n_e):
                base = zs_ref[e]

                def body(r, c):
                    do(pad_row(base + r))
                    return c
                lax.fori_loop(0, zn_ref[e], body, 0)
            for j in range(first_free_tile, n_tiles):
                @pl.when(j >= n_used)
                def _():
                    do(tail_tile(j))

        pads(lambda cp: cp.start())
        pads(lambda cp: cp.wait())

    for k in range(TOP_K_INNER):
        def body(q, c):
            for u in range(ROW_DMA_UNROLL):
                r = q * ROW_DMA_UNROLL + u
                p = pos_ref[k * t_total + i * tm + r]
                pltpu.make_async_copy(h_ref.at[pl.ds(r, 1)], xs_hbm.at[pl.ds(p, 1)], sem).start(priority=u % 2)
            return c
        lax.fori_loop(0, tm // ROW_DMA_UNROLL, body, 0)
    for k in range(TOP_K_INNER):
        pltpu.make_async_copy(h_ref, xs_hbm.at[pl.ds(0, tm)], sem).wait()


def _dispatch(pos, zero_start, zero_count, n_used, h, *, n_tiles, tg, tm=1024):
    t, d = h.shape
    n_e = zero_start.shape[0]
    grid_spec = pltpu.PrefetchScalarGridSpec(
        num_scalar_prefetch=4,
        grid=(t // tm,),
        in_specs=[pl.BlockSpec((tm, d), lambda i, *_: (i, 0))],
        out_specs=pl.BlockSpec(memory_space=pl.ANY),
        scratch_shapes=[pltpu.VMEM((SUBLANES_V7X, d), h.dtype), pltpu.VMEM((tg, d), h.dtype),
                        pltpu.SemaphoreType.DMA(()), pltpu.SemaphoreType.DMA(())],
    )
    return pl.pallas_call(
        functools.partial(_dispatch_kernel, t_total=t, n_e=n_e, first_free_tile=(TOP_K_INNER * t) // tg),
        grid_spec=grid_spec,
        out_shape=jax.ShapeDtypeStruct((n_tiles * tg, d), h.dtype),
        compiler_params=_params("arbitrary"),
        name="dispatch",
    )(pos, zero_start, zero_count, n_used, h)


def _moe_kernel(te_ref, nxt_ref, nu_ref, xs_ref, w1_hbm, w3_hbm, w2_hbm, o_ref,
                w1s, w3s, w2s, w1b, w3b, w2b, wslot_ref, wsem, *, w_base):
    i = pl.program_id(0)
    n_used = nu_ref[0]

    def weight_copies(e, slot):
        return [pltpu.make_async_copy(hbm.at[w_base + e], stage.at[slot], wsem.at[slot])
                for hbm, stage in ((w1_hbm, w1s), (w3_hbm, w3s), (w2_hbm, w2s))]

    @pl.when(i == 0)
    def _():
        wslot_ref[0] = 1
        for cp in weight_copies(te_ref[0], 0):
            cp.start()

    @pl.when(i < n_used)
    def _():
        e = te_ref[i]

        @pl.when((i == 0) | (e != te_ref[jnp.maximum(i - 1, 0)]))
        def _():
            ws = 1 - wslot_ref[0]
            wslot_ref[0] = ws
            for cp in weight_copies(e, ws):
                cp.wait()
            w1b[...] = w1s[ws].astype(BF16)
            w3b[...] = w3s[ws].astype(BF16)
            w2b[...] = w2s[ws].astype(BF16)
            ne = nxt_ref[e]

            @pl.when(ne >= 0)
            def _():
                for cp in weight_copies(ne, 1 - ws):
                    cp.start()

        x = xs_ref[...].astype(BF16)
        a = _dot(x, w1b[...])
        b = _dot(x, w3b[...])
        o_ref[...] = _dot((_silu(a) * b).astype(BF16), w2b[...])

    @pl.when(i >= n_used)
    def _():
        o_ref[...] = jnp.zeros_like(o_ref)


def _moe(xs, w1, w3, w2, w_base, te, nxt, n_used, *, tg):
    p_tot, d = xs.shape
    f = w1.shape[-1]
    any_spec = pl.BlockSpec(memory_space=pl.ANY)
    grid_spec = pltpu.PrefetchScalarGridSpec(
        num_scalar_prefetch=3,
        grid=(p_tot // tg,),
        in_specs=[pl.BlockSpec((tg, d), lambda i, te, nxt, nu: (jnp.minimum(i, nu[0] - 1), 0)),
                  any_spec, any_spec, any_spec],
        out_specs=pl.BlockSpec((tg, d), lambda i, *_: (i, 0)),
        scratch_shapes=[pltpu.VMEM((2, d, f), F32), pltpu.VMEM((2, d, f), F32), pltpu.VMEM((2, f, d), F32),
                        pltpu.VMEM((d, f), BF16), pltpu.VMEM((d, f), BF16), pltpu.VMEM((f, d), BF16),
                        pltpu.SMEM((1,), I32), pltpu.SemaphoreType.DMA((2,))],
    )
    return pl.pallas_call(
        functools.partial(_moe_kernel, w_base=w_base),
        grid_spec=grid_spec,
        out_shape=jax.ShapeDtypeStruct((p_tot, d), F32),
        compiler_params=_params("arbitrary"),
        name="moe",
    )(te, nxt, n_used, xs, w1, w3, w2)


def _comb_kernel(pos_ref, y_hbm, wt_ref, x_ref, gtp_ref, gts_ref, lng_ref, lnb_ref, *rest,
                 alpha, tm, t_total, has_next, n_prompt_tiles, tiles_per_seq):
    if has_next:
        shp_ref, shs_ref, scp_ref, scs_ref, x2_ref, hn_ref, ybuf, sem = rest
    else:
        x2p_ref, x2s_ref, ybuf, sem = rest
    i = pl.program_id(0)
    n = pl.num_programs(0)

    n_slots = ybuf.shape[0]

    def wait(slot):
        for k in range(TOP_K_INNER):
            _row_gather_wait(y_hbm, ybuf.at[slot, k], sem.at[slot], tm)

    @pl.when(i == 0)
    def _():
        for tile in range(n_slots - 1):
            for k in range(TOP_K_INNER):
                _row_gather_start(y_hbm, ybuf.at[tile, k], sem.at[tile], pos_ref, k * t_total + tile * tm, tm)

    slot = i % n_slots
    wait(slot)

    def start_ahead():
        tile = jnp.minimum(i + n_slots - 1, n - 1)
        nslot = (i + n_slots - 1) % n_slots
        for k in range(TOP_K_INNER):
            for r in range(tm):
                row = pos_ref[k * t_total + tile * tm + r]
                pltpu.make_async_copy(y_hbm.at[pl.ds(row, 1)], ybuf.at[nslot, k, pl.ds(r, 1)],
                                      sem.at[nslot]).start(priority=r % 2)

    def run(prompt):
        start_ahead()
        seq_idx = i // tiles_per_seq
        w = wt_ref[...]
        f = w[:, 0:1] * ybuf[slot, 0] + w[:, 1:2] * ybuf[slot, 1]
        g, s, d = x_ref.shape
        x2 = _layer_norm(alpha * x_ref[...] + _mod_value(gtp_ref, gts_ref, prompt, seq_idx) * f.reshape(g, s, d),
                         lng_ref[...], lnb_ref[...])
        if has_next:
            x2_ref[...] = x2
            hn = (x2 * (1.0 + _mod_value(scp_ref, scs_ref, prompt, seq_idx))
                  + _mod_value(shp_ref, shs_ref, prompt, seq_idx))
            hn_ref[...] = hn.reshape(g * s, d).astype(BF16)
        else:
            (x2p_ref if prompt else x2s_ref)[...] = x2

    _by_group(i, n_prompt_tiles, run)

    @pl.when(i == n - 1)
    def _():
        for ahead in range(1, n_slots):
            wait((i + ahead) % n_slots)


def _comb(pos_flat, y_sorted, wt_t, x1, mods, layer, ln_g, ln_b, *, alpha, has_next, t_prompt, seq, tm=256):
    g_total, s, d = x1.shape
    t = g_total * s
    gt = tm // SUBLANES_V7X
    n_p = t_prompt // tm
    mod = lambda l, c: _mod_specs(mods, l, c, d, gt, n_p)
    xspec = pl.BlockSpec((gt, s, d), lambda i, pos: (i, 0, 0))
    vec = pl.BlockSpec((1, d), lambda i, pos: (0, 0))
    in_specs = [pl.BlockSpec(memory_space=pl.ANY),
                pl.BlockSpec((tm, TOP_K_INNER), lambda i, pos: (i, 0)),
                xspec] + mod(layer, 5) + [vec, vec]
    ins = [y_sorted, wt_t, x1, *mods, ln_g.reshape(1, d), ln_b.reshape(1, d)]
    if has_next:
        in_specs += mod(layer + 1, 0) + mod(layer + 1, 1)
        ins += [*mods, *mods]
        out_specs = [xspec, pl.BlockSpec((tm, d), lambda i, pos: (i, 0))]
        out_shape = [jax.ShapeDtypeStruct(x1.shape, F32), jax.ShapeDtypeStruct((t, d), BF16)]
    else:
        out_specs = [pl.BlockSpec((gt, s, d), lambda i, pos: (jnp.minimum(i, n_p - 1), 0, 0)),
                     pl.BlockSpec((gt, s, d), lambda i, pos: (jnp.maximum(i - n_p, 0), 0, 0))]
        out_shape = [jax.ShapeDtypeStruct((t_prompt // s, s, d), F32),
                     jax.ShapeDtypeStruct(((t - t_prompt) // s, s, d), F32)]
    grid_spec = pltpu.PrefetchScalarGridSpec(
        num_scalar_prefetch=1,
        grid=(t // tm,),
        in_specs=in_specs,
        out_specs=out_specs,
        scratch_shapes=[pltpu.VMEM((3, TOP_K_INNER, tm, d), F32), pltpu.SemaphoreType.DMA((3,))],
    )
    return pl.pallas_call(
        functools.partial(_comb_kernel, alpha=alpha, tm=tm, t_total=t, has_next=has_next, n_prompt_tiles=n_p,
                          tiles_per_seq=seq // tm),
        grid_spec=grid_spec,
        out_shape=out_shape,
        compiler_params=_params("arbitrary"),
        name="comb",
    )(pos_flat, *ins)


def _gla_tables(rows, span):
    t = np.arange(rows)[:, None]
    u = np.arange(rows)[None, :]
    same = (t // span) == (u // span)
    mats = [same & (u <= t), same]
    n_levels, m = 0, 1
    while m < span:
        mid = (t // (2 * m)) * (2 * m) + m - 1
        mats.append(same & np.where(t > mid, (u > mid) & (u <= t), (u > t) & (u <= mid)))
        m *= 2
        n_levels += 1
    x = t ^ u
    level = np.where(x > 0, np.floor(np.log2(np.maximum(x, 1))), n_levels).astype(np.int32)
    lid = np.where(same & (u <= t), level, -1).astype(np.int32)
    grp = (t // span) == np.arange(LANES_V7X)[None, :]
    stack = np.concatenate(mats, axis=0).astype(np.float32)
    return jnp.asarray(stack, BF16), jnp.asarray(lid), jnp.asarray(grp.astype(np.float32), BF16), n_levels


def _gla_block(q, k, la, stack, lid, grp, n_levels):
    rows = q.shape[0]
    hi, lo = _split_bf16(la)
    dall = _dot(stack, hi)
    d_lo = _dot(stack[0:2 * rows, :], lo)
    cum, total = dall[0:rows] + d_lo[0:rows], dall[rows:2 * rows] + d_lo[rows:2 * rows]
    group_total = (lax.dot_general(hi, grp, TN_DIMS, preferred_element_type=F32)
                   + lax.dot_general(lo, grp, TN_DIMS, preferred_element_type=F32))
    nt = lambda a, b: lax.dot_general(a.astype(BF16), b.astype(BF16), NT_DIMS, preferred_element_type=F32)
    sc = jnp.where(lid == n_levels, nt(q, k), 0.0)
    for l in range(n_levels):
        e = jnp.exp(dall[(2 + l) * rows:(3 + l) * rows])
        sc = jnp.where(lid == l, nt(q * e, k * e), sc)
    return sc.astype(BF16), cum, total, group_total


def _gla_log_decay(gl, wgk, bg_row):
    return _log_sigmoid(_dot(gl, wgk) + bg_row) * (1.0 / GLA_GATE_NORMALIZER)


def _rms_gate(o, ng, gate):
    on = o * lax.rsqrt(jnp.mean(o * o, axis=-1, keepdims=True) + RMS_EPS) * ng
    return (on * _silu(gate)).astype(BF16)


def _gla_prompt_kernel(q_ref, k_ref, v_ref, g_ref, gl_ref, wgk_ref, bgr_ref, ng_ref, stack_ref, lid_ref, grp_ref,
                       y_ref, sout_ref, st_ref, *, n_heads, scale, n_levels):
    j = pl.program_id(1)

    @pl.when(j == 0)
    def _():
        st_ref[...] = jnp.zeros_like(st_ref)

    dk = q_ref.shape[1] // n_heads
    dv = v_ref.shape[1] // n_heads
    gl = gl_ref[...]
    for h in range(n_heads):
        ks, vs = slice(h * dk, (h + 1) * dk), slice(h * dv, (h + 1) * dv)
        q = q_ref[:, ks].astype(F32) * scale
        k = k_ref[:, ks].astype(F32)
        v = v_ref[:, vs]
        la = _gla_log_decay(gl, wgk_ref[:, ks], bgr_ref[:, ks])
        sc, cum, total, group_total = _gla_block(q, k, la, stack_ref[...], lid_ref[...], grp_ref[...], n_levels)
        s_old = st_ref[h]
        o = _dot(sc, v) + _dot((q * jnp.exp(cum)).astype(BF16), s_old.astype(BF16))
        kd = (k * jnp.exp(total - cum)).astype(BF16)
        st_ref[h] = s_old * jnp.exp(group_total[:, 0:1]) + lax.dot_general(
            kd, v, TN_DIMS, preferred_element_type=F32)
        y_ref[:, vs] = _rms_gate(o, ng_ref[...], g_ref[:, vs].astype(F32))

    @pl.when(j == pl.num_programs(1) - 1)
    def _():
        sout_ref[...] = st_ref[...]


def _gla_sample_kernel(q_ref, k_ref, v_ref, g_ref, gl_ref, wgk_ref, bgr_ref, ng_ref, stack_ref, lid_ref, grp_ref,
                       sin_ref, y_ref, sout_ref, *, n_heads, scale, seq, n_levels):
    rows = q_ref.shape[0]
    nb = rows // seq
    dk = q_ref.shape[1] // n_heads
    dv = v_ref.shape[1] // n_heads
    gl = gl_ref[...]
    for h in range(n_heads):
        ks, vs = slice(h * dk, (h + 1) * dk), slice(h * dv, (h + 1) * dv)
        q = q_ref[:, ks].astype(F32) * scale
        k = k_ref[:, ks].astype(F32)
        v = v_ref[:, vs]
        la = _gla_log_decay(gl, wgk_ref[:, ks], bgr_ref[:, ks])
        sc, cum, total, group_total = _gla_block(q, k, la, stack_ref[...], lid_ref[...], grp_ref[...], n_levels)
        q_dec = q * jnp.exp(cum)
        kd = k * jnp.exp(total - cum)
        v32 = v.astype(F32)
        o_state = []
        for s in range(nb):
            rs = slice(s * seq, (s + 1) * seq)
            s_old = sin_ref[s, h]
            o_state.append(_dot(q_dec[rs, :].astype(BF16), s_old.astype(BF16)))
            sout_ref[s, h] = s_old * jnp.exp(group_total[:, s:s + 1]) + lax.dot_general(
                kd[rs, :].astype(BF16), v32[rs, :].astype(BF16), TN_DIMS, preferred_element_type=F32)
        o = _dot(sc, v) + jnp.concatenate(o_state, axis=0)
        y_ref[:, vs] = _rms_gate(o, ng_ref[...], g_ref[:, vs].astype(F32))


def _gla_kernel(*refs, n_prompt_in, n_sample_in, prompt_args, sample_args):
    p_in, s_in = refs[:n_prompt_in], refs[n_prompt_in:n_prompt_in + n_sample_in]
    yp_ref, sp_ref, ys_ref, ss_ref, st_ref = refs[n_prompt_in + n_sample_in:]
    _gla_prompt_kernel(*p_in, yp_ref, sp_ref, st_ref, **prompt_args)
    _gla_sample_kernel(*s_in, ys_ref, ss_ref, **sample_args)


def _gla(p, gl, wgk, bg_row, ng, state_s, *, t_prompt, seq, n_heads, dk, dv, bk=256):
    t = p.shape[0]
    bs, _, _, _ = state_s.shape
    s_len = (t - t_prompt) // bs
    bp = t_prompt // seq
    dkt, dvt = n_heads * dk, n_heads * dv
    assert dvt == 2 * dkt
    scale = dk ** -0.5
    r = gl.shape[1]
    const = lambda *_: (0, 0)

    def tables(rows, span):
        stack, lid, grp, n_levels = _gla_tables(rows, span)
        specs = [pl.BlockSpec((r, dkt), const), pl.BlockSpec((1, dkt), const), pl.BlockSpec((1, dv), const),
                 pl.BlockSpec(stack.shape, const), pl.BlockSpec(lid.shape, const), pl.BlockSpec(grp.shape, const)]
        return specs, [wgk, bg_row, ng, stack, lid, grp], n_levels

    nblk = seq // bk
    n_steps = bp * nblk
    assert bs % n_steps == 0
    nb = bs // n_steps
    rows = nb * s_len
    off = t_prompt // rows
    rowp = lambda b, j: b * nblk + j
    rows_s = lambda b, j: off + rowp(b, j)
    p_specs, p_ins, levels_p = tables(bk, bk)
    s_specs, s_ins, levels_s = tables(rows, s_len)
    prompt_in = [pl.BlockSpec((bk, dkt), lambda b, j: (rowp(b, j), 0)),
                 pl.BlockSpec((bk, dkt), lambda b, j: (rowp(b, j), 1)),
                 pl.BlockSpec((bk, dvt), lambda b, j: (rowp(b, j), 1)),
                 pl.BlockSpec((bk, dvt), lambda b, j: (rowp(b, j), 2)),
                 pl.BlockSpec((bk, r), lambda b, j: (rowp(b, j), 0))] + p_specs
    sample_in = [pl.BlockSpec((rows, dkt), lambda b, j: (rows_s(b, j), 0)),
                 pl.BlockSpec((rows, dkt), lambda b, j: (rows_s(b, j), 1)),
                 pl.BlockSpec((rows, dvt), lambda b, j: (rows_s(b, j), 1)),
                 pl.BlockSpec((rows, dvt), lambda b, j: (rows_s(b, j), 2)),
                 pl.BlockSpec((rows, r), lambda b, j: (rows_s(b, j), 0))] + s_specs + [
                 pl.BlockSpec((nb, n_heads, dk, dv), lambda b, j: (rowp(b, j), 0, 0, 0))]
    y_p, s_p, y_s, s_s = pl.pallas_call(
        functools.partial(
            _gla_kernel, n_prompt_in=len(prompt_in), n_sample_in=len(sample_in),
            prompt_args=dict(n_heads=n_heads, scale=scale, n_levels=levels_p),
            sample_args=dict(n_heads=n_heads, scale=scale, seq=s_len, n_levels=levels_s)),
        grid=(bp, nblk),
        in_specs=prompt_in + sample_in,
        out_specs=[pl.BlockSpec((bk, dvt), lambda b, j: (rowp(b, j), 0)),
                   pl.BlockSpec((None, n_heads, dk, dv), lambda b, j: (b, 0, 0, 0)),
                   pl.BlockSpec((rows, dvt), lambda b, j: (rowp(b, j), 0)),
                   pl.BlockSpec((nb, n_heads, dk, dv), lambda b, j: (rowp(b, j), 0, 0, 0))],
        out_shape=[jax.ShapeDtypeStruct((t_prompt, dvt), BF16),
                   jax.ShapeDtypeStruct((bp, n_heads, dk, dv), F32),
                   jax.ShapeDtypeStruct((t - t_prompt, dvt), BF16),
                   jax.ShapeDtypeStruct(state_s.shape, F32)],
        scratch_shapes=[pltpu.VMEM((n_heads, dk, dv), F32)],
        compiler_params=_params("arbitrary", "arbitrary"),
        name="gla",
    )(p, p, p, p, gl, *p_ins, p, p, p, p, gl, *s_ins, state_s)
    return (y_p, y_s), s_p, s_s


def _moe_schedule(eid, rank, counts, *, tg, n_tiles):
    n_e = counts.shape[0]
    e_ids = jnp.arange(n_e, dtype=I32)
    padded = ((counts + tg - 1) // tg) * tg
    ends = jnp.sum(jnp.where(e_ids[None, :] <= e_ids[:, None], padded[None, :], 0), axis=1)
    starts = ends - padded
    pos = jnp.sum(jnp.where(eid[None] == e_ids[:, None, None], starts[:, None, None], 0), axis=0) + rank
    n_used = ends[n_e - 1] // tg
    tile_start = jnp.arange(n_tiles, dtype=I32) * tg
    te = jnp.sum((ends[None, :] <= tile_start[:, None]).astype(I32), axis=1)
    te_last = jnp.sum((ends <= (n_used - 1) * tg).astype(I32))
    te = jnp.where(jnp.arange(n_tiles) < n_used, te, te_last)
    later = (e_ids[None, :] > e_ids[:, None]) & (counts[None, :] > 0)
    nxt = jnp.min(jnp.where(later, e_ids[None, :], n_e), axis=1)
    nxt = jnp.where(nxt == n_e, -1, nxt)
    i32 = lambda a: a.astype(I32)
    return (i32(pos.reshape(-1)), i32(starts + counts), i32(padded - counts), i32(te), i32(nxt),
            i32(n_used.reshape(1)))


def kernel(x_prompt, x_sample, cache_conv, state_gla, c_prompt, c_sample, w_mod, b_mod, ln_g, ln_b, ab_w_in, ab_conv_w, ab_v_ln_g, ab_v_ln_b, ab_w_s, ab_b_s, ab_w_out, gla_w_in, gla_w_gk, gla_b_gk, gla_norm_g, gla_w_out, moe_w_grp, moe_b_grp, moe_w_rt, moe_b_rt, moe_w1, moe_w3, moe_w2):
    bp, seq, d = x_prompt.shape
    bs, s_len, _ = x_sample.shape
    assert s_len == SUBLANES_V7X and seq % SUBLANES_V7X == 0
    depth = w_mod.shape[0]
    alpha = float((2 * depth) ** 0.25)
    t_p, t_s = bp * seq, bs * s_len
    t = t_p + t_s
    n_groups, n_exp = moe_w_rt.shape[1], moe_w_rt.shape[3]
    n_e = n_groups * n_exp
    d_ff = moe_w1.shape[-1]
    tg = 256
    n_tiles = (TOP_K_INNER * t) // tg + n_e

    x = (x_prompt.reshape(t_p // SUBLANES_V7X, SUBLANES_V7X, d), x_sample)
    mods = _mod_vectors(c_prompt, c_sample, w_mod, b_mod)

    w1 = moe_w1.reshape(depth * n_e, d, d_ff)
    w3 = moe_w3.reshape(depth * n_e, d, d_ff)
    w2 = moe_w2.reshape(depth * n_e, d_ff, d)

    conv_p, conv_s, chunk_v, gla_p, gla_s = [], [], [], [], []
    h_bf = None
    for layer in range(depth):
        li = layer // 2
        if layer % 2 == 0:
            n_heads, chunk = ab_w_s.shape[1], ab_w_s.shape[2]
            dc = ab_conv_w.shape[-1]
            if h_bf is None:
                p = _mm(_modulate(x, mods, layer, 0, 1, seq=seq), ab_w_in, li, ab_w_in.shape[-1])
            else:
                p = _mm(h_bf, ab_w_in, li, ab_w_in.shape[-1])
            w_s = ab_w_s[li]
            wm_p = jnp.tril(w_s)
            reps = chunk // s_len
            blk = jnp.tril(w_s[:, :s_len, :s_len])
            wm_s = jnp.einsum("ab,hts->hatbs", jnp.eye(reps, dtype=F32), blk).reshape(n_heads, chunk, chunk)
            wm = jnp.stack([wm_p, wm_s]).astype(BF16)
            b_s = ab_b_s[li]
            hd = dc // n_heads
            bias_p = jnp.repeat(b_s.T, hd, axis=1)
            bias_s = jnp.repeat(jnp.tile(b_s[:, :s_len].T, (reps, 1)), hd, axis=1)
            bias = jnp.stack([bias_p, bias_s])
            y, cp_new, cs_new, vn_s = _mix0(p, cache_conv[li], ab_conv_w[li], ab_v_ln_g[li], ab_v_ln_b[li],
                                            wm, bias, t_prompt=t_p, seq=seq, n_heads=n_heads)
            conv_p.append(cp_new)
            conv_s.append(cs_new)
            chunk_v.append(vn_s)
            w_out = ab_w_out[li].astype(BF16)
        else:
            n_heads, dk, dv = state_gla.shape[2], state_gla.shape[3], state_gla.shape[4]
            dkt, dvt = n_heads * dk, n_heads * dv
            rank = gla_w_gk.shape[1]
            n_main = 2 * dkt + 2 * dvt
            w_in_t = jnp.swapaxes(gla_w_in, 1, 2)
            p = _mm(h_bf, w_in_t, li, n_main, w_transposed=True)
            w_lo = jnp.pad(w_in_t[li, n_main:, :], ((0, LANES_V7X - rank), (0, 0)))[None]
            gl = _mm(h_bf, w_lo, 0, LANES_V7X, w_transposed=True)
            wgk = jnp.pad(gla_w_gk[li], ((0, LANES_V7X - rank), (0, 0))).astype(BF16)
            y, sp_new, ss_new = _gla(p, gl, wgk, gla_b_gk[li].reshape(1, dkt),
                                     gla_norm_g[li].reshape(1, dv), state_gla[li],
                                     t_prompt=t_p, seq=seq, n_heads=n_heads, dk=dk, dv=dv)
            gla_p.append(sp_new)
            gla_s.append(ss_new)
            w_out = gla_w_out[li].astype(BF16)

        wr = jnp.concatenate([moe_w_grp[layer].T,
                              jnp.transpose(moe_w_rt[layer], (0, 2, 1)).reshape(n_e, d)], axis=0)
        wr = jnp.pad(wr, ((0, LANES_V7X - wr.shape[0]), (0, 0)))
        br = jnp.concatenate([moe_b_grp[layer], moe_b_rt[layer].reshape(n_e)])
        br = jnp.pad(br, (0, LANES_V7X - br.shape[0])).reshape(LANES_V7X, 1)
        x1, h2, eid, wt, rank_, cnt = _outln(y, w_out, x, mods, layer, ln_g[layer, 0], ln_b[layer, 0], wr, br,
                                             alpha=alpha, n_groups=n_groups, n_exp=n_exp, t_prompt=t_p, seq=seq)
        pos, zero_start, zero_count, te, nxt, n_used = _moe_schedule(eid, rank_, cnt[:, 0], tg=tg, n_tiles=n_tiles)
        xs = _dispatch(pos, zero_start, zero_count, n_used, h2, n_tiles=n_tiles, tg=tg)
        ys = _moe(xs, w1, w3, w2, layer * n_e, te, nxt, n_used, tg=tg)
        has_next = layer + 1 < depth
        outs = _comb(pos, ys, wt.T, x1, mods, layer, ln_g[layer, 1], ln_b[layer, 1],
                     alpha=alpha, has_next=has_next, t_prompt=t_p, seq=seq)
        if has_next:
            x, h_bf = outs

    y_prompt = outs[0].reshape(bp, seq, d)
    y_sample = outs[1].reshape(bs, s_len, d)
    return (y_prompt, y_sample, jnp.stack(conv_p), jnp.stack(conv_s), jnp.stack(chunk_v),
            jnp.stack(gla_p), jnp.stack(gla_s))
```
